```python
import math, functools
import jax, jax.numpy as jnp
from jax import lax
import numpy as np

D_MODEL = 2048
BATCH = 1
SEQ = 8192
DEPTH = 2
DEC_BATCH = 32
DEC_SEQ = 8
PAST_LEN = 8192
PAGE_SIZE = 128

DA_HEADS = 8
DA_DK = 64
DA_DV = 2 * DA_DK
DA_QW = DA_HEADS * 2 * DA_DK
DA_VW = DA_HEADS * DA_DV
DA_SCALE = DA_DK ** -0.5
Q_BLOCK = 128
N_BUCKETS = 32
MAX_DISTANCE = 128
SSM_INNER = D_MODEL
SSM_HEAD_DIM = 64
SSM_HEADS = SSM_INNER // SSM_HEAD_DIM
SSM_GROUPS = 4
SSM_STATE = 128
SSM_CONV = 4
SSM_CHUNK = 128
SSM_GN = SSM_GROUPS * SSM_STATE
SSM_CONV_DIM = SSM_INNER + 2 * SSM_GN
MEM_TOKENS = 256
MEM_HEADS = 4
MEM_HEAD_DIM = 256
MEM_W = MEM_HEADS * MEM_HEAD_DIM
MEM_SCALE = MEM_HEAD_DIM ** -0.5
PEER_HEADS = 8
PEER_KEYS = 128
PEER_EXPERTS = PEER_KEYS * PEER_KEYS
PEER_TOPK = 16
PEER_HALF = 128
PEER_QDIM = 2 * PEER_HALF
PEER_BLOCK = 128
IN_SIZES = (DA_QW, DA_QW, DA_VW, SSM_INNER, SSM_CONV_DIM, SSM_HEADS, MEM_W, 3 * D_MODEL)
IN_W = 2 * DA_QW + DA_VW + SSM_INNER + SSM_CONV_DIM + SSM_HEADS + MEM_W + 3 * D_MODEL
NORM_EPS = 1e-6

kernel_name = 'hybrid_diffattn_ssd_peer_decoder_step'


def rmsnorm(x, g):
    xf = x.astype(jnp.float32)
    y = xf * lax.rsqrt(jnp.mean(xf * xf, axis=-1, keepdims=True) + NORM_EPS)
    return (y * g.astype(jnp.float32)).astype(x.dtype)


def split_cols(u, sizes):
    out, start = [], 0
    for s in sizes:
        out.append(u[..., start:start + s])
        start += s
    return out


def t5_bias(dist, table):
    n = jnp.maximum(dist, 0)
    exact = N_BUCKETS // 2
    nf = jnp.maximum(n, exact).astype(jnp.float32)
    large = exact + (jnp.log(nf / exact) / math.log(MAX_DISTANCE / exact) * (N_BUCKETS - exact)).astype(jnp.int32)
    bucket = jnp.where(n < exact, n, jnp.minimum(large, N_BUCKETS - 1))
    return jnp.moveaxis(table.astype(jnp.float32)[bucket], -1, 0)


def diff_attn_prompt(q, k, v, lam, bias_table):
    b, S = q.shape[:2]
    nb = S // Q_BLOCK
    qb = jnp.moveaxis(q.reshape(b, nb, Q_BLOCK, DA_HEADS, 2, DA_DK), 1, 0)
    kpos = jnp.arange(S, dtype=jnp.int32)

    def block(args):
        qi, i = args
        qpos = i * Q_BLOCK + jnp.arange(Q_BLOCK, dtype=jnp.int32)
        dist = qpos[:, None] - kpos[None, :]
        logits = jnp.einsum('bqhmd,bkhmd->bhmqk', qi, k, preferred_element_type=jnp.float32) * DA_SCALE
        logits = logits + t5_bias(dist, bias_table)[None, :, None]
        logits = jnp.where(dist >= 0, logits, -jnp.inf)
        prob = jax.nn.softmax(logits, axis=-1)
        attn = prob[:, :, 0] - lam * prob[:, :, 1]
        return jnp.einsum('bhqk,bkhd->bqhd', attn.astype(v.dtype), v)

    o = lax.map(block, (qb, jnp.arange(nb, dtype=jnp.int32)))
    return jnp.moveaxis(o, 0, 1).reshape(b, S, DA_HEADS, DA_DV)


def diff_attn_sample(q, k, v, lam, k_pool, v_pool, layer, page_table, bias_table):
    b, T = q.shape[:2]
    past = page_table.shape[1] * k_pool.shape[2]
    k_past = k_pool[layer, page_table].reshape(b, past, DA_HEADS, 2, DA_DK)
    v_past = v_pool[layer, page_table].reshape(b, past, DA_HEADS, DA_DV)
    tpos = jnp.arange(T, dtype=jnp.int32)
    dist_past = (past + tpos)[:, None] - jnp.arange(past, dtype=jnp.int32)[None, :]
    dist_new = tpos[:, None] - tpos[None, :]
    lp = jnp.einsum('bqhmd,bkhmd->bhmqk', q, k_past, preferred_element_type=jnp.float32) * DA_SCALE
    lp = lp + t5_bias(dist_past, bias_table)[None, :, None]
    ln = jnp.einsum('bqhmd,bkhmd->bhmqk', q, k, preferred_element_type=jnp.float32) * DA_SCALE
    ln = jnp.where(dist_new >= 0, ln + t5_bias(dist_new, bias_table)[None, :, None], -jnp.inf)
    prob = jax.nn.softmax(jnp.concatenate([lp, ln], axis=-1), axis=-1)
    attn = (prob[:, :, 0] - lam * prob[:, :, 1]).astype(v.dtype)
    return (jnp.einsum('bhqk,bkhd->bqhd', attn[..., :past], v_past)
            + jnp.einsum('bhqk,bkhd->bqhd', attn[..., past:], v))


def memory_kv(mem, l, p):
    b, m = mem.shape[:2]
    mk, mv = split_cols(rmsnorm(mem, p['g_mem'][l]) @ p['w_mem_kv'][l], (MEM_W, MEM_W))
    mk = rmsnorm(mk.reshape(b, m, MEM_HEADS, MEM_HEAD_DIM), p['g_mk'][l])
    return mk, mv.reshape(b, m, MEM_HEADS, MEM_HEAD_DIM)


def memory_attend(q, mk, mv):
    logits = jnp.einsum('bqhd,bmhd->bhqm', q, mk, preferred_element_type=jnp.float32) * MEM_SCALE
    prob = jax.nn.softmax(logits, axis=-1)
    return jnp.einsum('bhqm,bmhd->bqhd', prob.astype(mv.dtype), mv)


def ssd_scan(xh, dt, a, bm, cm, h0):
    b, L = xh.shape[:2]
    q = min(SSM_CHUNK, L)
    pad = (-L) % q
    if pad:
        padf = lambda t: jnp.pad(t, [(0, 0), (0, pad)] + [(0, 0)] * (t.ndim - 2))
        xh, dt, bm, cm = padf(xh), padf(dt), padf(bm), padf(cm)
    c = (L + pad) // q
    hpg = SSM_HEADS // SSM_GROUPS
    xdt = (xh * dt[..., None]).reshape(b, c, q, SSM_GROUPS, hpg, SSM_HEAD_DIM)
    acs = jnp.cumsum((dt * a).reshape(b, c, q, SSM_GROUPS, hpg), axis=2)
    bc = bm.reshape(b, c, q, SSM_GROUPS, SSM_STATE)
    cc = cm.reshape(b, c, q, SSM_GROUPS, SSM_STATE)
    causal = jnp.tril(jnp.ones((q, q), dtype=bool))[:, :, None, None]
    seg = acs[:, :, :, None] - acs[:, :, None, :]
    lmat = jnp.exp(jnp.where(causal, seg, -jnp.inf))
    cb = jnp.einsum('bclgn,bcsgn->bclsg', cc, bc)
    y_diag = jnp.einsum('bclsgh,bcsghp->bclghp', cb[..., None] * lmat, xdt)
    decay_s = jnp.exp(acs[:, :, -1:] - acs)
    states = jnp.einsum('bcsgn,bcsghp->bcghpn', bc, decay_s[..., None] * xdt)
    chunk_decay = jnp.exp(acs[:, :, -1])

    def step(h, inp):
        dec, st = inp
        return h * dec[..., None, None] + st, h

    h_last, h_in = lax.scan(step, h0.reshape(b, SSM_GROUPS, hpg, SSM_HEAD_DIM, SSM_STATE),
                            (jnp.moveaxis(chunk_decay, 1, 0), jnp.moveaxis(states, 1, 0)))
    h_in = jnp.moveaxis(h_in, 0, 1)
    y_off = jnp.einsum('bclgn,bcghpn->bclghp', cc, h_in) * jnp.exp(acs)[..., None]
    y = (y_diag + y_off).reshape(b, c * q, SSM_HEADS, SSM_HEAD_DIM)[:, :L]
    return y, h_last.reshape(b, SSM_HEADS, SSM_HEAD_DIM, SSM_STATE)


def mamba_branch(z, xbc, dt_raw, conv_prev, h0, l, p):
    b, L = xbc.shape[:2]
    f32 = jnp.float32
    xpad = jnp.concatenate([conv_prev.astype(xbc.dtype), xbc], axis=1)
    w = p['conv_w'][l]
    conv = p['conv_b'][l] + w[0] * xpad[:, 0:L]
    for j in range(1, SSM_CONV):
        conv = conv + w[j] * xpad[:, j:j + L]
    conv_new = xpad[:, L:]
    xc = jax.nn.silu(conv)
    xs, bm, cm = split_cols(xc, (SSM_INNER, SSM_GN, SSM_GN))
    xh = xs.reshape(b, L, SSM_HEADS, SSM_HEAD_DIM).astype(f32)
    dt = jax.nn.softplus(dt_raw.astype(f32) + p['dt_bias'][l].astype(f32))
    a = -jnp.exp(p['a_log'][l].astype(f32))
    y, h_fin = ssd_scan(xh, dt, a,
                        bm.reshape(b, L, SSM_GROUPS, SSM_STATE).astype(f32),
                        cm.reshape(b, L, SSM_GROUPS, SSM_STATE).astype(f32),
                        h0.astype(f32))
    y = y + p['d_skip'][l].astype(f32)[:, None] * xh
    y = y.reshape(b, L, SSM_INNER).astype(z.dtype) * jax.nn.silu(z)
    y = rmsnorm(y.reshape(b, L, SSM_GROUPS, SSM_INNER // SSM_GROUPS),
                p['g_ssm'][l].reshape(SSM_GROUPS, SSM_INNER // SSM_GROUPS))
    return y.reshape(b, L, SSM_INNER), h_fin, conv_new


def peer_ffn(h, l, p):
    b, L, D = h.shape
    T = b * L
    nb = -(-T // PEER_BLOCK)
    hb = jnp.pad(h.reshape(T, D), ((0, nb * PEER_BLOCK - T), (0, 0))).reshape(nb, PEER_BLOCK, D)
    wq, k1, k2 = p['peer_wq'][l], p['peer_k1'][l], p['peer_k2'][l]
    u_tab, v_tab = p['peer_u'], p['peer_v']

    def block(xt):
        q = (xt @ wq).reshape(PEER_BLOCK, PEER_HEADS, 2, PEER_HALF)
        s1 = jnp.einsum('thd,kd->thk', q[:, :, 0], k1, preferred_element_type=jnp.float32)
        s2 = jnp.einsum('thd,kd->thk', q[:, :, 1], k2, preferred_element_type=jnp.float32)
        v1, i1 = lax.top_k(s1, PEER_TOPK)
        v2, i2 = lax.top_k(s2, PEER_TOPK)
        cand = (v1[..., :, None] + v2[..., None, :]).reshape(PEER_BLOCK, PEER_HEADS, PEER_TOPK * PEER_TOPK)
        sc, ci = lax.top_k(cand, PEER_TOPK)
        e = (jnp.take_along_axis(i1, ci // PEER_TOPK, axis=-1) * PEER_KEYS
             + jnp.take_along_axis(i2, ci % PEER_TOPK, axis=-1))
        gate = jax.nn.softmax(sc, axis=-1)
        u = u_tab[l, e]
        act = jax.nn.gelu(jnp.einsum('td,thkd->thk', xt, u, preferred_element_type=jnp.float32), approximate=False)
        return jnp.einsum('thk,thkd->td', (gate * act).astype(xt.dtype), v_tab[l, e])

    out = lax.map(block, hb).reshape(nb * PEER_BLOCK, D)[:T]
    return out.reshape(b, L, D)


def decoder_layer(x, l, p, attend, mem_k, mem_v, conv_prev, h0):
    b, L, _ = x.shape
    f32 = jnp.float32
    lam_init = 0.8 - 0.6 * math.exp(-0.3 * l)
    h = rmsnorm(x, p['g_mix'][l])
    q, k, v, z, xbc, dt_raw, mq, gates = split_cols(h @ p['w_in'][l], IN_SIZES)
    q = rmsnorm(q.reshape(b, L, DA_HEADS, 2, DA_DK), p['g_q'][l])
    k = rmsnorm(k.reshape(b, L, DA_HEADS, 2, DA_DK), p['g_k'][l])
    v = v.reshape(b, L, DA_HEADS, DA_DV)
    lam = (jnp.exp(jnp.sum(p['lam_q1'][l].astype(f32) * p['lam_k1'][l].astype(f32)))
           - jnp.exp(jnp.sum(p['lam_q2'][l].astype(f32) * p['lam_k2'][l].astype(f32))) + lam_init)
    o = attend(q, k, v, lam)
    a_out = (rmsnorm(o, p['g_sub'][l]) * (1.0 - lam_init)).reshape(b, L, DA_VW)
    s_out, h_fin, conv_new = mamba_branch(z, xbc, dt_raw, conv_prev, h0, l, p)
    mq = rmsnorm(mq.reshape(b, L, MEM_HEADS, MEM_HEAD_DIM), p['g_mq'][l])
    m_out = memory_attend(mq, mem_k, mem_v).reshape(b, L, MEM_W)
    g_a, g_s, g_m = split_cols(jax.nn.sigmoid(gates), (D_MODEL, D_MODEL, D_MODEL))
    merged = (g_a * (a_out @ p['w_br_attn'][l]) + g_s * (s_out @ p['w_br_ssm'][l])
              + g_m * (m_out @ p['w_br_mem'][l]))
    x = x + merged @ p['w_out'][l]
    x = x + peer_ffn(rmsnorm(x, p['g_ffn'][l]), l, p)
    return x, k, v, h_fin, conv_new


def setup_inputs(seed: int = 0) -> dict:
    key = jax.random.key(seed)
    ks = iter(jax.random.split(key, 64))
    f32 = jnp.float32
    nrm = lambda shape, scale: jax.random.normal(next(ks), shape, f32) * scale
    gain = lambda shape: 1.0 + 0.05 * jax.random.normal(next(ks), shape, f32)
    n_pages = PAST_LEN // PAGE_SIZE
    n_pool = (5 * DEC_BATCH * n_pages + 3) // 4
    page_table = jax.random.permutation(next(ks), n_pool)[:DEC_BATCH * n_pages].reshape(DEC_BATCH, n_pages).astype(jnp.int32)
    dt0 = jnp.exp(jax.random.uniform(next(ks), (DEPTH, SSM_HEADS), f32, math.log(1e-3), math.log(1e-1)))
    dt_bias = dt0 + jnp.log(-jnp.expm1(-dt0))
    a_log = jnp.log(jax.random.uniform(next(ks), (DEPTH, SSM_HEADS), f32, 1.0, 16.0))
    return {
        'x_prompt': nrm((BATCH, SEQ, D_MODEL), 1.0),
        'x_sample': nrm((DEC_BATCH, DEC_SEQ, D_MODEL), 1.0),
        'cache_attn_k': nrm((DEPTH, n_pool, PAGE_SIZE, DA_HEADS, 2, DA_DK), 1.0),
        'cache_attn_v': nrm((DEPTH, n_pool, PAGE_SIZE, DA_HEADS, DA_DV), 1.0),
        'cache_mem_k': nrm((DEPTH, DEC_BATCH, MEM_TOKENS, MEM_HEADS, MEM_HEAD_DIM), 1.0),
        'cache_mem_v': nrm((DEPTH, DEC_BATCH, MEM_TOKENS, MEM_HEADS, MEM_HEAD_DIM), 1.0),
        'state_ssm': nrm((DEPTH, DEC_BATCH, SSM_HEADS, SSM_HEAD_DIM, SSM_STATE), 0.5),
        'state_conv': nrm((DEPTH, DEC_BATCH, SSM_CONV - 1, SSM_CONV_DIM), 1.0),
        'page_table': page_table,
        'mem_prompt': nrm((BATCH, MEM_TOKENS, D_MODEL), 1.0),
        'rel_bias': nrm((N_BUCKETS, DA_HEADS), 0.5),
        'g_mix': gain((DEPTH, D_MODEL)),
        'w_in': nrm((DEPTH, D_MODEL, IN_W), D_MODEL ** -0.5),
        'g_q': gain((DEPTH, DA_DK)),
        'g_k': gain((DEPTH, DA_DK)),
        'lam_q1': nrm((DEPTH, DA_DK), 0.1),
        'lam_k1': nrm((DEPTH, DA_DK), 0.1),
        'lam_q2': nrm((DEPTH, DA_DK), 0.1),
        'lam_k2': nrm((DEPTH, DA_DK), 0.1),
        'g_sub': gain((DEPTH, DA_DV)),
        'conv_w': nrm((DEPTH, SSM_CONV, SSM_CONV_DIM), SSM_CONV ** -0.5),
        'conv_b': nrm((DEPTH, SSM_CONV_DIM), 0.02),
        'dt_bias': dt_bias,
        'a_log': a_log,
        'd_skip': gain((DEPTH, SSM_HEADS)),
        'g_ssm': gain((DEPTH, SSM_INNER)),
        'g_mem': gain((DEPTH, D_MODEL)),
        'w_mem_kv': nrm((DEPTH, D_MODEL, 2 * MEM_W), D_MODEL ** -0.5),
        'g_mq': gain((DEPTH, MEM_HEAD_DIM)),
        'g_mk': gain((DEPTH, MEM_HEAD_DIM)),
        'w_br_attn': nrm((DEPTH, DA_VW, D_MODEL), DA_VW ** -0.5),
        'w_br_ssm': nrm((DEPTH, SSM_INNER, D_MODEL), SSM_INNER ** -0.5),
        'w_br_mem': nrm((DEPTH, MEM_W, D_MODEL), MEM_W ** -0.5),
        'w_out': nrm((DEPTH, D_MODEL, D_MODEL), D_MODEL ** -0.5),
        'g_ffn': gain((DEPTH, D_MODEL)),
        'peer_wq': nrm((DEPTH, D_MODEL, PEER_HEADS * PEER_QDIM), D_MODEL ** -0.5),
        'peer_k1': nrm((DEPTH, PEER_KEYS, PEER_HALF), PEER_HALF ** -0.5),
        'peer_k2': nrm((DEPTH, PEER_KEYS, PEER_HALF), PEER_HALF ** -0.5),
        'peer_u': nrm((DEPTH, PEER_EXPERTS, D_MODEL), D_MODEL ** -0.5),
        'peer_v': nrm((DEPTH, PEER_EXPERTS, D_MODEL), 0.3),
    }


def reference(x_prompt, x_sample, cache_attn_k, cache_attn_v, cache_mem_k, cache_mem_v, state_ssm, state_conv,
              page_table, mem_prompt, rel_bias, g_mix, w_in, g_q, g_k, lam_q1, lam_k1, lam_q2, lam_k2, g_sub,
              conv_w, conv_b, dt_bias, a_log, d_skip, g_ssm, g_mem, w_mem_kv, g_mq, g_mk,
              w_br_attn, w_br_ssm, w_br_mem, w_out, g_ffn, peer_wq, peer_k1, peer_k2, peer_u, peer_v):
    p = dict(g_mix=g_mix, w_in=w_in, g_q=g_q, g_k=g_k, lam_q1=lam_q1, lam_k1=lam_k1, lam_q2=lam_q2,
             lam_k2=lam_k2, g_sub=g_sub, conv_w=conv_w, conv_b=conv_b, dt_bias=dt_bias, a_log=a_log,
             d_skip=d_skip, g_ssm=g_ssm, g_mem=g_mem, w_mem_kv=w_mem_kv, g_mq=g_mq, g_mk=g_mk,
             w_br_attn=w_br_attn, w_br_ssm=w_br_ssm, w_br_mem=w_br_mem, w_out=w_out, g_ffn=g_ffn,
             peer_wq=peer_wq, peer_k1=peer_k1, peer_k2=peer_k2, peer_u=peer_u, peer_v=peer_v)

    b = x_prompt.shape[0]
    zero_conv = jnp.zeros((b, SSM_CONV - 1, SSM_CONV_DIM), x_prompt.dtype)
    zero_h = jnp.zeros((b, SSM_HEADS, SSM_HEAD_DIM, SSM_STATE), jnp.float32)
    prompt_attend = functools.partial(diff_attn_prompt, bias_table=rel_bias)
    y_prompt = x_prompt
    pk, pv, pmk, pmv, ph, pc = [], [], [], [], [], []
    for l in range(DEPTH):
        mk, mv = memory_kv(mem_prompt, l, p)
        y_prompt, k_l, v_l, h_l, c_l = decoder_layer(y_prompt, l, p, prompt_attend, mk, mv, zero_conv, zero_h)
        pk.append(k_l); pv.append(v_l); pmk.append(mk); pmv.append(mv)
        ph.append(h_l.astype(x_prompt.dtype)); pc.append(c_l)

    y_sample = x_sample
    sk, sv, sh, sc = [], [], [], []
    for l in range(DEPTH):
        sample_attend = functools.partial(diff_attn_sample, k_pool=cache_attn_k, v_pool=cache_attn_v, layer=l,
                                          page_table=page_table, bias_table=rel_bias)
        y_sample, k_l, v_l, h_l, c_l = decoder_layer(y_sample, l, p, sample_attend, cache_mem_k[l], cache_mem_v[l],
                                                     state_conv[l], state_ssm[l])
        sk.append(k_l); sv.append(v_l); sh.append(h_l.astype(state_ssm.dtype)); sc.append(c_l)

    new_attn_k_prompt = jnp.stack(pk)
    new_attn_v_prompt = jnp.stack(pv)
    new_mem_k_prompt = jnp.stack(pmk)
    new_mem_v_prompt = jnp.stack(pmv)
    new_ssm_prompt = jnp.stack(ph)
    new_conv_prompt = jnp.stack(pc)
    new_attn_k_sample = jnp.stack(sk)
    new_attn_v_sample = jnp.stack(sv)
    new_ssm_sample = jnp.stack(sh)
    new_conv_sample = jnp.stack(sc)
    return (y_prompt, y_sample, new_attn_k_prompt, new_attn_v_prompt, new_mem_k_prompt, new_mem_v_prompt,
            new_ssm_prompt, new_conv_prompt, new_attn_k_sample, new_attn_v_sample, new_ssm_sample, new_conv_sample)
```

```python
import functools
import math

import jax
import jax.numpy as jnp
from jax import lax
from jax.experimental import pallas as pl
from jax.experimental.pallas import tpu as pltpu

F32 = jnp.float32
BF16 = jnp.bfloat16

NORM_EPS = 1e-6
NEG_BIG = -1e30
V7X_VMEM_LIMIT_BYTES = 56 * 1024 * 1024

DA_HEADS, DA_DK, DA_DV = 8, 64, 128
DA_QW = DA_HEADS * 2 * DA_DK
DA_VW = DA_HEADS * DA_DV
DA_SCALE = DA_DK ** -0.5
N_BUCKETS, MAX_DISTANCE = 32, 128
SSM_HEADS, SSM_HEAD_DIM, SSM_GROUPS, SSM_STATE, SSM_CONV, SSM_CHUNK = 32, 64, 4, 128, 4, 128
MEM_HEADS, MEM_HEAD_DIM = 4, 256
MEM_W = MEM_HEADS * MEM_HEAD_DIM
MEM_SCALE = MEM_HEAD_DIM ** -0.5
PEER_HEADS, PEER_KEYS, PEER_TOPK, PEER_HALF = 8, 128, 16, 128


def _cparams(*sem):
    return pltpu.CompilerParams(dimension_semantics=sem, vmem_limit_bytes=V7X_VMEM_LIMIT_BYTES)


def _dot_nt(a, b):
    return lax.dot_general(a, b, (((1,), (1,)), ((), ())), preferred_element_type=F32)


def _pick_tile(n, candidates):
    for c in candidates:
        if n % c == 0:
            return c
    return n


def _rmsnorm_kernel(x_ref, g_ref, o_ref):
    x = x_ref[...]
    ms = jnp.mean(x * x, axis=-1, keepdims=True)
    o_ref[...] = (x * lax.rsqrt(ms + NORM_EPS) * g_ref[...]).astype(o_ref.dtype)


def rmsnorm_rows(x, g, out_dtype=BF16):
    t, d = x.shape
    tm = _pick_tile(t, (1056, 1024, 512, 256, 128))
    return pl.pallas_call(
        _rmsnorm_kernel,
        grid=(t // tm,),
        in_specs=[pl.BlockSpec((tm, d), lambda i: (i, 0)), pl.BlockSpec((1, d), lambda i: (0, 0))],
        out_specs=pl.BlockSpec((tm, d), lambda i: (i, 0)),
        out_shape=jax.ShapeDtypeStruct((t, d), out_dtype),
        compiler_params=_cparams("parallel"),
        name="rmsnorm_rows",
    )(x, g.reshape(1, d).astype(F32))


def _mm_kernel(a_ref, w_ref, o_ref):
    o_ref[...] = jnp.dot(a_ref[...], w_ref[...], preferred_element_type=F32).astype(o_ref.dtype)


def _mm_res_kernel(a_ref, w_ref, r_ref, o_ref):
    o_ref[...] = r_ref[...] + jnp.dot(a_ref[...], w_ref[...], preferred_element_type=F32)


def matmul(a, w, residual=None, out_dtype=F32, name="matmul"):
    t, k = a.shape
    n = w.shape[1]
    tm = _pick_tile(t, (528, 512, 256, 128))
    tn = _pick_tile(n, (1536, 1024, 512, 256, 128))
    in_specs = [pl.BlockSpec((tm, k), lambda i, j: (i, 0)), pl.BlockSpec((k, tn), lambda i, j: (0, j))]
    args = [a, w]
    body = _mm_kernel
    if residual is not None:
        in_specs.append(pl.BlockSpec((tm, tn), lambda i, j: (i, j)))
        args.append(residual)
        body = _mm_res_kernel
    return pl.pallas_call(
        body,
        grid=(t // tm, n // tn),
        in_specs=in_specs,
        out_specs=pl.BlockSpec((tm, tn), lambda i, j: (i, j)),
        out_shape=jax.ShapeDtypeStruct((t, n), out_dtype),
        compiler_params=_cparams("parallel", "parallel"),
        name=name,
    )(*args)


def _merge_kernel(a_ref, s_ref, m_ref, ga_ref, gs_ref, gm_ref, wa_ref, ws_ref, wm_ref, o_ref):
    acc = jax.nn.sigmoid(ga_ref[...]) * jnp.dot(a_ref[...], wa_ref[...], preferred_element_type=F32)
    acc += jax.nn.sigmoid(gs_ref[...]) * jnp.dot(s_ref[...], ws_ref[...], preferred_element_type=F32)
    acc += jax.nn.sigmoid(gm_ref[...]) * jnp.dot(m_ref[...], wm_ref[...], preferred_element_type=F32)
    o_ref[...] = acc.astype(o_ref.dtype)


def merge_branches(a, s, m, u, gate_col0, wa, ws, wm):
    t = a.shape[0]
    d = wa.shape[1]
    tm = _pick_tile(t, (528, 512, 256, 128))
    tn = 512
    gb = gate_col0 // tn
    nd = d // tn
    row = lambda width: pl.BlockSpec((tm, width), lambda i, j: (i, 0))
    gate = lambda which: pl.BlockSpec((tm, tn), lambda i, j: (i, gb + which * nd + j))
    wcol = lambda kdim: pl.BlockSpec((kdim, tn), lambda i, j: (0, j))
    return pl.pallas_call(
        _merge_kernel,
        grid=(t // tm, nd),
        in_specs=[row(a.shape[1]), row(s.shape[1]), row(m.shape[1]), gate(0), gate(1), gate(2),
                  wcol(wa.shape[0]), wcol(ws.shape[0]), wcol(wm.shape[0])],
        out_specs=pl.BlockSpec((tm, tn), lambda i, j: (i, j)),
        out_shape=jax.ShapeDtypeStruct((t, d), BF16),
        compiler_params=_cparams("parallel", "parallel"),
        name="merge_branches",
    )(a, s, m, u, u, u, wa, ws, wm)


def _t5_bucket(dist):
    n = jnp.maximum(dist, 0)
    exact = N_BUCKETS // 2
    nf = jnp.maximum(n, exact).astype(F32)
    large = exact + (jnp.log(nf / exact) / math.log(MAX_DISTANCE / exact) * (N_BUCKETS - exact)).astype(jnp.int32)
    return jnp.where(n < exact, n, jnp.minimum(large, N_BUCKETS - 1))


def _far_bias(table):
    return table.astype(F32)[N_BUCKETS - 1]


def prompt_bias_tiles(table, tile):
    assert tile >= MAX_DISTANCE
    i = jnp.arange(tile, dtype=jnp.int32)
    far = _far_bias(table)
    out = []
    for off in (0, tile):
        dist = off + i[:, None] - i[None, :]
        b = jnp.moveaxis(table.astype(F32)[_t5_bucket(dist)], -1, 0) - far[:, None, None]
        out.append(jnp.where(dist >= 0, b, NEG_BIG))
    return jnp.stack(out, axis=1)


def _fa_kernel(qt_ref, kt_ref, lam_ref, q_ref, k_ref, v_ref, b_ref, g_ref, o_ref, m_sc, l_sc, acc_sc, *,
               post_scale):
    s_id = pl.program_id(1)
    qi = qt_ref[s_id]
    ki = kt_ref[s_id]

    @pl.when(ki == 0)
    def _():
        m_sc[...] = jnp.full(m_sc.shape, NEG_BIG, F32)
        l_sc[...] = jnp.zeros(l_sc.shape, F32)
        acc_sc[...] = jnp.zeros(acc_sc.shape, F32)

    def update(bias_idx):
        q = q_ref[...]
        k = k_ref[...]
        v = v_ref[...]
        lane = lax.broadcasted_iota(jnp.int32, q.shape, 1)
        for m in range(2):
            in_map = (lane >= m * DA_DK) & (lane < (m + 1) * DA_DK)
            s = _dot_nt(jnp.where(in_map, q, jnp.zeros_like(q)), k)
            if bias_idx is not None:
                s = s + b_ref[bias_idx]
            m_prev = m_sc[m]
            m_new = jnp.maximum(m_prev, jnp.max(s, axis=-1, keepdims=True))
            alpha = jnp.exp(m_prev - m_new)
            p = jnp.exp(s - m_new)
            l_sc[m] = alpha * l_sc[m] + jnp.sum(p, axis=-1, keepdims=True)
            acc_sc[m] = alpha * acc_sc[m] + jnp.dot(p.astype(BF16), v, preferred_element_type=F32)
            m_sc[m] = m_new

    @pl.when(ki < qi - 1)
    def _():
        update(None)

    @pl.when(ki == qi - 1)
    def _():
        update(1)

    @pl.when(ki == qi)
    def _():
        update(0)
        o = acc_sc[0] / l_sc[0] - lam_ref[0] * (acc_sc[1] / l_sc[1])
        y = o * lax.rsqrt(jnp.mean(o * o, axis=-1, keepdims=True) + NORM_EPS)
        o_ref[...] = ((y * g_ref[...]) * post_scale).astype(o_ref.dtype)


def prompt_diff_attention(qn, kn, v, lam, bias_tiles, g_sub, post_scale, tile):
    s_len = qn.shape[0]
    nq = s_len // tile
    pairs = [(qi, ki) for qi in range(nq) for ki in range(qi + 1)]
    qt = jnp.asarray([p[0] for p in pairs], jnp.int32)
    kt = jnp.asarray([p[1] for p in pairs], jnp.int32)
    grid_spec = pltpu.PrefetchScalarGridSpec(
        num_scalar_prefetch=2,
        grid=(DA_HEADS, len(pairs)),
        in_specs=[
            pl.BlockSpec(memory_space=pltpu.SMEM),
            pl.BlockSpec((tile, 128), lambda h, s, qt, kt: (qt[s], h)),
            pl.BlockSpec((tile, 128), lambda h, s, qt, kt: (kt[s], h)),
            pl.BlockSpec((tile, 128), lambda h, s, qt, kt: (kt[s], h)),
            pl.BlockSpec((None, 2, tile, tile), lambda h, s, qt, kt: (h, 0, 0, 0)),
            pl.BlockSpec((1, DA_DV), lambda h, s, qt, kt: (0, 0)),
        ],
        out_specs=pl.BlockSpec((tile, 128), lambda h, s, qt, kt: (qt[s], h)),
        scratch_shapes=[pltpu.VMEM((2, tile, 1), F32), pltpu.VMEM((2, tile, 1), F32),
                        pltpu.VMEM((2, tile, DA_DV), F32)],
    )
    return pl.pallas_call(
        functools.partial(_fa_kernel, post_scale=post_scale),
        grid_spec=grid_spec,
        out_shape=jax.ShapeDtypeStruct((s_len, DA_VW), BF16),
        compiler_params=_cparams("parallel", "arbitrary"),
        name="prompt_diff_attention",
    )(qt, kt, lam.reshape(1).astype(F32), qn, kn, v, bias_tiles, g_sub.reshape(1, DA_DV).astype(F32))


def _topk_rows(s, k):
    r = s.shape[0]
    riota = lax.broadcasted_iota(jnp.int32, s.shape, 0)
    vals, idxs = [], []
    for _ in range(k):
        m = jnp.max(s, axis=0, keepdims=True)
        idx = jnp.min(jnp.where(s == m, riota, r), axis=0, keepdims=True)
        vals.append(m)
        idxs.append(idx)
        s = jnp.where(riota == idx, -jnp.inf, s)
    return vals, idxs


def _peer_route_kernel(q_ref, k1_ref, k2_ref, gate_ref, e1_ref, e2_ref):
    tt = q_ref.shape[0]
    k1 = k1_ref[...]
    k2 = k2_ref[...]
    gates, e1s, e2s = [], [], []
    for h in range(PEER_HEADS):
        qa = q_ref[:, (2 * h) * PEER_HALF:(2 * h + 1) * PEER_HALF].astype(BF16)
        qb = q_ref[:, (2 * h + 1) * PEER_HALF:(2 * h + 2) * PEER_HALF].astype(BF16)
        v1, i1 = _topk_rows(_dot_nt(k1, qa), PEER_TOPK)
        v2, i2 = _topk_rows(_dot_nt(k2, qb), PEER_TOPK)
        v2m = jnp.concatenate(v2, axis=0)
        i2m = jnp.concatenate(i2, axis=0)
        cand = jnp.concatenate([v1[a] + v2m for a in range(PEER_TOPK)], axis=0)
        c1 = jnp.concatenate([jnp.broadcast_to(i1[a], (PEER_TOPK, tt)) for a in range(PEER_TOPK)], axis=0)
        c2 = jnp.concatenate([i2m] * PEER_TOPK, axis=0)
        riota = lax.broadcasted_iota(jnp.int32, cand.shape, 0)
        sc, ea, eb = [], [], []
        for _ in range(PEER_TOPK):
            m = jnp.max(cand, axis=0, keepdims=True)
            idx = jnp.min(jnp.where(cand == m, riota, cand.shape[0]), axis=0, keepdims=True)
            sel = riota == idx
            sc.append(m)
            ea.append(jnp.sum(jnp.where(sel, c1, 0), axis=0, keepdims=True))
            eb.append(jnp.sum(jnp.where(sel, c2, 0), axis=0, keepdims=True))
            cand = jnp.where(sel, -jnp.inf, cand)
        scm = jnp.concatenate(sc, axis=0)
        ex = jnp.exp(scm - sc[0])
        gates.append(ex / jnp.sum(ex, axis=0, keepdims=True))
        e1s.append(jnp.concatenate(ea, axis=0).astype(F32))
        e2s.append(jnp.concatenate(eb, axis=0).astype(F32))
    gate_ref[...] = jnp.concatenate(gates, axis=0).T
    e1_ref[...] = jnp.concatenate(e1s, axis=0).T
    e2_ref[...] = jnp.concatenate(e2s, axis=0).T


def peer_route(q, k1, k2):
    t = q.shape[0]
    tt = 128
    hk = PEER_HEADS * PEER_TOPK
    out = jax.ShapeDtypeStruct((t, hk), F32)
    ospec = pl.BlockSpec((tt, hk), lambda i: (i, 0))
    kspec = pl.BlockSpec((PEER_KEYS, PEER_HALF), lambda i: (0, 0))
    return pl.pallas_call(
        _peer_route_kernel,
        grid=(t // tt,),
        in_specs=[pl.BlockSpec((tt, q.shape[1]), lambda i: (i, 0)), kspec, kspec],
        out_specs=[ospec, ospec, ospec],
        out_shape=[out, out, out],
        compiler_params=_cparams("parallel"),
        name="peer_route",
    )(q, k1, k2)


def _peer_w_kernel(gate_ref, e1_ref, e2_ref, w_ref):
    tb = gate_ref.shape[0]
    sub = lax.broadcasted_iota(jnp.int32, (PEER_KEYS, gate_ref.shape[1]), 0).astype(F32)

    def body(t, carry):
        g = gate_ref[pl.ds(t, 1), :]
        a = e1_ref[pl.ds(t, 1), :]
        b = e2_ref[pl.ds(t, 1), :]
        lhs = jnp.where(a == sub, g, 0.0).astype(BF16)
        rhs = jnp.where(b == sub, 1.0, 0.0).astype(BF16)
        w_ref[t] = _dot_nt(lhs, rhs).astype(w_ref.dtype)
        return carry

    lax.fori_loop(0, tb, body, 0)


def peer_dense_weights(gate, e1, e2):
    t, hk = gate.shape
    tb = 128
    spec = pl.BlockSpec((tb, hk), lambda i: (i, 0))
    return pl.pallas_call(
        _peer_w_kernel,
        grid=(t // tb,),
        in_specs=[spec, spec, spec],
        out_specs=pl.BlockSpec((tb, PEER_KEYS, PEER_KEYS), lambda i: (i, 0, 0)),
        out_shape=jax.ShapeDtypeStruct((t, PEER_KEYS, PEER_KEYS), BF16),
        compiler_params=_cparams("parallel"),
        name="peer_dense_weights",
    )(gate, e1, e2)


def _peer_ffn_kernel(x_ref, w_ref, u_ref, v_ref, r_ref, o_ref):
    e = pl.program_id(1)

    @pl.when(e == 0)
    def _():
        o_ref[...] = r_ref[...]

    a = _dot_nt(x_ref[...], u_ref[...])
    act = 0.5 * a * (1.0 + lax.erf(a * (2.0 ** -0.5)))
    hmat = (act * w_ref[...].astype(F32)).astype(BF16)
    o_ref[...] += jnp.dot(hmat, v_ref[...], preferred_element_type=F32)


def peer_ffn_dense(xn, w, u_tab, v_tab, resid):
    t, d = xn.shape
    n_exp = u_tab.shape[0]
    tb = _pick_tile(t, (528, 512, 256, 128))
    eb = 512
    return pl.pallas_call(
        _peer_ffn_kernel,
        grid=(t // tb, n_exp // eb),
        in_specs=[pl.BlockSpec((tb, d), lambda i, e: (i, 0)),
                  pl.BlockSpec((tb, eb), lambda i, e: (i, e)),
                  pl.BlockSpec((eb, d), lambda i, e: (e, 0)),
                  pl.BlockSpec((eb, d), lambda i, e: (e, 0)),
                  pl.BlockSpec((tb, d), lambda i, e: (i, 0))],
        out_specs=pl.BlockSpec((tb, d), lambda i, e: (i, 0)),
        out_shape=jax.ShapeDtypeStruct((t, d), F32),
        compiler_params=_cparams("parallel", "arbitrary"),
        name="peer_ffn_dense",
    )(xn, w, u_tab, v_tab, resid)


def peer_layer(x, g_ffn, wq, k1, k2, u_tab, v_tab):
    xn = rmsnorm_rows(x, g_ffn)
    q = matmul(xn, wq, name="peer_query")
    gate, e1, e2 = peer_route(q, k1, k2)
    w = peer_dense_weights(gate, e1, e2).reshape(x.shape[0], PEER_KEYS * PEER_KEYS)
    return peer_ffn_dense(xn, w, u_tab, v_tab, x)


def _rmsnorm(x, g):
    xf = x.astype(F32)
    y = xf * lax.rsqrt(jnp.mean(xf * xf, axis=-1, keepdims=True) + NORM_EPS)
    return (y * g.astype(F32)).astype(x.dtype)


def _t5_bias(dist, table):
    return jnp.moveaxis(table.astype(F32)[_t5_bucket(dist)], -1, 0)


def _diff_attn_sample(q, k, v, lam, k_pool, v_pool, layer, page_table, bias_table):
    b, t = q.shape[:2]
    past = page_table.shape[1] * k_pool.shape[2]
    k_past = k_pool[layer, page_table].reshape(b, past, DA_HEADS, 2, DA_DK)
    v_past = v_pool[layer, page_table].reshape(b, past, DA_HEADS, DA_DV)
    tpos = jnp.arange(t, dtype=jnp.int32)
    dist_past = (past + tpos)[:, None] - jnp.arange(past, dtype=jnp.int32)[None, :]
    dist_new = tpos[:, None] - tpos[None, :]
    lp = jnp.einsum('bqhmd,bkhmd->bhmqk', q, k_past, preferred_element_type=F32) * DA_SCALE
    lp = lp + _t5_bias(dist_past, bias_table)[None, :, None]
    ln = jnp.einsum('bqhmd,bkhmd->bhmqk', q, k, preferred_element_type=F32) * DA_SCALE
    ln = jnp.where(dist_new >= 0, ln + _t5_bias(dist_new, bias_table)[None, :, None], -jnp.inf)
    prob = jax.nn.softmax(jnp.concatenate([lp, ln], axis=-1), axis=-1)
    attn = (prob[:, :, 0] - lam * prob[:, :, 1]).astype(v.dtype)
    return (jnp.einsum('bhqk,bkhd->bqhd', attn[..., :past], v_past)
            + jnp.einsum('bhqk,bkhd->bqhd', attn[..., past:], v))


def _memory_attend(q, mk, mv):
    logits = jnp.einsum('bqhd,bmhd->bhqm', q, mk, preferred_element_type=F32) * MEM_SCALE
    prob = jax.nn.softmax(logits, axis=-1)
    return jnp.einsum('bhqm,bmhd->bqhd', prob.astype(mv.dtype), mv)


def _ssd_scan(xh, dt, a, bm, cm, h0):
    b, L = xh.shape[:2]
    q = min(SSM_CHUNK, L)
    assert L % q == 0
    c = L // q
    hpg = SSM_HEADS // SSM_GROUPS
    xdt = (xh * dt[..., None]).reshape(b, c, q, SSM_GROUPS, hpg, SSM_HEAD_DIM)
    acs = jnp.cumsum((dt * a).reshape(b, c, q, SSM_GROUPS, hpg), axis=2)
    bc = bm.reshape(b, c, q, SSM_GROUPS, SSM_STATE)
    cc = cm.reshape(b, c, q, SSM_GROUPS, SSM_STATE)
    causal = jnp.tril(jnp.ones((q, q), dtype=bool))[:, :, None, None]
    seg = acs[:, :, :, None] - acs[:, :, None, :]
    lmat = jnp.exp(jnp.where(causal, seg, -jnp.inf))
    cb = jnp.einsum('bclgn,bcsgn->bclsg', cc, bc)
    y_diag = jnp.einsum('bclsgh,bcsghp->bclghp', cb[..., None] * lmat, xdt)
    decay_s = jnp.exp(acs[:, :, -1:] - acs)
    states = jnp.einsum('bcsgn,bcsghp->bcghpn', bc, decay_s[..., None] * xdt)
    chunk_decay = jnp.exp(acs[:, :, -1])

    def step(h, inp):
        dec, st = inp
        return h * dec[..., None, None] + st, h

    h_last, h_in = lax.scan(step, h0.reshape(b, SSM_GROUPS, hpg, SSM_HEAD_DIM, SSM_STATE),
                            (jnp.moveaxis(chunk_decay, 1, 0), jnp.moveaxis(states, 1, 0)))
    h_in = jnp.moveaxis(h_in, 0, 1)
    y_off = jnp.einsum('bclgn,bcghpn->bclghp', cc, h_in) * jnp.exp(acs)[..., None]
    y = (y_diag + y_off).reshape(b, c * q, SSM_HEADS, SSM_HEAD_DIM)
    return y, h_last.reshape(b, SSM_HEADS, SSM_HEAD_DIM, SSM_STATE)


def _mamba_branch(z, xbc, dt_raw, conv_prev, h0, conv_w, conv_b, dt_bias, a_log, d_skip, g_ssm):
    b, L = xbc.shape[:2]
    inner = SSM_HEADS * SSM_HEAD_DIM
    gn = SSM_GROUPS * SSM_STATE
    xpad = jnp.concatenate([conv_prev.astype(xbc.dtype), xbc], axis=1)
    conv = conv_b + conv_w[0] * xpad[:, 0:L]
    for j in range(1, SSM_CONV):
        conv = conv + conv_w[j] * xpad[:, j:j + L]
    conv_new = xpad[:, L:]
    xc = jax.nn.silu(conv)
    xs, bm, cm = xc[..., :inner], xc[..., inner:inner + gn], xc[..., inner + gn:]
    xh = xs.reshape(b, L, SSM_HEADS, SSM_HEAD_DIM).astype(F32)
    dt = jax.nn.softplus(dt_raw.astype(F32) + dt_bias.astype(F32))
    a = -jnp.exp(a_log.astype(F32))
    y, h_fin = _ssd_scan(xh, dt, a, bm.reshape(b, L, SSM_GROUPS, SSM_STATE).astype(F32),
                         cm.reshape(b, L, SSM_GROUPS, SSM_STATE).astype(F32), h0.astype(F32))
    y = y + d_skip.astype(F32)[:, None] * xh
    y = y.reshape(b, L, inner).astype(z.dtype) * jax.nn.silu(z)
    y = _rmsnorm(y.reshape(b, L, SSM_GROUPS, inner // SSM_GROUPS), g_ssm.reshape(SSM_GROUPS, inner // SSM_GROUPS))
    return y.reshape(b, L, inner), h_fin, conv_new


FA_TILE = 512


def kernel(x_prompt, x_sample, cache_attn_k, cache_attn_v, cache_mem_k, cache_mem_v, state_ssm, state_conv, page_table, mem_prompt, rel_bias, g_mix, w_in, g_q, g_k, lam_q1, lam_k1, lam_q2, lam_k2, g_sub, conv_w, conv_b, dt_bias, a_log, d_skip, g_ssm, g_mem, w_mem_kv, g_mq, g_mk, w_br_attn, w_br_ssm, w_br_mem, w_out, g_ffn, peer_wq, peer_k1, peer_k2, peer_u, peer_v):
    depth = w_in.shape[0]
    bp, sp, d = x_prompt.shape
    bs, ts, _ = x_sample.shape
    assert bp == 1
    n_p = bp * sp
    n_s = bs * ts
    inner = SSM_HEADS * SSM_HEAD_DIM
    conv_dim = inner + 2 * SSM_GROUPS * SSM_STATE
    c_q, c_k, c_v, c_z = 0, DA_QW, 2 * DA_QW, 2 * DA_QW + DA_VW
    c_xbc = c_z + inner
    c_dt = c_xbc + conv_dim
    c_mq_src = c_dt + SSM_HEADS
    c_mq = c_dt
    c_gate = c_mq + MEM_W

    x = jnp.concatenate([x_prompt.reshape(n_p, d), x_sample.reshape(n_s, d)], axis=0)
    bias_tiles = prompt_bias_tiles(rel_bias, FA_TILE)

    outs = {k: [] for k in ("pk", "pv", "pmk", "pmv", "ph", "pc", "sk", "sv", "sh", "sc")}
    for l in range(depth):
        lam_init = 0.8 - 0.6 * math.exp(-0.3 * l)
        lam = (jnp.exp(jnp.sum(lam_q1[l].astype(F32) * lam_k1[l].astype(F32)))
               - jnp.exp(jnp.sum(lam_q2[l].astype(F32) * lam_k2[l].astype(F32))) + lam_init)
        w_main = jnp.concatenate([w_in[l][:, :c_dt], w_in[l][:, c_mq_src:]], axis=1).astype(BF16)
        w_dt = jnp.pad(w_in[l][:, c_dt:c_mq_src], ((0, 0), (0, 128 - SSM_HEADS))).astype(BF16)

        h = rmsnorm_rows(x, g_mix[l])
        u = matmul(h, w_main, name="in_proj")
        dt_raw = matmul(h, w_dt, name="dt_proj")[:, :SSM_HEADS]

        q = _rmsnorm(u[:, c_q:c_q + DA_QW].reshape(-1, DA_HEADS, 2, DA_DK), g_q[l])
        k = _rmsnorm(u[:, c_k:c_k + DA_QW].reshape(-1, DA_HEADS, 2, DA_DK), g_k[l])
        v = u[:, c_v:c_v + DA_VW]
        z = u[:, c_z:c_z + inner]
        xbc = u[:, c_xbc:c_xbc + conv_dim]
        mq = _rmsnorm(u[:, c_mq:c_mq + MEM_W].reshape(-1, MEM_HEADS, MEM_HEAD_DIM), g_mq[l])

        qn_p = (q[:n_p] * DA_SCALE).reshape(n_p, DA_QW).astype(BF16)
        a_p = prompt_diff_attention(qn_p, k[:n_p].reshape(n_p, DA_QW).astype(BF16), v[:n_p].astype(BF16), lam,
                                    bias_tiles, g_sub[l], 1.0 - lam_init, FA_TILE)
        hm = rmsnorm_rows(mem_prompt.reshape(-1, d), g_mem[l])
        mkv = matmul(hm, w_mem_kv[l].astype(BF16), name="mem_kv")
        mk_p = _rmsnorm(mkv[:, :MEM_W].reshape(bp, -1, MEM_HEADS, MEM_HEAD_DIM), g_mk[l])
        mv_p = mkv[:, MEM_W:].reshape(bp, -1, MEM_HEADS, MEM_HEAD_DIM)
        m_p = _memory_attend(mq[:n_p].reshape(bp, sp, MEM_HEADS, MEM_HEAD_DIM), mk_p, mv_p).reshape(n_p, MEM_W)
        zero_conv = jnp.zeros((bp, SSM_CONV - 1, conv_dim), F32)
        zero_h = jnp.zeros((bp, SSM_HEADS, SSM_HEAD_DIM, SSM_STATE), F32)
        s_p, h_p, c_p = _mamba_branch(z[:n_p].reshape(bp, sp, inner), xbc[:n_p].reshape(bp, sp, conv_dim),
                                      dt_raw[:n_p].reshape(bp, sp, SSM_HEADS), zero_conv, zero_h,
                                      conv_w[l], conv_b[l], dt_bias[l], a_log[l], d_skip[l], g_ssm[l])

        q_s = q[n_p:].reshape(bs, ts, DA_HEADS, 2, DA_DK)
        k_s = k[n_p:].reshape(bs, ts, DA_HEADS, 2, DA_DK)
        v_s = v[n_p:].reshape(bs, ts, DA_HEADS, DA_DV)
        o_s = _diff_attn_sample(q_s, k_s, v_s, lam, cache_attn_k, cache_attn_v, l, page_table, rel_bias)
        a_s = (_rmsnorm(o_s, g_sub[l]) * (1.0 - lam_init)).reshape(n_s, DA_VW)
        m_s = _memory_attend(mq[n_p:].reshape(bs, ts, MEM_HEADS, MEM_HEAD_DIM), cache_mem_k[l],
                             cache_mem_v[l]).reshape(n_s, MEM_W)
        s_s, h_s, c_s = _mamba_branch(z[n_p:].reshape(bs, ts, inner), xbc[n_p:].reshape(bs, ts, conv_dim),
                                      dt_raw[n_p:].reshape(bs, ts, SSM_HEADS), state_conv[l], state_ssm[l],
                                      conv_w[l], conv_b[l], dt_bias[l], a_log[l], d_skip[l], g_ssm[l])

        a_out = jnp.concatenate([a_p, a_s.astype(BF16)], axis=0)
        s_out = jnp.concatenate([s_p.reshape(n_p, inner), s_s.reshape(n_s, inner)], axis=0).astype(BF16)
        m_out = jnp.concatenate([m_p, m_s], axis=0).astype(BF16)
        merged = merge_branches(a_out, s_out, m_out, u, c_gate, w_br_attn[l].astype(BF16),
                                w_br_ssm[l].astype(BF16), w_br_mem[l].astype(BF16))
        x = matmul(merged, w_out[l].astype(BF16), residual=x, name="out_proj")
        x = peer_layer(x, g_ffn[l], peer_wq[l].astype(BF16), peer_k1[l].astype(BF16), peer_k2[l].astype(BF16),
                       peer_u[l].astype(BF16), peer_v[l].astype(BF16))

        outs["pk"].append(k[:n_p].reshape(bp, sp, DA_HEADS, 2, DA_DK))
        outs["pv"].append(v[:n_p].reshape(bp, sp, DA_HEADS, DA_DV))
        outs["pmk"].append(mk_p)
        outs["pmv"].append(mv_p)
        outs["ph"].append(h_p)
        outs["pc"].append(c_p)
        outs["sk"].append(k_s)
        outs["sv"].append(v_s)
        outs["sh"].append(h_s)
        outs["sc"].append(c_s)

    y_prompt = x[:n_p].reshape(bp, sp, d)
    y_sample = x[n_p:].reshape(bs, ts, d)
    st = lambda name: jnp.stack(outs[name])
    return (y_prompt, y_sample, st("pk"), st("pv"), st("pmk"), st("pmv"), st("ph"), st("pc"),
            st("sk"), st("sv"), st("sh"), st("sc"))
```

```python
import functools
import math

import jax
import jax.numpy as jnp
from jax import lax
from jax.experimental import pallas as pl
from jax.experimental.pallas import tpu as pltpu

F32 = jnp.float32
BF16 = jnp.bfloat16

NORM_EPS = 1e-6
NEG_BIG = -1e30
V7X_VMEM_LIMIT_BYTES = 56 * 1024 * 1024

DA_HEADS, DA_DK, DA_DV = 8, 64, 128
DA_QW = DA_HEADS * 2 * DA_DK
DA_VW = DA_HEADS * DA_DV
DA_SCALE = DA_DK ** -0.5
N_BUCKETS, MAX_DISTANCE = 32, 128
SSM_HEADS, SSM_HEAD_DIM, SSM_GROUPS, SSM_STATE, SSM_CONV, SSM_CHUNK = 32, 64, 4, 128, 4, 128
MEM_HEADS, MEM_HEAD_DIM = 4, 256
MEM_W = MEM_HEADS * MEM_HEAD_DIM
MEM_SCALE = MEM_HEAD_DIM ** -0.5
PEER_HEADS, PEER_KEYS, PEER_TOPK, PEER_HALF = 8, 128, 16, 128


def _cparams(*sem):
    return pltpu.CompilerParams(dimension_semantics=sem, vmem_limit_bytes=V7X_VMEM_LIMIT_BYTES)


def _dot_nt(a, b):
    return lax.dot_general(a, b, (((1,), (1,)), ((), ())), preferred_element_type=F32)


def _pick_tile(n, candidates):
    for c in candidates:
        if n % c == 0:
            return c
    return n


def _rmsnorm_kernel(x_ref, g_ref, o_ref):
    x = x_ref[...]
    ms = jnp.mean(x * x, axis=-1, keepdims=True)
    o_ref[...] = (x * lax.rsqrt(ms + NORM_EPS) * g_ref[...]).astype(o_ref.dtype)


def rmsnorm_rows(x, g, out_dtype=BF16):
    t, d = x.shape
    tm = _pick_tile(t, (1056, 1024, 512, 256, 128))
    return pl.pallas_call(
        _rmsnorm_kernel,
        grid=(t // tm,),
        in_specs=[pl.BlockSpec((tm, d), lambda i: (i, 0)), pl.BlockSpec((1, d), lambda i: (0, 0))],
        out_specs=pl.BlockSpec((tm, d), lambda i: (i, 0)),
        out_shape=jax.ShapeDtypeStruct((t, d), out_dtype),
        compiler_params=_cparams("parallel"),
        name="rmsnorm_rows",
    )(x, g.reshape(1, d).astype(F32))


def _mm_kernel(a_ref, w_ref, o_ref):
    o_ref[...] = jnp.dot(a_ref[...], w_ref[...], preferred_element_type=F32).astype(o_ref.dtype)


def _mm_res_kernel(a_ref, w_ref, r_ref, o_ref):
    o_ref[...] = r_ref[...] + jnp.dot(a_ref[...], w_ref[...], preferred_element_type=F32)


def matmul(a, w, residual=None, out_dtype=F32, name="matmul"):
    t, k = a.shape
    n = w.shape[1]
    tm = _pick_tile(t, (528, 512, 256, 128))
    tn = _pick_tile(n, (1536, 1024, 512, 256, 128))
    in_specs = [pl.BlockSpec((tm, k), lambda i, j: (i, 0)), pl.BlockSpec((k, tn), lambda i, j: (0, j))]
    args = [a, w]
    body = _mm_kernel
    if residual is not None:
        in_specs.append(pl.BlockSpec((tm, tn), lambda i, j: (i, j)))
        args.append(residual)
        body = _mm_res_kernel
    return pl.pallas_call(
        body,
        grid=(t // tm, n // tn),
        in_specs=in_specs,
        out_specs=pl.BlockSpec((tm, tn), lambda i, j: (i, j)),
        out_shape=jax.ShapeDtypeStruct((t, n), out_dtype),
        compiler_params=_cparams("parallel", "parallel"),
        name=name,
    )(*args)


def _merge_kernel(a_ref, s_ref, m_ref, ga_ref, gs_ref, gm_ref, wa_ref, ws_ref, wm_ref, o_ref):
    acc = jax.nn.sigmoid(ga_ref[...]) * jnp.dot(a_ref[...], wa_ref[...], preferred_element_type=F32)
    acc += jax.nn.sigmoid(gs_ref[...]) * jnp.dot(s_ref[...], ws_ref[...], preferred_element_type=F32)
    acc += jax.nn.sigmoid(gm_ref[...]) * jnp.dot(m_ref[...], wm_ref[...], preferred_element_type=F32)
    o_ref[...] = acc.astype(o_ref.dtype)


def merge_branches(a, s, m, u, gate_col0, wa, ws, wm):
    t = a.shape[0]
    d = wa.shape[1]
    tm = _pick_tile(t, (528, 512, 256, 128))
    tn = 512
    gb = gate_col0 // tn
    nd = d // tn
    row = lambda width: pl.BlockSpec((tm, width), lambda i, j: (i, 0))
    gate = lambda which: pl.BlockSpec((tm, tn), lambda i, j: (i, gb + which * nd + j))
    wcol = lambda kdim: pl.BlockSpec((kdim, tn), lambda i, j: (0, j))
    return pl.pallas_call(
        _merge_kernel,
        grid=(t // tm, nd),
        in_specs=[row(a.shape[1]), row(s.shape[1]), row(m.shape[1]), gate(0), gate(1), gate(2),
                  wcol(wa.shape[0]), wcol(ws.shape[0]), wcol(wm.shape[0])],
        out_specs=pl.BlockSpec((tm, tn), lambda i, j: (i, j)),
        out_shape=jax.ShapeDtypeStruct((t, d), BF16),
        compiler_params=_cparams("parallel", "parallel"),
        name="merge_branches",
    )(a, s, m, u, u, u, wa, ws, wm)


def _t5_bucket(dist):
    n = jnp.maximum(dist, 0)
    exact = N_BUCKETS // 2
    nf = jnp.maximum(n, exact).astype(F32)
    large = exact + (jnp.log(nf / exact) / math.log(MAX_DISTANCE / exact) * (N_BUCKETS - exact)).astype(jnp.int32)
    return jnp.where(n < exact, n, jnp.minimum(large, N_BUCKETS - 1))


def _far_bias(table):
    return table.astype(F32)[N_BUCKETS - 1]


def prompt_bias_tiles(table, tile):
    assert tile >= MAX_DISTANCE
    i = jnp.arange(tile, dtype=jnp.int32)
    far = _far_bias(table)
    out = []
    for off in (0, tile):
        dist = off + i[:, None] - i[None, :]
        b = jnp.moveaxis(table.astype(F32)[_t5_bucket(dist)], -1, 0) - far[:, None, None]
        out.append(jnp.where(dist >= 0, b, NEG_BIG))
    return jnp.stack(out, axis=1)


def _fa_kernel(qt_ref, kt_ref, lam_ref, q_ref, k_ref, v_ref, b_ref, g_ref, o_ref, m_sc, l_sc, acc_sc, *,
               post_scale):
    s_id = pl.program_id(1)
    qi = qt_ref[s_id]
    ki = kt_ref[s_id]

    @pl.when(ki == 0)
    def _():
        m_sc[...] = jnp.full(m_sc.shape, NEG_BIG, F32)
        l_sc[...] = jnp.zeros(l_sc.shape, F32)
        acc_sc[...] = jnp.zeros(acc_sc.shape, F32)

    def update(bias_idx):
        q = q_ref[...]
        k = k_ref[...]
        v = v_ref[...].astype(BF16)
        lane = lax.broadcasted_iota(jnp.int32, q.shape, 1)
        for m in range(2):
            in_map = (lane >= m * DA_DK) & (lane < (m + 1) * DA_DK)
            s = _dot_nt(jnp.where(in_map, q, jnp.zeros_like(q)), k)
            if bias_idx is not None:
                s = s + b_ref[bias_idx]
            m_prev = m_sc[m]
            m_new = jnp.maximum(m_prev, jnp.max(s, axis=-1, keepdims=True))
            alpha = jnp.exp(m_prev - m_new)
            p = jnp.exp(s - m_new)
            l_sc[m] = alpha * l_sc[m] + jnp.sum(p, axis=-1, keepdims=True)
            acc_sc[m] = alpha * acc_sc[m] + jnp.dot(p.astype(BF16), v, preferred_element_type=F32)
            m_sc[m] = m_new

    @pl.when(ki < qi - 1)
    def _():
        update(None)

    @pl.when(ki == qi - 1)
    def _():
        update(1)

    @pl.when(ki == qi)
    def _():
        update(0)
        o = acc_sc[0] / l_sc[0] - lam_ref[0] * (acc_sc[1] / l_sc[1])
        y = o * lax.rsqrt(jnp.mean(o * o, axis=-1, keepdims=True) + NORM_EPS)
        o_ref[...] = ((y * g_ref[...]) * post_scale).astype(o_ref.dtype)


def prompt_diff_attention(qn, kn, v_src, v_col0, s_len, lam, bias_tiles, g_sub, post_scale, tile):
    assert v_col0 % DA_DV == 0
    vb = v_col0 // DA_DV
    nq = s_len // tile
    pairs = [(qi, ki) for qi in range(nq) for ki in range(qi + 1)]
    qt = jnp.asarray([p[0] for p in pairs], jnp.int32)
    kt = jnp.asarray([p[1] for p in pairs], jnp.int32)
    grid_spec = pltpu.PrefetchScalarGridSpec(
        num_scalar_prefetch=2,
        grid=(DA_HEADS, len(pairs)),
        in_specs=[
            pl.BlockSpec(memory_space=pltpu.SMEM),
            pl.BlockSpec((tile, 128), lambda h, s, qt, kt: (qt[s], h)),
            pl.BlockSpec((tile, 128), lambda h, s, qt, kt: (kt[s], h)),
            pl.BlockSpec((tile, DA_DV), lambda h, s, qt, kt: (kt[s], vb + h)),
            pl.BlockSpec((None, 2, tile, tile), lambda h, s, qt, kt: (h, 0, 0, 0)),
            pl.BlockSpec((1, DA_DV), lambda h, s, qt, kt: (0, 0)),
        ],
        out_specs=pl.BlockSpec((tile, 128), lambda h, s, qt, kt: (qt[s], h)),
        scratch_shapes=[pltpu.VMEM((2, tile, 1), F32), pltpu.VMEM((2, tile, 1), F32),
                        pltpu.VMEM((2, tile, DA_DV), F32)],
    )
    return pl.pallas_call(
        functools.partial(_fa_kernel, post_scale=post_scale),
        grid_spec=grid_spec,
        out_shape=jax.ShapeDtypeStruct((s_len, DA_VW), BF16),
        compiler_params=_cparams("parallel", "arbitrary"),
        name="prompt_diff_attention",
    )(qt, kt, lam.reshape(1).astype(F32), qn, kn, v_src, bias_tiles, g_sub.reshape(1, DA_DV).astype(F32))


def _topk_rows(s, k):
    r = s.shape[0]
    riota = lax.broadcasted_iota(jnp.int32, s.shape, 0)
    vals, idxs = [], []
    for _ in range(k):
        m = jnp.max(s, axis=0, keepdims=True)
        idx = jnp.min(jnp.where(s == m, riota, r), axis=0, keepdims=True)
        vals.append(m)
        idxs.append(idx)
        s = jnp.where(riota == idx, -jnp.inf, s)
    return vals, idxs


def _peer_route_kernel(q_ref, k1_ref, k2_ref, gate_ref, e1_ref, e2_ref):
    tt = q_ref.shape[0]
    k1 = k1_ref[...]
    k2 = k2_ref[...]
    gates, e1s, e2s = [], [], []
    for h in range(PEER_HEADS):
        qa = q_ref[:, (2 * h) * PEER_HALF:(2 * h + 1) * PEER_HALF].astype(BF16)
        qb = q_ref[:, (2 * h + 1) * PEER_HALF:(2 * h + 2) * PEER_HALF].astype(BF16)
        v1, i1 = _topk_rows(_dot_nt(k1, qa), PEER_TOPK)
        v2, i2 = _topk_rows(_dot_nt(k2, qb), PEER_TOPK)
        v2m = jnp.concatenate(v2, axis=0)
        i2m = jnp.concatenate(i2, axis=0)
        cand = jnp.concatenate([v1[a] + v2m for a in range(PEER_TOPK)], axis=0)
        c1 = jnp.concatenate([jnp.broadcast_to(i1[a], (PEER_TOPK, tt)) for a in range(PEER_TOPK)], axis=0)
        c2 = jnp.concatenate([i2m] * PEER_TOPK, axis=0)
        riota = lax.broadcasted_iota(jnp.int32, cand.shape, 0)
        sc, ea, eb = [], [], []
        for _ in range(PEER_TOPK):
            m = jnp.max(cand, axis=0, keepdims=True)
            idx = jnp.min(jnp.where(cand == m, riota, cand.shape[0]), axis=0, keepdims=True)
            sel = riota == idx
            sc.append(m)
            ea.append(jnp.sum(jnp.where(sel, c1, 0), axis=0, keepdims=True))
            eb.append(jnp.sum(jnp.where(sel, c2, 0), axis=0, keepdims=True))
            cand = jnp.where(sel, -jnp.inf, cand)
        scm = jnp.concatenate(sc, axis=0)
        ex = jnp.exp(scm - sc[0])
        gates.append(ex / jnp.sum(ex, axis=0, keepdims=True))
        e1s.append(jnp.concatenate(ea, axis=0).astype(F32))
        e2s.append(jnp.concatenate(eb, axis=0).astype(F32))
    gate_ref[...] = jnp.concatenate(gates, axis=0).T
    e1_ref[...] = jnp.concatenate(e1s, axis=0).T
    e2_ref[...] = jnp.concatenate(e2s, axis=0).T


def peer_route(q, k1, k2):
    t = q.shape[0]
    tt = 128
    hk = PEER_HEADS * PEER_TOPK
    out = jax.ShapeDtypeStruct((t, hk), F32)
    ospec = pl.BlockSpec((tt, hk), lambda i: (i, 0))
    kspec = pl.BlockSpec((PEER_KEYS, PEER_HALF), lambda i: (0, 0))
    return pl.pallas_call(
        _peer_route_kernel,
        grid=(t // tt,),
        in_specs=[pl.BlockSpec((tt, q.shape[1]), lambda i: (i, 0)), kspec, kspec],
        out_specs=[ospec, ospec, ospec],
        out_shape=[out, out, out],
        compiler_params=_cparams("parallel"),
        name="peer_route",
    )(q, k1, k2)


def _peer_w_kernel(gate_ref, e1_ref, e2_ref, w_ref):
    tb = gate_ref.shape[0]
    sub = lax.broadcasted_iota(jnp.int32, (PEER_KEYS, gate_ref.shape[1]), 0).astype(F32)

    def body(t, carry):
        g = gate_ref[pl.ds(t, 1), :]
        a = e1_ref[pl.ds(t, 1), :]
        b = e2_ref[pl.ds(t, 1), :]
        lhs = jnp.where(a == sub, g, 0.0).astype(BF16)
        rhs = jnp.where(b == sub, 1.0, 0.0).astype(BF16)
        w_ref[t] = _dot_nt(lhs, rhs)
        return carry

    lax.fori_loop(0, tb, body, 0, unroll=8)


def peer_dense_weights(gate, e1, e2):
    t, hk = gate.shape
    tb = 128
    spec = pl.BlockSpec((tb, hk), lambda i: (i, 0))
    return pl.pallas_call(
        _peer_w_kernel,
        grid=(t // tb,),
        in_specs=[spec, spec, spec],
        out_specs=pl.BlockSpec((tb, PEER_KEYS, PEER_KEYS), lambda i: (i, 0, 0)),
        out_shape=jax.ShapeDtypeStruct((t, PEER_KEYS, PEER_KEYS), F32),
        compiler_params=_cparams("parallel"),
        name="peer_dense_weights",
    )(gate, e1, e2)


PEER_KEY1_PER_STEP = 8


def _peer_ffn_kernel(x_ref, w_ref, u_ref, v_ref, r_ref, o_ref):
    e = pl.program_id(1)

    @pl.when(e == 0)
    def _():
        o_ref[...] = r_ref[...]

    a = _dot_nt(x_ref[...], u_ref[...])
    act = 0.5 * a * (1.0 + lax.erf(a * (2.0 ** -0.5)))
    hmat = jnp.concatenate(
        [(act[:, i * PEER_KEYS:(i + 1) * PEER_KEYS] * w_ref[:, i, :]).astype(BF16)
         for i in range(PEER_KEY1_PER_STEP)], axis=1)
    o_ref[...] += jnp.dot(hmat, v_ref[...], preferred_element_type=F32)


def peer_ffn_dense(xn, w, u_tab, v_tab, resid):
    t, d = xn.shape
    n_exp = u_tab.shape[0]
    tb = _pick_tile(t, (528, 512, 256, 128))
    eb = PEER_KEY1_PER_STEP * PEER_KEYS
    once = pl.Buffered(1)
    return pl.pallas_call(
        _peer_ffn_kernel,
        grid=(t // tb, n_exp // eb),
        in_specs=[pl.BlockSpec((tb, d), lambda i, e: (i, 0), pipeline_mode=once),
                  pl.BlockSpec((tb, PEER_KEY1_PER_STEP, PEER_KEYS), lambda i, e: (i, e, 0)),
                  pl.BlockSpec((eb, d), lambda i, e: (e, 0)),
                  pl.BlockSpec((eb, d), lambda i, e: (e, 0)),
                  pl.BlockSpec((tb, d), lambda i, e: (i, 0), pipeline_mode=once)],
        out_specs=pl.BlockSpec((tb, d), lambda i, e: (i, 0)),
        out_shape=jax.ShapeDtypeStruct((t, d), F32),
        compiler_params=_cparams("parallel", "arbitrary"),
        name="peer_ffn_dense",
    )(xn, w, u_tab, v_tab, resid)


def peer_layer(x, g_ffn, wq, k1, k2, u_tab, v_tab):
    xn = rmsnorm_rows(x, g_ffn)
    q = matmul(xn, wq, name="peer_query")
    gate, e1, e2 = peer_route(q, k1, k2)
    w = peer_dense_weights(gate, e1, e2)
    return peer_ffn_dense(xn, w, u_tab, v_tab, x)


def _group_norm_kernel(x_ref, bd_ref, g_ref, *out_refs, inv_group, scales):
    x = x_ref[...]
    sq = x * x
    hi = sq.astype(BF16)
    lo = (sq - hi.astype(F32)).astype(BF16)
    bd = bd_ref[...]
    ss = jnp.dot(hi, bd, preferred_element_type=F32) + jnp.dot(lo, bd, preferred_element_type=F32)
    y = x * lax.rsqrt(ss * inv_group + NORM_EPS) * g_ref[...]
    for o_ref, sc in zip(out_refs, scales):
        o_ref[...] = (y if sc == 1.0 else y * sc).astype(o_ref.dtype)


def group_norm(src, col0, width, group, gain, outs):
    t = src.shape[0]
    assert col0 % width == 0 and width % group == 0
    tm = _pick_tile(t, (528, 512, 256, 128))
    lane = jnp.arange(width, dtype=jnp.int32) // group
    bd = (lane[:, None] == lane[None, :]).astype(BF16)
    g = jnp.tile(gain.astype(F32), width // group).reshape(1, width)
    ospec = pl.BlockSpec((tm, width), lambda i: (i, 0))
    res = pl.pallas_call(
        functools.partial(_group_norm_kernel, inv_group=1.0 / group, scales=tuple(s for _, s in outs)),
        grid=(t // tm,),
        in_specs=[pl.BlockSpec((tm, width), lambda i: (i, col0 // width)),
                  pl.BlockSpec((width, width), lambda i: (0, 0)),
                  pl.BlockSpec((1, width), lambda i: (0, 0))],
        out_specs=[ospec] * len(outs),
        out_shape=[jax.ShapeDtypeStruct((t, width), dt) for dt, _ in outs],
        compiler_params=_cparams("parallel"),
        name="group_norm",
    )(src, bd, g)
    return res


CONV_HALO = 8


def _conv_kernel(x_ref, halo_ref, w_ref, b_ref, o_ref, sc, *, first_tile_has_no_history):
    tm = x_ref.shape[0]
    halo = halo_ref[...]
    if first_tile_has_no_history:
        halo = jnp.where(pl.program_id(0) == 0, 0.0, halo)
    sc[0:CONV_HALO, :] = halo
    sc[CONV_HALO:, :] = x_ref[...]
    acc = b_ref[...]
    for j in range(SSM_CONV):
        acc = acc + w_ref[j:j + 1, :] * sc[pl.ds(CONV_HALO - (SSM_CONV - 1 - j), tm), :]
    o_ref[...] = jax.nn.silu(acc)


def causal_conv_silu(src, col0, width, row0, rows, tm, halo_src, conv_w, conv_b):
    tc = 512
    assert col0 % tc == 0 and width % tc == 0 and row0 % tm == 0 and rows % tm == 0 and tm % CONV_HALO == 0
    cb, rb, hb = col0 // tc, row0 // tm, tm // CONV_HALO
    if halo_src is None:
        halo_arr = src
        halo_spec = pl.BlockSpec((CONV_HALO, tc), lambda i, j: (jnp.maximum((rb + i) * hb - 1, 0), cb + j))
    else:
        halo_arr = halo_src
        halo_spec = pl.BlockSpec((CONV_HALO, tc), lambda i, j: (i, j))
    return pl.pallas_call(
        functools.partial(_conv_kernel, first_tile_has_no_history=halo_src is None),
        grid=(rows // tm, width // tc),
        in_specs=[pl.BlockSpec((tm, tc), lambda i, j: (rb + i, cb + j)),
                  halo_spec,
                  pl.BlockSpec((SSM_CONV, tc), lambda i, j: (0, j)),
                  pl.BlockSpec((1, tc), lambda i, j: (0, j))],
        out_specs=pl.BlockSpec((tm, tc), lambda i, j: (i, j)),
        out_shape=jax.ShapeDtypeStruct((rows, width), F32),
        scratch_shapes=[pltpu.VMEM((tm + CONV_HALO, tc), F32)],
        compiler_params=_cparams("parallel", "parallel"),
        name="causal_conv_silu",
    )(src, halo_arr, conv_w.astype(F32), conv_b.reshape(1, width).astype(F32))


SSM_INNER = SSM_HEADS * SSM_HEAD_DIM
SSM_GN = SSM_GROUPS * SSM_STATE
HEADS_PER_GROUP = SSM_HEADS // SSM_GROUPS
GROUP_W = HEADS_PER_GROUP * SSM_HEAD_DIM


def _ssd_kernel(xs_ref, b_ref, c_ref, z_ref, dt_ref, dtt_ref, dtb_row_ref, dtb_col_ref, a_row_ref, a_col_ref,
                e_ref, dsk_ref, g_ref, h0_ref, y_ref, hout_ref, h_sc, y_sc, *pad_scs, valid):
    q = SSM_CHUNK
    hi = lax.Precision.HIGHEST
    c = pl.program_id(1)

    @pl.when(c == 0)
    def _():
        h_sc[...] = h0_ref[...]

    def rows(ref, sc):
        if valid == q:
            return ref[...]
        sc[...] = jnp.zeros(sc.shape, F32)
        sc[0:valid, :] = ref[...]
        return sc[...]

    if valid == q:
        pad_scs = (None,) * 5
    xs = rows(xs_ref, pad_scs[0])
    bm = rows(b_ref, pad_scs[1])
    cm = rows(c_ref, pad_scs[2])
    z = rows(z_ref, pad_scs[3])
    dt_raw = rows(dt_ref, pad_scs[4])

    row_i = lax.broadcasted_iota(jnp.int32, (q, q), 0)
    col_i = lax.broadcasted_iota(jnp.int32, (q, q), 1)
    causal = row_i >= col_i
    dt = jax.nn.softplus(dt_raw + dtb_row_ref[...])
    dtt = jax.nn.softplus(dtt_ref[...] + dtb_col_ref[...])
    if valid < q:
        dt = jnp.where(row_i < valid, dt, 0.0)
        dtt = jnp.where(lax.broadcasted_iota(jnp.int32, dtt.shape, 1) < valid, dtt, 0.0)
    acs = jnp.dot(causal.astype(F32), dt * a_row_ref[...], precision=hi, preferred_element_type=F32)
    acst = jnp.dot(dtt * a_col_ref[...], (row_i <= col_i).astype(F32), precision=hi,
                   preferred_element_type=F32)
    expand = e_ref[...]
    dt_e = jnp.dot(dt, expand, precision=hi, preferred_element_type=F32)
    dec_e = jnp.dot(jnp.exp(acs[q - 1:q, :] - acs), expand, precision=hi, preferred_element_type=F32)
    eacs_e = jnp.dot(jnp.exp(acs), expand, precision=hi, preferred_element_type=F32)
    xdt = xs * dt_e
    xdd = xdt * dec_e
    xdt_b = xdt.astype(BF16)
    lane = lax.broadcasted_iota(jnp.int32, (q, 2 * SSM_HEAD_DIM), 1)

    for g in range(SSM_GROUPS):
        bg = bm[:, g * SSM_STATE:(g + 1) * SSM_STATE].astype(BF16)
        cg = cm[:, g * SSM_STATE:(g + 1) * SSM_STATE].astype(BF16)
        cb = _dot_nt(cg, bg)
        for pair in range(HEADS_PER_GROUP // 2):
            h_a = g * HEADS_PER_GROUP + 2 * pair
            slab = slice(h_a * SSM_HEAD_DIM, (h_a + 2) * SSM_HEAD_DIM)
            xpair = xdt_b[:, slab]
            ypair = jnp.zeros((q, 2 * SSM_HEAD_DIM), F32)
            for which in range(2):
                h = h_a + which
                seg = acs[:, h:h + 1] - acst[h:h + 1, :]
                lmat = jnp.exp(jnp.where(causal, seg, -jnp.inf))
                mine = (lane >= which * SSM_HEAD_DIM) & (lane < (which + 1) * SSM_HEAD_DIM)
                ypair = ypair + jnp.dot((cb * lmat).astype(BF16), jnp.where(mine, xpair, jnp.zeros_like(xpair)),
                                        preferred_element_type=F32)
            y_sc[:, slab] = ypair
        gs = slice(g * GROUP_W, (g + 1) * GROUP_W)
        y_sc[:, gs] += _dot_nt(cg, h_sc[gs, :].astype(BF16)) * eacs_e[:, gs]
        st = jnp.dot(xdd[:, gs].T.astype(BF16), bg, preferred_element_type=F32)
        for hh in range(HEADS_PER_GROUP):
            h = g * HEADS_PER_GROUP + hh
            hs = slice(h * SSM_HEAD_DIM, (h + 1) * SSM_HEAD_DIM)
            dec = jnp.exp(acst[h:h + 1, q - 1:q])
            h_sc[hs, :] = h_sc[hs, :] * dec + st[hh * SSM_HEAD_DIM:(hh + 1) * SSM_HEAD_DIM, :]

    y = y_sc[...] + dsk_ref[...] * xs
    y = y * jax.nn.silu(z)
    gain = g_ref[...]
    for g in range(SSM_GROUPS):
        gs = slice(g * GROUP_W, (g + 1) * GROUP_W)
        yg = y[:, gs]
        yn = yg * lax.rsqrt(jnp.mean(yg * yg, axis=-1, keepdims=True) + NORM_EPS) * gain[:, gs]
        y_ref[:, gs] = yn[0:valid].astype(y_ref.dtype)

    @pl.when(c == pl.num_programs(1) - 1)
    def _():
        hout_ref[...] = h_sc[...]


def ssd_branch(xc, u, z_col0, u_row0, dt_pad, dt_t, h0, n_batch, n_chunks, valid, dt_bias, a_log, d_skip, g_ssm,
               out_dtype):
    q = SSM_CHUNK
    rows = xc.shape[0]
    assert rows == n_batch * n_chunks * valid and u_row0 % valid == 0 and z_col0 % SSM_INNER == 0
    rb = u_row0 // valid
    step = lambda b, c: b * n_chunks + c
    pad128 = lambda v: jnp.pad(v.astype(F32), (0, 128 - SSM_HEADS))
    a = -jnp.exp(a_log.astype(F32))
    head_lane = jnp.arange(SSM_INNER, dtype=jnp.int32) // SSM_HEAD_DIM
    expand = (jnp.arange(128, dtype=jnp.int32)[:, None] == head_lane[None, :]).astype(F32)
    const = lambda shape: pl.BlockSpec(shape, lambda b, c: (0,) * len(shape))
    pad_scs = [] if valid == q else [pltpu.VMEM((q, w), F32) for w in (SSM_INNER, SSM_GN, SSM_GN, SSM_INNER, 128)]
    y, h_fin = pl.pallas_call(
        functools.partial(_ssd_kernel, valid=valid),
        grid=(n_batch, n_chunks),
        in_specs=[
            pl.BlockSpec((valid, SSM_INNER), lambda b, c: (step(b, c), 0)),
            pl.BlockSpec((valid, SSM_GN), lambda b, c: (step(b, c), SSM_INNER // SSM_GN)),
            pl.BlockSpec((valid, SSM_GN), lambda b, c: (step(b, c), SSM_INNER // SSM_GN + 1)),
            pl.BlockSpec((valid, SSM_INNER), lambda b, c: (rb + step(b, c), z_col0 // SSM_INNER)),
            pl.BlockSpec((valid, 128), lambda b, c: (rb + step(b, c), 0)),
            pl.BlockSpec((None, SSM_HEADS, q), lambda b, c: (step(b, c), 0, 0)),
            const((1, 128)), const((SSM_HEADS, q)), const((1, 128)), const((SSM_HEADS, q)),
            const((128, SSM_INNER)), const((1, SSM_INNER)), const((1, SSM_INNER)),
            pl.BlockSpec((None, SSM_INNER, SSM_STATE), lambda b, c: (b, 0, 0)),
        ],
        out_specs=[pl.BlockSpec((valid, SSM_INNER), lambda b, c: (step(b, c), 0)),
                   pl.BlockSpec((None, SSM_INNER, SSM_STATE), lambda b, c: (b, 0, 0))],
        out_shape=[jax.ShapeDtypeStruct((rows, SSM_INNER), out_dtype),
                   jax.ShapeDtypeStruct((n_batch, SSM_INNER, SSM_STATE), F32)],
        scratch_shapes=[pltpu.VMEM((SSM_INNER, SSM_STATE), F32), pltpu.VMEM((q, SSM_INNER), F32)] + pad_scs,
        compiler_params=_cparams("parallel", "arbitrary"),
        name="ssd_branch",
    )(xc, xc, xc, u, dt_pad, dt_t,
      pad128(dt_bias).reshape(1, 128), jnp.broadcast_to(dt_bias.astype(F32)[:, None], (SSM_HEADS, q)),
      pad128(a).reshape(1, 128), jnp.broadcast_to(a[:, None], (SSM_HEADS, q)),
      expand, jnp.repeat(d_skip.astype(F32), SSM_HEAD_DIM).reshape(1, SSM_INNER),
      g_ssm.astype(F32).reshape(1, SSM_INNER), h0)
    return y, h_fin


def _mem_attn_kernel(q_ref, k_ref, v_ref, o_ref):
    for h in range(MEM_HEADS):
        hs = slice(h * MEM_HEAD_DIM, (h + 1) * MEM_HEAD_DIM)
        s = _dot_nt(q_ref[:, hs].astype(BF16), k_ref[:, hs].astype(BF16))
        p = jnp.exp(s - jnp.max(s, axis=-1, keepdims=True))
        o = jnp.dot(p.astype(BF16), v_ref[:, hs].astype(BF16), preferred_element_type=F32)
        o_ref[:, hs] = (o / jnp.sum(p, axis=-1, keepdims=True)).astype(o_ref.dtype)


def memory_attention(qn, q_row0, rows, tq, mk, mv, v_col_blk, tiles_per_batch, out_dtype):
    n_mem = mk.shape[0] * tq * tiles_per_batch // rows
    rb = q_row0 // tq
    return pl.pallas_call(
        _mem_attn_kernel,
        grid=(rows // tq,),
        in_specs=[pl.BlockSpec((tq, MEM_W), lambda i: (rb + i, 0)),
                  pl.BlockSpec((n_mem, MEM_W), lambda i: (i // tiles_per_batch, 0)),
                  pl.BlockSpec((n_mem, MEM_W), lambda i: (i // tiles_per_batch, v_col_blk))],
        out_specs=pl.BlockSpec((tq, MEM_W), lambda i: (i, 0)),
        out_shape=jax.ShapeDtypeStruct((rows, MEM_W), out_dtype),
        compiler_params=_cparams("parallel"),
        name="memory_attention",
    )(qn, mk, mv)


SAMPLE_PAGES_PER_STEP = 8
SAMPLE_ROWS = DA_HEADS * 2 * 8


def _sample_attn_kernel(pt_ref, lam_ref, q_ref, *refs, pages_per_step, post_scale):
    n = pages_per_step
    kt_refs, v_refs = refs[:n], refs[n:2 * n]
    knew_ref, vnew_ref, blast_ref, bnew_ref, g_ref, o_ref, m_sc, l_sc, acc_sc = refs[2 * n:]
    del pt_ref
    step = pl.program_id(1)
    n_new = o_ref.shape[0]
    rows_per_head = 2 * n_new

    @pl.when(step == 0)
    def _():
        m_sc[...] = jnp.full(m_sc.shape, NEG_BIG, F32)
        l_sc[...] = jnp.zeros(l_sc.shape, F32)
        acc_sc[...] = jnp.zeros(acc_sc.shape, F32)

    q = q_ref[...]

    def update(kt_ref, v_ref, bias):
        s = jnp.dot(q, kt_ref[...].astype(BF16), preferred_element_type=F32)
        if bias is not None:
            s = s + bias
        m_prev = m_sc[...]
        m_new = jnp.maximum(m_prev, jnp.max(s, axis=-1, keepdims=True))
        alpha = jnp.exp(m_prev - m_new)
        p = jnp.exp(s - m_new)
        l_sc[...] = alpha * l_sc[...] + jnp.sum(p, axis=-1, keepdims=True)
        m_sc[...] = m_new
        pb = p.astype(BF16)
        for h in range(DA_HEADS):
            rs = slice(h * rows_per_head, (h + 1) * rows_per_head)
            acc_sc[rs, :] = alpha[rs] * acc_sc[rs, :] + jnp.dot(pb[rs, :], v_ref[:, h, :].astype(BF16),
                                                                 preferred_element_type=F32)

    last = pl.num_programs(1) - 1

    @pl.when(step < last)
    def _():
        for i in range(n):
            update(kt_refs[i], v_refs[i], None)

    @pl.when(step == last)
    def _():
        for i in range(n - 1):
            update(kt_refs[i], v_refs[i], None)
        update(kt_refs[n - 1], v_refs[n - 1], blast_ref[...])
        update(knew_ref, vnew_ref, bnew_ref[...])
        for h in range(DA_HEADS):
            r0 = h * rows_per_head
            o = (acc_sc[r0:r0 + n_new, :] / l_sc[r0:r0 + n_new, :]
                 - lam_ref[0] * (acc_sc[r0 + n_new:r0 + 2 * n_new, :] / l_sc[r0 + n_new:r0 + 2 * n_new, :]))
            y = o * lax.rsqrt(jnp.mean(o * o, axis=-1, keepdims=True) + NORM_EPS)
            o_ref[:, h * DA_DV:(h + 1) * DA_DV] = (y * g_ref[...]) * post_scale


def sample_bias_tiles(table, n_new, page):
    far = _far_bias(table)
    r = jnp.arange(SAMPLE_ROWS, dtype=jnp.int32)
    head, t = r // (2 * n_new), r % n_new
    j = jnp.arange(page, dtype=jnp.int32)
    tab = table.astype(F32)
    dist_last = t[:, None] + page - j[None, :]
    b_last = tab[_t5_bucket(dist_last), head[:, None]] - far[head][:, None]
    dist_new = t[:, None] - j[None, :]
    b_new = tab[_t5_bucket(dist_new), head[:, None]] - far[head][:, None]
    b_new = jnp.where((dist_new >= 0) & (j[None, :] < n_new), b_new, NEG_BIG)
    return b_last, b_new


def sample_diff_attention(layer, qn_s, kn_s, v_s, lam, k_pool_t, v_pool, page_table, b_last, b_new, g_sub,
                          post_scale):
    bsz, n_new, qw = qn_s.shape
    page = v_pool.shape[2]
    n_pages = page_table.shape[1]
    n = SAMPLE_PAGES_PER_STEP
    assert n_pages % n == 0 and SAMPLE_ROWS == DA_HEADS * 2 * n_new
    qg = qn_s.reshape(bsz, n_new, DA_HEADS * 2, DA_DK)
    eye = jnp.eye(DA_HEADS * 2, dtype=F32)
    q_rows = (qg.transpose(0, 2, 1, 3)[:, :, :, None, :] * eye[None, :, None, :, None]).reshape(
        bsz, SAMPLE_ROWS, qw).astype(BF16)
    knew_t = jnp.pad(kn_s.transpose(0, 2, 1), ((0, 0), (0, 0), (0, page - n_new)))
    vnew = jnp.pad(v_s, ((0, 0), (0, page - n_new), (0, 0), (0, 0)))
    kspec = lambda i: pl.BlockSpec((None, None, qw, page), lambda b, s, pt: (layer, pt[b, s * n + i], 0, 0))
    vspec = lambda i: pl.BlockSpec((None, None, page, DA_HEADS, DA_DV),
                                   lambda b, s, pt: (layer, pt[b, s * n + i], 0, 0, 0))
    const2 = lambda shape: pl.BlockSpec(shape, lambda b, s, pt: (0, 0))
    grid_spec = pltpu.PrefetchScalarGridSpec(
        num_scalar_prefetch=1,
        grid=(bsz, n_pages // n),
        in_specs=[pl.BlockSpec(memory_space=pltpu.SMEM),
                  pl.BlockSpec((None, SAMPLE_ROWS, qw), lambda b, s, pt: (b, 0, 0))]
                 + [kspec(i) for i in range(n)] + [vspec(i) for i in range(n)]
                 + [pl.BlockSpec((None, qw, page), lambda b, s, pt: (b, 0, 0)),
                    pl.BlockSpec((None, page, DA_HEADS, DA_DV), lambda b, s, pt: (b, 0, 0, 0)),
                    const2((SAMPLE_ROWS, page)), const2((SAMPLE_ROWS, page)), const2((1, DA_DV))],
        out_specs=pl.BlockSpec((n_new, DA_VW), lambda b, s, pt: (b, 0)),
        scratch_shapes=[pltpu.VMEM((SAMPLE_ROWS, 1), F32), pltpu.VMEM((SAMPLE_ROWS, 1), F32),
                        pltpu.VMEM((SAMPLE_ROWS, DA_DV), F32)],
    )
    return pl.pallas_call(
        functools.partial(_sample_attn_kernel, pages_per_step=n, post_scale=post_scale),
        grid_spec=grid_spec,
        out_shape=jax.ShapeDtypeStruct((bsz * n_new, DA_VW), F32),
        compiler_params=_cparams("parallel", "arbitrary"),
        name="sample_diff_attention",
    )(page_table, lam.reshape(1).astype(F32), q_rows, *([k_pool_t] * n), *([v_pool] * n), knew_t, vnew,
      b_last, b_new, g_sub.reshape(1, DA_DV).astype(F32))


def _rmsnorm(x, g):
    xf = x.astype(F32)
    y = xf * lax.rsqrt(jnp.mean(xf * xf, axis=-1, keepdims=True) + NORM_EPS)
    return (y * g.astype(F32)).astype(x.dtype)


def _t5_bias(dist, table):
    return jnp.moveaxis(table.astype(F32)[_t5_bucket(dist)], -1, 0)


def _diff_attn_sample(q, k, v, lam, k_pool, v_pool, layer, page_table, bias_table):
    b, t = q.shape[:2]
    past = page_table.shape[1] * k_pool.shape[2]
    k_past = k_pool[layer, page_table].reshape(b, past, DA_HEADS, 2, DA_DK)
    v_past = v_pool[layer, page_table].reshape(b, past, DA_HEADS, DA_DV)
    tpos = jnp.arange(t, dtype=jnp.int32)
    dist_past = (past + tpos)[:, None] - jnp.arange(past, dtype=jnp.int32)[None, :]
    dist_new = tpos[:, None] - tpos[None, :]
    lp = jnp.einsum('bqhmd,bkhmd->bhmqk', q, k_past, preferred_element_type=F32) * DA_SCALE
    lp = lp + _t5_bias(dist_past, bias_table)[None, :, None]
    ln = jnp.einsum('bqhmd,bkhmd->bhmqk', q, k, preferred_element_type=F32) * DA_SCALE
    ln = jnp.where(dist_new >= 0, ln + _t5_bias(dist_new, bias_table)[None, :, None], -jnp.inf)
    prob = jax.nn.softmax(jnp.concatenate([lp, ln], axis=-1), axis=-1)
    attn = (prob[:, :, 0] - lam * prob[:, :, 1]).astype(v.dtype)
    return (jnp.einsum('bhqk,bkhd->bqhd', attn[..., :past], v_past)
            + jnp.einsum('bhqk,bkhd->bqhd', attn[..., past:], v))


def _memory_attend(q, mk, mv):
    logits = jnp.einsum('bqhd,bmhd->bhqm', q, mk, preferred_element_type=F32) * MEM_SCALE
    prob = jax.nn.softmax(logits, axis=-1)
    return jnp.einsum('bhqm,bmhd->bqhd', prob.astype(mv.dtype), mv)


def _ssd_scan(xh, dt, a, bm, cm, h0):
    b, L = xh.shape[:2]
    q = min(SSM_CHUNK, L)
    assert L % q == 0
    c = L // q
    hpg = SSM_HEADS // SSM_GROUPS
    xdt = (xh * dt[..., None]).reshape(b, c, q, SSM_GROUPS, hpg, SSM_HEAD_DIM)
    acs = jnp.cumsum((dt * a).reshape(b, c, q, SSM_GROUPS, hpg), axis=2)
    bc = bm.reshape(b, c, q, SSM_GROUPS, SSM_STATE)
    cc = cm.reshape(b, c, q, SSM_GROUPS, SSM_STATE)
    causal = jnp.tril(jnp.ones((q, q), dtype=bool))[:, :, None, None]
    seg = acs[:, :, :, None] - acs[:, :, None, :]
    lmat = jnp.exp(jnp.where(causal, seg, -jnp.inf))
    cb = jnp.einsum('bclgn,bcsgn->bclsg', cc, bc)
    y_diag = jnp.einsum('bclsgh,bcsghp->bclghp', cb[..., None] * lmat, xdt)
    decay_s = jnp.exp(acs[:, :, -1:] - acs)
    states = jnp.einsum('bcsgn,bcsghp->bcghpn', bc, decay_s[..., None] * xdt)
    chunk_decay = jnp.exp(acs[:, :, -1])

    def step(h, inp):
        dec, st = inp
        return h * dec[..., None, None] + st, h

    h_last, h_in = lax.scan(step, h0.reshape(b, SSM_GROUPS, hpg, SSM_HEAD_DIM, SSM_STATE),
                            (jnp.moveaxis(chunk_decay, 1, 0), jnp.moveaxis(states, 1, 0)))
    h_in = jnp.moveaxis(h_in, 0, 1)
    y_off = jnp.einsum('bclgn,bcghpn->bclghp', cc, h_in) * jnp.exp(acs)[..., None]
    y = (y_diag + y_off).reshape(b, c * q, SSM_HEADS, SSM_HEAD_DIM)
    return y, h_last.reshape(b, SSM_HEADS, SSM_HEAD_DIM, SSM_STATE)


def _mamba_branch(z, xbc, dt_raw, conv_prev, h0, conv_w, conv_b, dt_bias, a_log, d_skip, g_ssm):
    b, L = xbc.shape[:2]
    inner = SSM_HEADS * SSM_HEAD_DIM
    gn = SSM_GROUPS * SSM_STATE
    xpad = jnp.concatenate([conv_prev.astype(xbc.dtype), xbc], axis=1)
    conv = conv_b + conv_w[0] * xpad[:, 0:L]
    for j in range(1, SSM_CONV):
        conv = conv + conv_w[j] * xpad[:, j:j + L]
    conv_new = xpad[:, L:]
    xc = jax.nn.silu(conv)
    xs, bm, cm = xc[..., :inner], xc[..., inner:inner + gn], xc[..., inner + gn:]
    xh = xs.reshape(b, L, SSM_HEADS, SSM_HEAD_DIM).astype(F32)
    dt = jax.nn.softplus(dt_raw.astype(F32) + dt_bias.astype(F32))
    a = -jnp.exp(a_log.astype(F32))
    y, h_fin = _ssd_scan(xh, dt, a, bm.reshape(b, L, SSM_GROUPS, SSM_STATE).astype(F32),
                         cm.reshape(b, L, SSM_GROUPS, SSM_STATE).astype(F32), h0.astype(F32))
    y = y + d_skip.astype(F32)[:, None] * xh
    y = y.reshape(b, L, inner).astype(z.dtype) * jax.nn.silu(z)
    y = _rmsnorm(y.reshape(b, L, SSM_GROUPS, inner // SSM_GROUPS), g_ssm.reshape(SSM_GROUPS, inner // SSM_GROUPS))
    return y.reshape(b, L, inner), h_fin, conv_new


FA_TILE = 512


def kernel(x_prompt, x_sample, cache_attn_k, cache_attn_v, cache_mem_k, cache_mem_v, state_ssm, state_conv, page_table, mem_prompt, rel_bias, g_mix, w_in, g_q, g_k, lam_q1, lam_k1, lam_q2, lam_k2, g_sub, conv_w, conv_b, dt_bias, a_log, d_skip, g_ssm, g_mem, w_mem_kv, g_mq, g_mk, w_br_attn, w_br_ssm, w_br_mem, w_out, g_ffn, peer_wq, peer_k1, peer_k2, peer_u, peer_v):
    depth = w_in.shape[0]
    bp, sp, d = x_prompt.shape
    bs, ts, _ = x_sample.shape
    assert bp == 1 and sp % SSM_CHUNK == 0 and sp % FA_TILE == 0 and ts >= SSM_CONV - 1
    n_p = bp * sp
    n_s = bs * ts
    conv_dim = SSM_INNER + 2 * SSM_GN
    n_mem = mem_prompt.shape[1]
    n_pool, page = cache_attn_v.shape[1], cache_attn_v.shape[2]
    s_z = 2 * DA_QW + DA_VW
    s_dt = s_z + SSM_INNER + conv_dim
    s_mq = s_dt + SSM_HEADS
    c_z, c_xbc = 0, SSM_INNER
    c_q = c_xbc + conv_dim
    c_k = c_q + DA_QW
    c_v = c_k + DA_QW
    c_mq = c_v + DA_VW
    c_gate = c_mq + MEM_W

    x = jnp.concatenate([x_prompt.reshape(n_p, d), x_sample.reshape(n_s, d)], axis=0)
    bias_tiles = prompt_bias_tiles(rel_bias, FA_TILE)
    b_last, b_new = sample_bias_tiles(rel_bias, ts, page)
    k_pool_t = jnp.transpose(cache_attn_k, (0, 1, 3, 4, 5, 2)).reshape(depth, n_pool, DA_QW, page)
    zero_h = jnp.zeros((bp, SSM_INNER, SSM_STATE), F32)

    outs = {k: [] for k in ("pk", "pv", "pmk", "pmv", "ph", "pc", "sk", "sv", "sh", "sc")}
    for l in range(depth):
        lam_init = 0.8 - 0.6 * math.exp(-0.3 * l)
        lam = (jnp.exp(jnp.sum(lam_q1[l].astype(F32) * lam_k1[l].astype(F32)))
               - jnp.exp(jnp.sum(lam_q2[l].astype(F32) * lam_k2[l].astype(F32))) + lam_init)
        wl = w_in[l]
        w_main = jnp.concatenate([wl[:, s_z:s_dt], wl[:, :s_z], wl[:, s_mq:]], axis=1).astype(BF16)
        w_dt = jnp.pad(wl[:, s_dt:s_mq], ((0, 0), (0, 128 - SSM_HEADS))).astype(BF16)

        h = rmsnorm_rows(x, g_mix[l])
        u = matmul(h, w_main, name="in_proj")
        dt_pad = matmul(h, w_dt, name="dt_proj")
        (qn,) = group_norm(u, c_q, DA_QW, DA_DK, g_q[l], [(BF16, DA_SCALE)])
        kn, kn_b = group_norm(u, c_k, DA_QW, DA_DK, g_k[l], [(F32, 1.0), (BF16, 1.0)])
        (mqn,) = group_norm(u, c_mq, MEM_W, MEM_HEAD_DIM, g_mq[l], [(F32, MEM_SCALE)])
        v_all = u[:, c_v:c_v + DA_VW]
        xbc_s = u[n_p:, c_xbc:c_xbc + conv_dim].reshape(bs, ts, conv_dim)

        a_p = prompt_diff_attention(qn, kn_b, u, c_v, n_p, lam, bias_tiles, g_sub[l], 1.0 - lam_init, FA_TILE)
        a_s = sample_diff_attention(l, qn[n_p:].astype(F32).reshape(bs, ts, DA_QW), kn[n_p:].reshape(bs, ts, DA_QW),
                                    v_all[n_p:].reshape(bs, ts, DA_HEADS, DA_DV), lam, k_pool_t, cache_attn_v,
                                    page_table, b_last, b_new, g_sub[l], 1.0 - lam_init)

        hm = rmsnorm_rows(mem_prompt.reshape(bp * n_mem, d), g_mem[l])
        mkv = matmul(hm, w_mem_kv[l].astype(BF16), name="mem_kv")
        (mk_p,) = group_norm(mkv, 0, MEM_W, MEM_HEAD_DIM, g_mk[l], [(F32, 1.0)])
        tq = 512
        m_p = memory_attention(mqn, 0, n_p, tq, mk_p, mkv, 1, sp // tq, BF16)
        m_s = memory_attention(mqn, n_p, n_s, ts, cache_mem_k[l].reshape(bs * n_mem, MEM_W),
                               cache_mem_v[l].reshape(bs * n_mem, MEM_W), 0, 1, F32)

        xc_p = causal_conv_silu(u, c_xbc, conv_dim, 0, n_p, 512, None, conv_w[l], conv_b[l])
        halo_s = jnp.pad(state_conv[l], ((0, 0), (CONV_HALO - (SSM_CONV - 1), 0), (0, 0))).reshape(
            bs * CONV_HALO, conv_dim)
        xc_s = causal_conv_silu(u, c_xbc, conv_dim, n_p, n_s, ts, halo_s, conv_w[l], conv_b[l])
        dt_raw = dt_pad[:, :SSM_HEADS]
        dtt_p = dt_raw[:n_p].reshape(n_p // SSM_CHUNK, SSM_CHUNK, SSM_HEADS).transpose(0, 2, 1)
        dtt_s = jnp.pad(dt_raw[n_p:].reshape(bs, ts, SSM_HEADS).transpose(0, 2, 1),
                        ((0, 0), (0, 0), (0, SSM_CHUNK - ts)))
        s_p, h_p = ssd_branch(xc_p, u, c_z, 0, dt_pad, dtt_p, zero_h, bp, sp // SSM_CHUNK, SSM_CHUNK,
                              dt_bias[l], a_log[l], d_skip[l], g_ssm[l], BF16)
        s_s, h_s = ssd_branch(xc_s, u, c_z, n_p, dt_pad, dtt_s, state_ssm[l].reshape(bs, SSM_INNER, SSM_STATE),
                              bs, 1, ts, dt_bias[l], a_log[l], d_skip[l], g_ssm[l], F32)

        a_out = jnp.concatenate([a_p, a_s.astype(BF16)], axis=0)
        s_out = jnp.concatenate([s_p, s_s.astype(BF16)], axis=0)
        m_out = jnp.concatenate([m_p, m_s.astype(BF16)], axis=0)
        merged = merge_branches(a_out, s_out, m_out, u, c_gate, w_br_attn[l].astype(BF16),
                                w_br_ssm[l].astype(BF16), w_br_mem[l].astype(BF16))
        x = matmul(merged, w_out[l].astype(BF16), residual=x, name="out_proj")
        x = peer_layer(x, g_ffn[l], peer_wq[l].astype(BF16), peer_k1[l].astype(BF16), peer_k2[l].astype(BF16),
                       peer_u[l].astype(BF16), peer_v[l].astype(BF16))

        outs["pk"].append(kn[:n_p].reshape(bp, sp, DA_HEADS, 2, DA_DK))
        outs["pv"].append(v_all[:n_p].reshape(bp, sp, DA_HEADS, DA_DV))
        outs["pmk"].append(mk_p.reshape(bp, n_mem, MEM_HEADS, MEM_HEAD_DIM))
        outs["pmv"].append(mkv[:, MEM_W:].reshape(bp, n_mem, MEM_HEADS, MEM_HEAD_DIM))
        outs["ph"].append(h_p.reshape(bp, SSM_HEADS, SSM_HEAD_DIM, SSM_STATE))
        outs["pc"].append(u[n_p - (SSM_CONV - 1):n_p, c_xbc:c_xbc + conv_dim].reshape(bp, SSM_CONV - 1, conv_dim))
        outs["sk"].append(kn[n_p:].reshape(bs, ts, DA_HEADS, 2, DA_DK))
        outs["sv"].append(v_all[n_p:].reshape(bs, ts, DA_HEADS, DA_DV))
        outs["sh"].append(h_s.reshape(bs, SSM_HEADS, SSM_HEAD_DIM, SSM_STATE))
        outs["sc"].append(xbc_s[:, ts - (SSM_CONV - 1):])

    y_prompt = x[:n_p].reshape(bp, sp, d)
    y_sample = x[n_p:].reshape(bs, ts, d)
    st = lambda name: jnp.stack(outs[name])
    return (y_prompt, y_sample, st("pk"), st("pv"), st("pmk"), st("pmv"), st("ph"), st("pc"),
            st("sk"), st("sv"), st("sh"), st("sc"))
```

```python
import functools
import math

import jax
import jax.numpy as jnp
from jax import lax
from jax.experimental import pallas as pl
from jax.experimental.pallas import tpu as pltpu

F32 = jnp.float32
BF16 = jnp.bfloat16

NORM_EPS = 1e-6
NEG_BIG = -1e30
V7X_VMEM_LIMIT_BYTES = 56 * 1024 * 1024

DA_HEADS, DA_DK, DA_DV = 8, 64, 128
DA_QW = DA_HEADS * 2 * DA_DK
DA_VW = DA_HEADS * DA_DV
DA_SCALE = DA_DK ** -0.5
N_BUCKETS, MAX_DISTANCE = 32, 128
SSM_HEADS, SSM_HEAD_DIM, SSM_GROUPS, SSM_STATE, SSM_CONV, SSM_CHUNK = 32, 64, 4, 128, 4, 128
MEM_HEADS, MEM_HEAD_DIM = 4, 256
MEM_W = MEM_HEADS * MEM_HEAD_DIM
MEM_SCALE = MEM_HEAD_DIM ** -0.5
PEER_HEADS, PEER_KEYS, PEER_TOPK, PEER_HALF = 8, 128, 16, 128


def _cparams(*sem):
    return pltpu.CompilerParams(dimension_semantics=sem, vmem_limit_bytes=V7X_VMEM_LIMIT_BYTES)


def _dot_nt(a, b):
    return lax.dot_general(a, b, (((1,), (1,)), ((), ())), preferred_element_type=F32)


def _pick_tile(n, candidates):
    for c in candidates:
        if n % c == 0:
            return c
    return n


def _rmsnorm_kernel(x_ref, g_ref, o_ref):
    x = x_ref[...]
    ms = jnp.mean(x * x, axis=-1, keepdims=True)
    o_ref[...] = (x * lax.rsqrt(ms + NORM_EPS) * g_ref[...]).astype(o_ref.dtype)


def rmsnorm_rows(x, g, out_dtype=BF16):
    t, d = x.shape
    tm = _pick_tile(t, (1056, 1024, 512, 256, 128))
    return pl.pallas_call(
        _rmsnorm_kernel,
        grid=(t // tm,),
        in_specs=[pl.BlockSpec((tm, d), lambda i: (i, 0)), pl.BlockSpec((1, d), lambda i: (0, 0))],
        out_specs=pl.BlockSpec((tm, d), lambda i: (i, 0)),
        out_shape=jax.ShapeDtypeStruct((t, d), out_dtype),
        compiler_params=_cparams("parallel"),
        name="rmsnorm_rows",
    )(x, g.reshape(1, d).astype(F32))


def _mm_kernel(a_ref, w_ref, o_ref):
    o_ref[...] = jnp.dot(a_ref[...], w_ref[...], preferred_element_type=F32).astype(o_ref.dtype)


def _mm_res_kernel(a_ref, w_ref, r_ref, o_ref):
    o_ref[...] = r_ref[...] + jnp.dot(a_ref[...], w_ref[...], preferred_element_type=F32)


def matmul(a, w, residual=None, out_dtype=F32, name="matmul"):
    t, k = a.shape
    n = w.shape[1]
    tm = _pick_tile(t, (528, 512, 256, 128))
    tn = _pick_tile(n, (1536, 1024, 512, 256, 128))
    in_specs = [pl.BlockSpec((tm, k), lambda i, j: (i, 0)), pl.BlockSpec((k, tn), lambda i, j: (0, j))]
    args = [a, w]
    body = _mm_kernel
    if residual is not None:
        in_specs.append(pl.BlockSpec((tm, tn), lambda i, j: (i, j)))
        args.append(residual)
        body = _mm_res_kernel
    return pl.pallas_call(
        body,
        grid=(t // tm, n // tn),
        in_specs=in_specs,
        out_specs=pl.BlockSpec((tm, tn), lambda i, j: (i, j)),
        out_shape=jax.ShapeDtypeStruct((t, n), out_dtype),
        compiler_params=_cparams("parallel", "parallel"),
        name=name,
    )(*args)


def _merge_kernel(a_ref, s_ref, m_ref, ga_ref, gs_ref, gm_ref, wa_ref, ws_ref, wm_ref, o_ref):
    acc = jax.nn.sigmoid(ga_ref[...]) * jnp.dot(a_ref[...], wa_ref[...], preferred_element_type=F32)
    acc += jax.nn.sigmoid(gs_ref[...]) * jnp.dot(s_ref[...], ws_ref[...], preferred_element_type=F32)
    acc += jax.nn.sigmoid(gm_ref[...]) * jnp.dot(m_ref[...], wm_ref[...], preferred_element_type=F32)
    o_ref[...] = acc.astype(o_ref.dtype)


def merge_branches(a, s, m, u, gate_col0, wa, ws, wm):
    t = a.shape[0]
    d = wa.shape[1]
    tm = _pick_tile(t, (528, 512, 256, 128))
    tn = 512
    gb = gate_col0 // tn
    nd = d // tn
    row = lambda width: pl.BlockSpec((tm, width), lambda i, j: (i, 0))
    gate = lambda which: pl.BlockSpec((tm, tn), lambda i, j: (i, gb + which * nd + j))
    wcol = lambda kdim: pl.BlockSpec((kdim, tn), lambda i, j: (0, j))
    return pl.pallas_call(
        _merge_kernel,
        grid=(t // tm, nd),
        in_specs=[row(a.shape[1]), row(s.shape[1]), row(m.shape[1]), gate(0), gate(1), gate(2),
                  wcol(wa.shape[0]), wcol(ws.shape[0]), wcol(wm.shape[0])],
        out_specs=pl.BlockSpec((tm, tn), lambda i, j: (i, j)),
        out_shape=jax.ShapeDtypeStruct((t, d), BF16),
        compiler_params=_cparams("parallel", "parallel"),
        name="merge_branches",
    )(a, s, m, u, u, u, wa, ws, wm)


def _t5_bucket(dist):
    n = jnp.maximum(dist, 0)
    exact = N_BUCKETS // 2
    nf = jnp.maximum(n, exact).astype(F32)
    large = exact + (jnp.log(nf / exact) / math.log(MAX_DISTANCE / exact) * (N_BUCKETS - exact)).astype(jnp.int32)
    return jnp.where(n < exact, n, jnp.minimum(large, N_BUCKETS - 1))


def _far_bias(table):
    return table.astype(F32)[N_BUCKETS - 1]


LOG2E = math.log2(math.e)


def _toeplitz_bias(table, off, t):
    x = jnp.arange(2 * t, dtype=jnp.int32)
    dist = jnp.where(x < t, off - x, off + 2 * t - x)
    far = _far_bias(table)
    v = (jnp.moveaxis(table.astype(F32)[_t5_bucket(dist)], -1, 0) - far[:, None]) * LOG2E
    v = jnp.where(dist[None, :] >= 0, v, NEG_BIG)
    h = v.shape[0]
    return jnp.tile(v, (1, t))[:, :t * (2 * t - 1)].reshape(h, t, 2 * t - 1)[:, :, :t]


def prompt_bias_tiles(table, tile):
    assert tile >= MAX_DISTANCE
    return jnp.stack([_toeplitz_bias(table, 0, tile), _toeplitz_bias(table, tile, tile)], axis=1)


def _fa_kernel(qt_ref, kt_ref, lam_ref, q_ref, k_ref, v_ref, b_ref, g_ref, o_ref, m_sc, l_sc, acc_sc, *,
               post_scale):
    s_id = pl.program_id(1)
    qi = qt_ref[s_id]
    ki = kt_ref[s_id]

    @pl.when(ki == 0)
    def _():
        m_sc[...] = jnp.full(m_sc.shape, NEG_BIG, F32)
        l_sc[...] = jnp.zeros(l_sc.shape, F32)
        acc_sc[...] = jnp.zeros(acc_sc.shape, F32)

    def update(bias_idx):
        q = q_ref[...]
        k = k_ref[...]
        v = v_ref[...].astype(BF16)
        lane = lax.broadcasted_iota(jnp.int32, q.shape, 1)
        n_lane_tiles = k.shape[0] // 128
        logits = []
        for m in range(2):
            in_map = (lane >= m * DA_DK) & (lane < (m + 1) * DA_DK)
            s = _dot_nt(jnp.where(in_map, q, jnp.zeros_like(q)), k)
            logits.append(s if bias_idx is None else s + b_ref[bias_idx])
        for m in range(2):
            s = logits[m]
            m_prev = m_sc[m]
            m_new = jnp.maximum(m_prev, jnp.max(s, axis=-1, keepdims=True))
            alpha = jnp.exp2(m_prev - m_new)
            p = jnp.exp2(s - jnp.tile(m_new, (1, n_lane_tiles)))
            l_part = p[:, 0:128]
            for t in range(1, n_lane_tiles):
                l_part = l_part + p[:, t * 128:(t + 1) * 128]
            l_sc[m] = alpha * l_sc[m] + l_part
            acc_sc[m] = alpha * acc_sc[m] + jnp.dot(p.astype(BF16), v, preferred_element_type=F32)
            m_sc[m] = m_new

    @pl.when(ki < qi - 1)
    def _():
        update(None)

    @pl.when(ki == qi - 1)
    def _():
        update(1)

    @pl.when(ki == qi)
    def _():
        update(0)
        l0 = jnp.sum(l_sc[0], axis=-1, keepdims=True)
        l1 = jnp.sum(l_sc[1], axis=-1, keepdims=True)
        o = acc_sc[0] / l0 - lam_ref[0] * (acc_sc[1] / l1)
        y = o * lax.rsqrt(jnp.mean(o * o, axis=-1, keepdims=True) + NORM_EPS)
        o_ref[...] = ((y * g_ref[...]) * post_scale).astype(o_ref.dtype)


def prompt_diff_attention(qn, kn, v_src, v_col0, s_len, lam, bias_tiles, g_sub, post_scale, tile):
    assert v_col0 % DA_DV == 0
    vb = v_col0 // DA_DV
    nq = s_len // tile
    pairs = [(qi, ki) for qi in range(nq) for ki in range(qi + 1)]
    qt = jnp.asarray([p[0] for p in pairs], jnp.int32)
    kt = jnp.asarray([p[1] for p in pairs], jnp.int32)
    grid_spec = pltpu.PrefetchScalarGridSpec(
        num_scalar_prefetch=2,
        grid=(DA_HEADS, len(pairs)),
        in_specs=[
            pl.BlockSpec(memory_space=pltpu.SMEM),
            pl.BlockSpec((tile, 128), lambda h, s, qt, kt: (qt[s], h)),
            pl.BlockSpec((tile, 128), lambda h, s, qt, kt: (kt[s], h)),
            pl.BlockSpec((tile, DA_DV), lambda h, s, qt, kt: (kt[s], vb + h)),
            pl.BlockSpec((None, 2, tile, tile), lambda h, s, qt, kt: (h, 0, 0, 0)),
            pl.BlockSpec((1, DA_DV), lambda h, s, qt, kt: (0, 0)),
        ],
        out_specs=pl.BlockSpec((tile, 128), lambda h, s, qt, kt: (qt[s], h)),
        scratch_shapes=[pltpu.VMEM((2, tile, 128), F32), pltpu.VMEM((2, tile, 128), F32),
                        pltpu.VMEM((2, tile, DA_DV), F32)],
    )
    return pl.pallas_call(
        functools.partial(_fa_kernel, post_scale=post_scale),
        grid_spec=grid_spec,
        out_shape=jax.ShapeDtypeStruct((s_len, DA_VW), BF16),
        compiler_params=_cparams("parallel", "arbitrary"),
        name="prompt_diff_attention",
    )(qt, kt, lam.reshape(1).astype(F32), qn, kn, v_src, bias_tiles, g_sub.reshape(1, DA_DV).astype(F32))


def _topk_rows(s, k):
    riota = lax.broadcasted_iota(jnp.int32, s.shape, 0).astype(F32)
    big = float(s.shape[0])
    vals, idxs = [], []
    for _ in range(k):
        m = jnp.max(s, axis=0, keepdims=True)
        idx = jnp.min(jnp.where(s == m, riota, big), axis=0, keepdims=True)
        vals.append(m)
        idxs.append(idx)
        s = jnp.where(riota == idx, -jnp.inf, s)
    return vals, idxs


def _pruned_pair_candidates(v1, i1, v2, i2):
    k = PEER_TOPK
    tt = v1[0].shape[1]
    cat = lambda rows: jnp.concatenate(rows, axis=0)
    v1m, i1m, v2m, i2m = cat(v1), cat(i1), cat(v2), cat(i2)
    row16 = lax.broadcasted_iota(jnp.int32, (k, tt), 0)
    row8 = lax.broadcasted_iota(jnp.int32, (k // 2, tt), 0)
    half = k // 2
    cand = [v1[0] + v2m]
    c1 = [jnp.broadcast_to(i1[0], (k, tt))]
    c2 = [i2m]
    rank = [row16]
    for a in range(1, half):
        cand.append(jnp.where(row8 < k // (a + 1), v1[a] + v2m[:half], -jnp.inf))
        c1.append(jnp.broadcast_to(i1[a], (half, tt)))
        c2.append(i2m[:half])
        rank.append(a * k + row8)
    cand.append(v1m[half:] + v2[0])
    c1.append(i1m[half:])
    c2.append(jnp.broadcast_to(i2[0], (half, tt)))
    rank.append((half + row8) * k)
    return cat(cand), cat(c1), cat(c2), cat(rank).astype(F32)


def _peer_route_kernel(q_ref, k1_ref, k2_ref, gate_ref, e1_ref, e2_ref):
    k1 = k1_ref[...]
    k2 = k2_ref[...]
    gates, e1s, e2s = [], [], []
    for h in range(PEER_HEADS):
        qa = q_ref[:, (2 * h) * PEER_HALF:(2 * h + 1) * PEER_HALF].astype(BF16)
        qb = q_ref[:, (2 * h + 1) * PEER_HALF:(2 * h + 2) * PEER_HALF].astype(BF16)
        v1, i1 = _topk_rows(_dot_nt(k1, qa), PEER_TOPK)
        v2, i2 = _topk_rows(_dot_nt(k2, qb), PEER_TOPK)
        cand, c1, c2, rank = _pruned_pair_candidates(v1, i1, v2, i2)
        sc, ea, eb = [], [], []
        for _ in range(PEER_TOPK):
            m = jnp.max(cand, axis=0, keepdims=True)
            first = jnp.min(jnp.where(cand == m, rank, float(PEER_TOPK * PEER_TOPK)), axis=0, keepdims=True)
            sel = rank == first
            sc.append(m)
            ea.append(jnp.sum(jnp.where(sel, c1, 0.0), axis=0, keepdims=True))
            eb.append(jnp.sum(jnp.where(sel, c2, 0.0), axis=0, keepdims=True))
            cand = jnp.where(sel, -jnp.inf, cand)
        scm = jnp.concatenate(sc, axis=0)
        ex = jnp.exp(scm - sc[0])
        gates.append(ex / jnp.sum(ex, axis=0, keepdims=True))
        e1s.append(jnp.concatenate(ea, axis=0))
        e2s.append(jnp.concatenate(eb, axis=0))
    gate_ref[...] = jnp.concatenate(gates, axis=0).T
    e1_ref[...] = jnp.concatenate(e1s, axis=0).T
    e2_ref[...] = jnp.concatenate(e2s, axis=0).T


def peer_route(q, k1, k2):
    t = q.shape[0]
    tt = 128
    hk = PEER_HEADS * PEER_TOPK
    out = jax.ShapeDtypeStruct((t, hk), F32)
    ospec = pl.BlockSpec((tt, hk), lambda i: (i, 0))
    kspec = pl.BlockSpec((PEER_KEYS, PEER_HALF), lambda i: (0, 0))
    return pl.pallas_call(
        _peer_route_kernel,
        grid=(t // tt,),
        in_specs=[pl.BlockSpec((tt, q.shape[1]), lambda i: (i, 0)), kspec, kspec],
        out_specs=[ospec, ospec, ospec],
        out_shape=[out, out, out],
        compiler_params=_cparams("parallel"),
        name="peer_route",
    )(q, k1, k2)


def _peer_w_kernel(gate_ref, e1_ref, e2_ref, w_ref):
    tb = gate_ref.shape[0]
    sub = lax.broadcasted_iota(jnp.int32, (PEER_KEYS, gate_ref.shape[1]), 0).astype(F32)

    def body(t, carry):
        g = gate_ref[pl.ds(t, 1), :]
        a = e1_ref[pl.ds(t, 1), :]
        b = e2_ref[pl.ds(t, 1), :]
        lhs = jnp.where(a == sub, g, 0.0).astype(BF16)
        rhs = jnp.where(b == sub, 1.0, 0.0).astype(BF16)
        w_ref[t] = _dot_nt(lhs, rhs)
        return carry

    lax.fori_loop(0, tb, body, 0, unroll=8)


def peer_dense_weights(gate, e1, e2):
    t, hk = gate.shape
    tb = 128
    spec = pl.BlockSpec((tb, hk), lambda i: (i, 0))
    return pl.pallas_call(
        _peer_w_kernel,
        grid=(t // tb,),
        in_specs=[spec, spec, spec],
        out_specs=pl.BlockSpec((tb, PEER_KEYS, PEER_KEYS), lambda i: (i, 0, 0)),
        out_shape=jax.ShapeDtypeStruct((t, PEER_KEYS, PEER_KEYS), F32),
        compiler_params=_cparams("parallel"),
        name="peer_dense_weights",
    )(gate, e1, e2)


PEER_KEY1_PER_STEP = 8


def _peer_ffn_kernel(x_ref, w_ref, u_ref, v_ref, r_ref, o_ref):
    e = pl.program_id(1)

    @pl.when(e == 0)
    def _():
        o_ref[...] = r_ref[...]

    a = _dot_nt(x_ref[...], u_ref[...])
    act = 0.5 * a * (1.0 + lax.erf(a * (2.0 ** -0.5)))
    hmat = jnp.concatenate(
        [(act[:, i * PEER_KEYS:(i + 1) * PEER_KEYS] * w_ref[:, i, :]).astype(BF16)
         for i in range(PEER_KEY1_PER_STEP)], axis=1)
    o_ref[...] += jnp.dot(hmat, v_ref[...], preferred_element_type=F32)


def peer_ffn_dense(xn, w, u_tab, v_tab, resid):
    t, d = xn.shape
    n_exp = u_tab.shape[0]
    tb = _pick_tile(t, (528, 512, 256, 128))
    eb = PEER_KEY1_PER_STEP * PEER_KEYS
    once = pl.Buffered(1)
    return pl.pallas_call(
        _peer_ffn_kernel,
        grid=(t // tb, n_exp // eb),
        in_specs=[pl.BlockSpec((tb, d), lambda i, e: (i, 0), pipeline_mode=once),
                  pl.BlockSpec((tb, PEER_KEY1_PER_STEP, PEER_KEYS), lambda i, e: (i, e, 0)),
                  pl.BlockSpec((eb, d), lambda i, e: (e, 0)),
                  pl.BlockSpec((eb, d), lambda i, e: (e, 0)),
                  pl.BlockSpec((tb, d), lambda i, e: (i, 0), pipeline_mode=once)],
        out_specs=pl.BlockSpec((tb, d), lambda i, e: (i, 0)),
        out_shape=jax.ShapeDtypeStruct((t, d), F32),
        compiler_params=_cparams("parallel", "arbitrary"),
        name="peer_ffn_dense",
    )(xn, w, u_tab, v_tab, resid)


def peer_layer(x, g_ffn, wq, k1, k2, u_tab, v_tab):
    xn = rmsnorm_rows(x, g_ffn)
    q = matmul(xn, wq, name="peer_query")
    gate, e1, e2 = peer_route(q, k1, k2)
    w = peer_dense_weights(gate, e1, e2)
    return peer_ffn_dense(xn, w, u_tab, v_tab, x)


def _group_norm_kernel(x_ref, bd_ref, g_ref, *out_refs, inv_group, scales):
    x = x_ref[...]
    sq = x * x
    hi = sq.astype(BF16)
    lo = (sq - hi.astype(F32)).astype(BF16)
    bd = bd_ref[...]
    ss = jnp.dot(hi, bd, preferred_element_type=F32) + jnp.dot(lo, bd, preferred_element_type=F32)
    y = x * lax.rsqrt(ss * inv_group + NORM_EPS) * g_ref[...]
    for o_ref, sc in zip(out_refs, scales):
        o_ref[...] = (y if sc == 1.0 else y * sc).astype(o_ref.dtype)


def group_norm(src, col0, width, group, gain, outs):
    t = src.shape[0]
    assert col0 % width == 0 and width % group == 0
    tm = _pick_tile(t, (528, 512, 256, 128))
    lane = jnp.arange(width, dtype=jnp.int32) // group
    bd = (lane[:, None] == lane[None, :]).astype(BF16)
    g = jnp.tile(gain.astype(F32), width // group).reshape(1, width)
    ospec = pl.BlockSpec((tm, width), lambda i: (i, 0))
    res = pl.pallas_call(
        functools.partial(_group_norm_kernel, inv_group=1.0 / group, scales=tuple(s for _, s in outs)),
        grid=(t // tm,),
        in_specs=[pl.BlockSpec((tm, width), lambda i: (i, col0 // width)),
                  pl.BlockSpec((width, width), lambda i: (0, 0)),
                  pl.BlockSpec((1, width), lambda i: (0, 0))],
        out_specs=[ospec] * len(outs),
        out_shape=[jax.ShapeDtypeStruct((t, width), dt) for dt, _ in outs],
        compiler_params=_cparams("parallel"),
        name="group_norm",
    )(src, bd, g)
    return res


CONV_HALO = 8


def _conv_kernel(x_ref, halo_ref, w_ref, b_ref, o_ref, sc, *, first_tile_has_no_history):
    tm = x_ref.shape[0]
    halo = halo_ref[...]
    if first_tile_has_no_history:
        halo = jnp.where(pl.program_id(0) == 0, 0.0, halo)
    sc[0:CONV_HALO, :] = halo
    sc[CONV_HALO:, :] = x_ref[...]
    acc = b_ref[...]
    for j in range(SSM_CONV):
        acc = acc + w_ref[j:j + 1, :] * sc[pl.ds(CONV_HALO - (SSM_CONV - 1 - j), tm), :]
    o_ref[...] = jax.nn.silu(acc)


def causal_conv_silu(src, col0, width, row0, rows, tm, halo_src, conv_w, conv_b):
    tc = 512
    assert col0 % tc == 0 and width % tc == 0 and row0 % tm == 0 and rows % tm == 0 and tm % CONV_HALO == 0
    cb, rb, hb = col0 // tc, row0 // tm, tm // CONV_HALO
    if halo_src is None:
        halo_arr = src
        halo_spec = pl.BlockSpec((CONV_HALO, tc), lambda i, j: (jnp.maximum((rb + i) * hb - 1, 0), cb + j))
    else:
        halo_arr = halo_src
        halo_spec = pl.BlockSpec((CONV_HALO, tc), lambda i, j: (i, j))
    return pl.pallas_call(
        functools.partial(_conv_kernel, first_tile_has_no_history=halo_src is None),
        grid=(rows // tm, width // tc),
        in_specs=[pl.BlockSpec((tm, tc), lambda i, j: (rb + i, cb + j)),
                  halo_spec,
                  pl.BlockSpec((SSM_CONV, tc), lambda i, j: (0, j)),
                  pl.BlockSpec((1, tc), lambda i, j: (0, j))],
        out_specs=pl.BlockSpec((tm, tc), lambda i, j: (i, j)),
        out_shape=jax.ShapeDtypeStruct((rows, width), F32),
        scratch_shapes=[pltpu.VMEM((tm + CONV_HALO, tc), F32)],
        compiler_params=_cparams("parallel", "parallel"),
        name="causal_conv_silu",
    )(src, halo_arr, conv_w.astype(F32), conv_b.reshape(1, width).astype(F32))


SSM_INNER = SSM_HEADS * SSM_HEAD_DIM
SSM_GN = SSM_GROUPS * SSM_STATE
HEADS_PER_GROUP = SSM_HEADS // SSM_GROUPS
GROUP_W = HEADS_PER_GROUP * SSM_HEAD_DIM


def _ssd_kernel(xs_ref, b_ref, c_ref, z_ref, dt_ref, dtt_ref, dtb_row_ref, dtb_col_ref, a_row_ref, a_col_ref,
                e_ref, dsk_ref, g_ref, h0_ref, y_ref, hout_ref, h_sc, y_sc, *pad_scs, valid):
    q = SSM_CHUNK
    hi = lax.Precision.HIGHEST
    c = pl.program_id(1)

    @pl.when(c == 0)
    def _():
        h_sc[...] = h0_ref[...]

    def rows(ref, sc):
        if valid == q:
            return ref[...]
        sc[...] = jnp.zeros(sc.shape, F32)
        sc[0:valid, :] = ref[...]
        return sc[...]

    if valid == q:
        pad_scs = (None,) * 5
    xs = rows(xs_ref, pad_scs[0])
    bm = rows(b_ref, pad_scs[1])
    cm = rows(c_ref, pad_scs[2])
    z = rows(z_ref, pad_scs[3])
    dt_raw = rows(dt_ref, pad_scs[4])

    row_i = lax.broadcasted_iota(jnp.int32, (q, q), 0)
    col_i = lax.broadcasted_iota(jnp.int32, (q, q), 1)
    causal = row_i >= col_i
    dt = jax.nn.softplus(dt_raw + dtb_row_ref[...])
    dtt = jax.nn.softplus(dtt_ref[...] + dtb_col_ref[...])
    if valid < q:
        dt = jnp.where(row_i < valid, dt, 0.0)
        dtt = jnp.where(lax.broadcasted_iota(jnp.int32, dtt.shape, 1) < valid, dtt, 0.0)
    acs = jnp.dot(causal.astype(F32), dt * a_row_ref[...], precision=hi, preferred_element_type=F32)
    acst = jnp.dot(dtt * a_col_ref[...], (row_i <= col_i).astype(F32), precision=hi,
                   preferred_element_type=F32)
    expand = e_ref[...]
    dt_e = jnp.dot(dt, expand, precision=hi, preferred_element_type=F32)
    dec_e = jnp.dot(jnp.exp(acs[q - 1:q, :] - acs), expand, precision=hi, preferred_element_type=F32)
    eacs_e = jnp.dot(jnp.exp(acs), expand, precision=hi, preferred_element_type=F32)
    xdt = xs * dt_e
    xdd = xdt * dec_e
    xdt_b = xdt.astype(BF16)
    lane = lax.broadcasted_iota(jnp.int32, (q, 2 * SSM_HEAD_DIM), 1)

    for g in range(SSM_GROUPS):
        bg = bm[:, g * SSM_STATE:(g + 1) * SSM_STATE].astype(BF16)
        cg = cm[:, g * SSM_STATE:(g + 1) * SSM_STATE].astype(BF16)
        cb = _dot_nt(cg, bg)
        for pair in range(HEADS_PER_GROUP // 2):
            h_a = g * HEADS_PER_GROUP + 2 * pair
            slab = slice(h_a * SSM_HEAD_DIM, (h_a + 2) * SSM_HEAD_DIM)
            xpair = xdt_b[:, slab]
            ypair = jnp.zeros((q, 2 * SSM_HEAD_DIM), F32)
            for which in range(2):
                h = h_a + which
                seg = acs[:, h:h + 1] - acst[h:h + 1, :]
                lmat = jnp.exp(jnp.where(causal, seg, -jnp.inf))
                mine = (lane >= which * SSM_HEAD_DIM) & (lane < (which + 1) * SSM_HEAD_DIM)
                ypair = ypair + jnp.dot((cb * lmat).astype(BF16), jnp.where(mine, xpair, jnp.zeros_like(xpair)),
                                        preferred_element_type=F32)
            y_sc[:, slab] = ypair
        gs = slice(g * GROUP_W, (g + 1) * GROUP_W)
        y_sc[:, gs] += _dot_nt(cg, h_sc[gs, :].astype(BF16)) * eacs_e[:, gs]
        st = jnp.dot(xdd[:, gs].T.astype(BF16), bg, preferred_element_type=F32)
        for hh in range(HEADS_PER_GROUP):
            h = g * HEADS_PER_GROUP + hh
            hs = slice(h * SSM_HEAD_DIM, (h + 1) * SSM_HEAD_DIM)
            dec = jnp.exp(acst[h:h + 1, q - 1:q])
            h_sc[hs, :] = h_sc[hs, :] * dec + st[hh * SSM_HEAD_DIM:(hh + 1) * SSM_HEAD_DIM, :]

    y = y_sc[...] + dsk_ref[...] * xs
    y = y * jax.nn.silu(z)
    gain = g_ref[...]
    for g in range(SSM_GROUPS):
        gs = slice(g * GROUP_W, (g + 1) * GROUP_W)
        yg = y[:, gs]
        yn = yg * lax.rsqrt(jnp.mean(yg * yg, axis=-1, keepdims=True) + NORM_EPS) * gain[:, gs]
        y_ref[:, gs] = yn[0:valid].astype(y_ref.dtype)

    @pl.when(c == pl.num_programs(1) - 1)
    def _():
        hout_ref[...] = h_sc[...]


def ssd_branch(xc, u, z_col0, u_row0, dt_pad, dt_t, h0, n_batch, n_chunks, valid, dt_bias, a_log, d_skip, g_ssm,
               out_dtype):
    q = SSM_CHUNK
    rows = xc.shape[0]
    assert rows == n_batch * n_chunks * valid and u_row0 % valid == 0 and z_col0 % SSM_INNER == 0
    rb = u_row0 // valid
    step = lambda b, c: b * n_chunks + c
    pad128 = lambda v: jnp.pad(v.astype(F32), (0, 128 - SSM_HEADS))
    a = -jnp.exp(a_log.astype(F32))
    head_lane = jnp.arange(SSM_INNER, dtype=jnp.int32) // SSM_HEAD_DIM
    expand = (jnp.arange(128, dtype=jnp.int32)[:, None] == head_lane[None, :]).astype(F32)
    const = lambda shape: pl.BlockSpec(shape, lambda b, c: (0,) * len(shape))
    pad_scs = [] if valid == q else [pltpu.VMEM((q, w), F32) for w in (SSM_INNER, SSM_GN, SSM_GN, SSM_INNER, 128)]
    y, h_fin = pl.pallas_call(
        functools.partial(_ssd_kernel, valid=valid),
        grid=(n_batch, n_chunks),
        in_specs=[
            pl.BlockSpec((valid, SSM_INNER), lambda b, c: (step(b, c), 0)),
            pl.BlockSpec((valid, SSM_GN), lambda b, c: (step(b, c), SSM_INNER // SSM_GN)),
            pl.BlockSpec((valid, SSM_GN), lambda b, c: (step(b, c), SSM_INNER // SSM_GN + 1)),
            pl.BlockSpec((valid, SSM_INNER), lambda b, c: (rb + step(b, c), z_col0 // SSM_INNER)),
            pl.BlockSpec((valid, 128), lambda b, c: (rb + step(b, c), 0)),
            pl.BlockSpec((None, SSM_HEADS, q), lambda b, c: (step(b, c), 0, 0)),
            const((1, 128)), const((SSM_HEADS, q)), const((1, 128)), const((SSM_HEADS, q)),
            const((128, SSM_INNER)), const((1, SSM_INNER)), const((1, SSM_INNER)),
            pl.BlockSpec((None, SSM_INNER, SSM_STATE), lambda b, c: (b, 0, 0)),
        ],
        out_specs=[pl.BlockSpec((valid, SSM_INNER), lambda b, c: (step(b, c), 0)),
                   pl.BlockSpec((None, SSM_INNER, SSM_STATE), lambda b, c: (b, 0, 0))],
        out_shape=[jax.ShapeDtypeStruct((rows, SSM_INNER), out_dtype),
                   jax.ShapeDtypeStruct((n_batch, SSM_INNER, SSM_STATE), F32)],
        scratch_shapes=[pltpu.VMEM((SSM_INNER, SSM_STATE), F32), pltpu.VMEM((q, SSM_INNER), F32)] + pad_scs,
        compiler_params=_cparams("parallel", "arbitrary"),
        name="ssd_branch",
    )(xc, xc, xc, u, dt_pad, dt_t,
      pad128(dt_bias).reshape(1, 128), jnp.broadcast_to(dt_bias.astype(F32)[:, None], (SSM_HEADS, q)),
      pad128(a).reshape(1, 128), jnp.broadcast_to(a[:, None], (SSM_HEADS, q)),
      expand, jnp.repeat(d_skip.astype(F32), SSM_HEAD_DIM).reshape(1, SSM_INNER),
      g_ssm.astype(F32).reshape(1, SSM_INNER), h0)
    return y, h_fin


def _mem_attn_kernel(q_ref, k_ref, v_ref, o_ref):
    for h in range(MEM_HEADS):
        hs = slice(h * MEM_HEAD_DIM, (h + 1) * MEM_HEAD_DIM)
        s = _dot_nt(q_ref[:, hs].astype(BF16), k_ref[:, hs].astype(BF16))
        p = jnp.exp(s - jnp.max(s, axis=-1, keepdims=True))
        o = jnp.dot(p.astype(BF16), v_ref[:, hs].astype(BF16), preferred_element_type=F32)
        o_ref[:, hs] = (o / jnp.sum(p, axis=-1, keepdims=True)).astype(o_ref.dtype)


def memory_attention(qn, q_row0, rows, tq, mk, mv, v_col_blk, tiles_per_batch, out_dtype):
    n_mem = mk.shape[0] * tq * tiles_per_batch // rows
    rb = q_row0 // tq
    return pl.pallas_call(
        _mem_attn_kernel,
        grid=(rows // tq,),
        in_specs=[pl.BlockSpec((tq, MEM_W), lambda i: (rb + i, 0)),
                  pl.BlockSpec((n_mem, MEM_W), lambda i: (i // tiles_per_batch, 0)),
                  pl.BlockSpec((n_mem, MEM_W), lambda i: (i // tiles_per_batch, v_col_blk))],
        out_specs=pl.BlockSpec((tq, MEM_W), lambda i: (i, 0)),
        out_shape=jax.ShapeDtypeStruct((rows, MEM_W), out_dtype),
        compiler_params=_cparams("parallel"),
        name="memory_attention",
    )(qn, mk, mv)


SAMPLE_PAGES_PER_STEP = 8
SAMPLE_ROWS = DA_HEADS * 2 * 8


def _sample_attn_kernel(pt_ref, lam_ref, q_ref, *refs, pages_per_step, post_scale):
    n = pages_per_step
    kt_refs, v_refs = refs[:n], refs[n:2 * n]
    knew_ref, vnew_ref, blast_ref, bnew_ref, rexp_ref, hmask_ref, g_ref, o_ref, m_sc, l_sc, acc_sc = refs[2 * n:]
    del pt_ref
    step = pl.program_id(1)
    n_new = o_ref.shape[0]
    rows_per_head = 2 * n_new

    @pl.when(step == 0)
    def _():
        m_sc[...] = jnp.full(m_sc.shape, NEG_BIG, F32)
        l_sc[...] = jnp.zeros(l_sc.shape, F32)
        acc_sc[...] = jnp.zeros(acc_sc.shape, F32)

    q = q_ref[...]

    def update(pages):
        logits = []
        for kt_ref, _, bias in pages:
            s = jnp.dot(q, kt_ref[...].astype(BF16), preferred_element_type=F32)
            logits.append(s if bias is None else s + bias)
        s_max = logits[0]
        for s in logits[1:]:
            s_max = jnp.maximum(s_max, s)
        m_prev = m_sc[...]
        m_new = jnp.maximum(m_prev, jnp.max(s_max, axis=-1, keepdims=True))
        alpha = jnp.exp2(m_prev - m_new)
        m_sc[...] = m_new
        l_new = alpha * l_sc[...]
        acc = alpha * acc_sc[...]
        rexp = rexp_ref[...]
        own_head = hmask_ref[...]
        for (_, v_ref, _), s in zip(pages, logits):
            p = jnp.exp2(s - m_new)
            l_new = l_new + p
            p_rep = jnp.dot(p.astype(BF16), rexp, preferred_element_type=F32)
            acc = acc + jnp.dot(p_rep.astype(BF16) * own_head, v_ref[...].astype(BF16),
                                preferred_element_type=F32)
        l_sc[...] = l_new
        acc_sc[...] = acc

    last = pl.num_programs(1) - 1
    past = [(kt_refs[i], v_refs[i], None) for i in range(n)]

    @pl.when(step < last)
    def _():
        update(past)

    @pl.when(step == last)
    def _():
        update(past[:-1] + [(kt_refs[n - 1], v_refs[n - 1], blast_ref[...]), (knew_ref, vnew_ref, bnew_ref[...])])
        attn = acc_sc[...] / jnp.sum(l_sc[...], axis=-1, keepdims=True)
        for h in range(DA_HEADS):
            r0 = h * rows_per_head
            o = attn[r0:r0 + n_new, :] - lam_ref[0] * attn[r0 + n_new:r0 + 2 * n_new, :]
            y = o * lax.rsqrt(jnp.mean(o * o, axis=-1, keepdims=True) + NORM_EPS)
            o_ref[:, h * DA_DV:(h + 1) * DA_DV] = (y * g_ref[...]) * post_scale


def sample_bias_tiles(table, n_new, page):
    def rows(off):
        b = _toeplitz_bias(table, off, page)[:, :n_new, :]
        return jnp.broadcast_to(b[:, None], (DA_HEADS, 2, n_new, page)).reshape(SAMPLE_ROWS, page)
    return rows(page), rows(0)


def sample_diff_attention(layer, qn_s, kn_s, v_s, lam, k_pool_t, v_pool, page_table, b_last, b_new, g_sub,
                          post_scale):
    bsz, n_new, qw = qn_s.shape
    page = v_pool.shape[2]
    n_pages = page_table.shape[1]
    n = SAMPLE_PAGES_PER_STEP
    assert n_pages % n == 0 and SAMPLE_ROWS == DA_HEADS * 2 * n_new
    qg = qn_s.reshape(bsz, n_new, DA_HEADS * 2, DA_DK)
    eye = jnp.eye(DA_HEADS * 2, dtype=F32)
    q_rows = (qg.transpose(0, 2, 1, 3)[:, :, :, None, :] * eye[None, :, None, :, None]).reshape(
        bsz, SAMPLE_ROWS, qw).astype(BF16)
    knew_t = jnp.pad(kn_s.transpose(0, 2, 1), ((0, 0), (0, 0), (0, page - n_new)))
    vnew = jnp.pad(v_s, ((0, 0), (0, page - n_new), (0, 0), (0, 0))).reshape(bsz, page * DA_HEADS, DA_DV)
    v_rows = v_pool.reshape(v_pool.shape[0], v_pool.shape[1], page * DA_HEADS, DA_DV)
    col = jnp.arange(page * DA_HEADS, dtype=jnp.int32)
    rexp = (col[None, :] // DA_HEADS == jnp.arange(page, dtype=jnp.int32)[:, None]).astype(BF16)
    hmask = (col[None, :] % DA_HEADS
             == jnp.arange(SAMPLE_ROWS, dtype=jnp.int32)[:, None] // (2 * n_new)).astype(BF16)
    kspec = lambda i: pl.BlockSpec((None, None, qw, page), lambda b, s, pt: (layer, pt[b, s * n + i], 0, 0))
    vspec = lambda i: pl.BlockSpec((None, None, page * DA_HEADS, DA_DV),
                                   lambda b, s, pt: (layer, pt[b, s * n + i], 0, 0))
    const2 = lambda shape: pl.BlockSpec(shape, lambda b, s, pt: (0, 0))
    grid_spec = pltpu.PrefetchScalarGridSpec(
        num_scalar_prefetch=1,
        grid=(bsz, n_pages // n),
        in_specs=[pl.BlockSpec(memory_space=pltpu.SMEM),
                  pl.BlockSpec((None, SAMPLE_ROWS, qw), lambda b, s, pt: (b, 0, 0))]
                 + [kspec(i) for i in range(n)] + [vspec(i) for i in range(n)]
                 + [pl.BlockSpec((None, qw, page), lambda b, s, pt: (b, 0, 0)),
                    pl.BlockSpec((None, page * DA_HEADS, DA_DV), lambda b, s, pt: (b, 0, 0)),
                    const2((SAMPLE_ROWS, page)), const2((SAMPLE_ROWS, page)),
                    const2((page, page * DA_HEADS)), const2((SAMPLE_ROWS, page * DA_HEADS)), const2((1, DA_DV))],
        out_specs=pl.BlockSpec((n_new, DA_VW), lambda b, s, pt: (b, 0)),
        scratch_shapes=[pltpu.VMEM((SAMPLE_ROWS, page), F32), pltpu.VMEM((SAMPLE_ROWS, page), F32),
                        pltpu.VMEM((SAMPLE_ROWS, DA_DV), F32)],
    )
    return pl.pallas_call(
        functools.partial(_sample_attn_kernel, pages_per_step=n, post_scale=post_scale),
        grid_spec=grid_spec,
        out_shape=jax.ShapeDtypeStruct((bsz * n_new, DA_VW), F32),
        compiler_params=_cparams("parallel", "arbitrary"),
        name="sample_diff_attention",
    )(page_table, lam.reshape(1).astype(F32), q_rows, *([k_pool_t] * n), *([v_rows] * n), knew_t, vnew,
      b_last, b_new, rexp, hmask, g_sub.reshape(1, DA_DV).astype(F32))


def _rmsnorm(x, g):
    xf = x.astype(F32)
    y = xf * lax.rsqrt(jnp.mean(xf * xf, axis=-1, keepdims=True) + NORM_EPS)
    return (y * g.astype(F32)).astype(x.dtype)


def _t5_bias(dist, table):
    return jnp.moveaxis(table.astype(F32)[_t5_bucket(dist)], -1, 0)


def _diff_attn_sample(q, k, v, lam, k_pool, v_pool, layer, page_table, bias_table):
    b, t = q.shape[:2]
    past = page_table.shape[1] * k_pool.shape[2]
    k_past = k_pool[layer, page_table].reshape(b, past, DA_HEADS, 2, DA_DK)
    v_past = v_pool[layer, page_table].reshape(b, past, DA_HEADS, DA_DV)
    tpos = jnp.arange(t, dtype=jnp.int32)
    dist_past = (past + tpos)[:, None] - jnp.arange(past, dtype=jnp.int32)[None, :]
    dist_new = tpos[:, None] - tpos[None, :]
    lp = jnp.einsum('bqhmd,bkhmd->bhmqk', q, k_past, preferred_element_type=F32) * DA_SCALE
    lp = lp + _t5_bias(dist_past, bias_table)[None, :, None]
    ln = jnp.einsum('bqhmd,bkhmd->bhmqk', q, k, preferred_element_type=F32) * DA_SCALE
    ln = jnp.where(dist_new >= 0, ln + _t5_bias(dist_new, bias_table)[None, :, None], -jnp.inf)
    prob = jax.nn.softmax(jnp.concatenate([lp, ln], axis=-1), axis=-1)
    attn = (prob[:, :, 0] - lam * prob[:, :, 1]).astype(v.dtype)
    return (jnp.einsum('bhqk,bkhd->bqhd', attn[..., :past], v_past)
            + jnp.einsum('bhqk,bkhd->bqhd', attn[..., past:], v))


def _memory_attend(q, mk, mv):
    logits = jnp.einsum('bqhd,bmhd->bhqm', q, mk, preferred_element_type=F32) * MEM_SCALE
    prob = jax.nn.softmax(logits, axis=-1)
    return jnp.einsum('bhqm,bmhd->bqhd', prob.astype(mv.dtype), mv)


def _ssd_scan(xh, dt, a, bm, cm, h0):
    b, L = xh.shape[:2]
    q = min(SSM_CHUNK, L)
    assert L % q == 0
    c = L // q
    hpg = SSM_HEADS // SSM_GROUPS
    xdt = (xh * dt[..., None]).reshape(b, c, q, SSM_GROUPS, hpg, SSM_HEAD_DIM)
    acs = jnp.cumsum((dt * a).reshape(b, c, q, SSM_GROUPS, hpg), axis=2)
    bc = bm.reshape(b, c, q, SSM_GROUPS, SSM_STATE)
    cc = cm.reshape(b, c, q, SSM_GROUPS, SSM_STATE)
    causal = jnp.tril(jnp.ones((q, q), dtype=bool))[:, :, None, None]
    seg = acs[:, :, :, None] - acs[:, :, None, :]
    lmat = jnp.exp(jnp.where(causal, seg, -jnp.inf))
    cb = jnp.einsum('bclgn,bcsgn->bclsg', cc, bc)
    y_diag = jnp.einsum('bclsgh,bcsghp->bclghp', cb[..., None] * lmat, xdt)
    decay_s = jnp.exp(acs[:, :, -1:] - acs)
    states = jnp.einsum('bcsgn,bcsghp->bcghpn', bc, decay_s[..., None] * xdt)
    chunk_decay = jnp.exp(acs[:, :, -1])

    def step(h, inp):
        dec, st = inp
        return h * dec[..., None, None] + st, h

    h_last, h_in = lax.scan(step, h0.reshape(b, SSM_GROUPS, hpg, SSM_HEAD_DIM, SSM_STATE),
                            (jnp.moveaxis(chunk_decay, 1, 0), jnp.moveaxis(states, 1, 0)))
    h_in = jnp.moveaxis(h_in, 0, 1)
    y_off = jnp.einsum('bclgn,bcghpn->bclghp', cc, h_in) * jnp.exp(acs)[..., None]
    y = (y_diag + y_off).reshape(b, c * q, SSM_HEADS, SSM_HEAD_DIM)
    return y, h_last.reshape(b, SSM_HEADS, SSM_HEAD_DIM, SSM_STATE)


def _mamba_branch(z, xbc, dt_raw, conv_prev, h0, conv_w, conv_b, dt_bias, a_log, d_skip, g_ssm):
    b, L = xbc.shape[:2]
    inner = SSM_HEADS * SSM_HEAD_DIM
    gn = SSM_GROUPS * SSM_STATE
    xpad = jnp.concatenate([conv_prev.astype(xbc.dtype), xbc], axis=1)
    conv = conv_b + conv_w[0] * xpad[:, 0:L]
    for j in range(1, SSM_CONV):
        conv = conv + conv_w[j] * xpad[:, j:j + L]
    conv_new = xpad[:, L:]
    xc = jax.nn.silu(conv)
    xs, bm, cm = xc[..., :inner], xc[..., inner:inner + gn], xc[..., inner + gn:]
    xh = xs.reshape(b, L, SSM_HEADS, SSM_HEAD_DIM).astype(F32)
    dt = jax.nn.softplus(dt_raw.astype(F32) + dt_bias.astype(F32))
    a = -jnp.exp(a_log.astype(F32))
    y, h_fin = _ssd_scan(xh, dt, a, bm.reshape(b, L, SSM_GROUPS, SSM_STATE).astype(F32),
                         cm.reshape(b, L, SSM_GROUPS, SSM_STATE).astype(F32), h0.astype(F32))
    y = y + d_skip.astype(F32)[:, None] * xh
    y = y.reshape(b, L, inner).astype(z.dtype) * jax.nn.silu(z)
    y = _rmsnorm(y.reshape(b, L, SSM_GROUPS, inner // SSM_GROUPS), g_ssm.reshape(SSM_GROUPS, inner // SSM_GROUPS))
    return y.reshape(b, L, inner), h_fin, conv_new


FA_TILE = 512


def kernel(x_prompt, x_sample, cache_attn_k, cache_attn_v, cache_mem_k, cache_mem_v, state_ssm, state_conv, page_table, mem_prompt, rel_bias, g_mix, w_in, g_q, g_k, lam_q1, lam_k1, lam_q2, lam_k2, g_sub, conv_w, conv_b, dt_bias, a_log, d_skip, g_ssm, g_mem, w_mem_kv, g_mq, g_mk, w_br_attn, w_br_ssm, w_br_mem, w_out, g_ffn, peer_wq, peer_k1, peer_k2, peer_u, peer_v):
    depth = w_in.shape[0]
    bp, sp, d = x_prompt.shape
    bs, ts, _ = x_sample.shape
    assert bp == 1 and sp % SSM_CHUNK == 0 and sp % FA_TILE == 0 and ts >= SSM_CONV - 1
    n_p = bp * sp
    n_s = bs * ts
    conv_dim = SSM_INNER + 2 * SSM_GN
    n_mem = mem_prompt.shape[1]
    n_pool, page = cache_attn_v.shape[1], cache_attn_v.shape[2]
    s_z = 2 * DA_QW + DA_VW
    s_dt = s_z + SSM_INNER + conv_dim
    s_mq = s_dt + SSM_HEADS
    c_z, c_xbc = 0, SSM_INNER
    c_q = c_xbc + conv_dim
    c_k = c_q + DA_QW
    c_v = c_k + DA_QW
    c_mq = c_v + DA_VW
    c_gate = c_mq + MEM_W

    x = jnp.concatenate([x_prompt.reshape(n_p, d), x_sample.reshape(n_s, d)], axis=0)
    bias_tiles = prompt_bias_tiles(rel_bias, FA_TILE)
    b_last, b_new = sample_bias_tiles(rel_bias, ts, page)
    k_pool_t = jnp.transpose(cache_attn_k, (0, 1, 3, 4, 5, 2)).reshape(depth, n_pool, DA_QW, page)
    zero_h = jnp.zeros((bp, SSM_INNER, SSM_STATE), F32)

    outs = {k: [] for k in ("pk", "pv", "pmk", "pmv", "ph", "pc", "sk", "sv", "sh", "sc")}
    for l in range(depth):
        lam_init = 0.8 - 0.6 * math.exp(-0.3 * l)
        lam = (jnp.exp(jnp.sum(lam_q1[l].astype(F32) * lam_k1[l].astype(F32)))
               - jnp.exp(jnp.sum(lam_q2[l].astype(F32) * lam_k2[l].astype(F32))) + lam_init)
        wl = w_in[l]
        w_main = jnp.concatenate([wl[:, s_z:s_dt], wl[:, :s_z], wl[:, s_mq:]], axis=1).astype(BF16)
        w_dt = jnp.pad(wl[:, s_dt:s_mq], ((0, 0), (0, 128 - SSM_HEADS))).astype(BF16)

        h = rmsnorm_rows(x, g_mix[l])
        u = matmul(h, w_main, name="in_proj")
        dt_pad = matmul(h, w_dt, name="dt_proj")
        (qn,) = group_norm(u, c_q, DA_QW, DA_DK, g_q[l], [(BF16, DA_SCALE * LOG2E)])
        kn, kn_b = group_norm(u, c_k, DA_QW, DA_DK, g_k[l], [(F32, 1.0), (BF16, 1.0)])
        (mqn,) = group_norm(u, c_mq, MEM_W, MEM_HEAD_DIM, g_mq[l], [(F32, MEM_SCALE)])
        v_all = u[:, c_v:c_v + DA_VW]
        xbc_s = u[n_p:, c_xbc:c_xbc + conv_dim].reshape(bs, ts, conv_dim)

        a_p = prompt_diff_attention(qn, kn_b, u, c_v, n_p, lam, bias_tiles, g_sub[l], 1.0 - lam_init, FA_TILE)
        a_s = sample_diff_attention(l, qn[n_p:].astype(F32).reshape(bs, ts, DA_QW), kn[n_p:].reshape(bs, ts, DA_QW),
                                    v_all[n_p:].reshape(bs, ts, DA_HEADS, DA_DV), lam, k_pool_t, cache_attn_v,
                                    page_table, b_last, b_new, g_sub[l], 1.0 - lam_init)

        hm = rmsnorm_rows(mem_prompt.reshape(bp * n_mem, d), g_mem[l])
        mkv = matmul(hm, w_mem_kv[l].astype(BF16), name="mem_kv")
        (mk_p,) = group_norm(mkv, 0, MEM_W, MEM_HEAD_DIM, g_mk[l], [(F32, 1.0)])
        tq = 512
        m_p = memory_attention(mqn, 0, n_p, tq, mk_p, mkv, 1, sp // tq, BF16)
        m_s = memory_attention(mqn, n_p, n_s, ts, cache_mem_k[l].reshape(bs * n_mem, MEM_W),
                               cache_mem_v[l].reshape(bs * n_mem, MEM_W), 0, 1, F32)

        xc_p = causal_conv_silu(u, c_xbc, conv_dim, 0, n_p, 512, None, conv_w[l], conv_b[l])
        halo_s = jnp.pad(state_conv[l], ((0, 0), (CONV_HALO - (SSM_CONV - 1), 0), (0, 0))).reshape(
            bs * CONV_HALO, conv_dim)
        xc_s = causal_conv_silu(u, c_xbc, conv_dim, n_p, n_s, ts, halo_s, conv_w[l], conv_b[l])
        dt_raw = dt_pad[:, :SSM_HEADS]
        dtt_p = dt_raw[:n_p].reshape(n_p // SSM_CHUNK, SSM_CHUNK, SSM_HEADS).transpose(0, 2, 1)
        dtt_s = jnp.pad(dt_raw[n_p:].reshape(bs, ts, SSM_HEADS).transpose(0, 2, 1),
                        ((0, 0), (0, 0), (0, SSM_CHUNK - ts)))
        s_p, h_p = ssd_branch(xc_p, u, c_z, 0, dt_pad, dtt_p, zero_h, bp, sp // SSM_CHUNK, SSM_CHUNK,
                              dt_bias[l], a_log[l], d_skip[l], g_ssm[l], BF16)
        s_s, h_s = ssd_branch(xc_s, u, c_z, n_p, dt_pad, dtt_s, state_ssm[l].reshape(bs, SSM_INNER, SSM_STATE),
                              bs, 1, ts, dt_bias[l], a_log[l], d_skip[l], g_ssm[l], F32)

        a_out = jnp.concatenate([a_p, a_s.astype(BF16)], axis=0)
        s_out = jnp.concatenate([s_p, s_s.astype(BF16)], axis=0)
        m_out = jnp.concatenate([m_p, m_s.astype(BF16)], axis=0)
        merged = merge_branches(a_out, s_out, m_out, u, c_gate, w_br_attn[l].astype(BF16),
                                w_br_ssm[l].astype(BF16), w_br_mem[l].astype(BF16))
        x = matmul(merged, w_out[l].astype(BF16), residual=x, name="out_proj")
        x = peer_layer(x, g_ffn[l], peer_wq[l].astype(BF16), peer_k1[l].astype(BF16), peer_k2[l].astype(BF16),
                       peer_u[l].astype(BF16), peer_v[l].astype(BF16))

        outs["pk"].append(kn[:n_p].reshape(bp, sp, DA_HEADS, 2, DA_DK))
        outs["pv"].append(v_all[:n_p].reshape(bp, sp, DA_HEADS, DA_DV))
        outs["pmk"].append(mk_p.reshape(bp, n_mem, MEM_HEADS, MEM_HEAD_DIM))
        outs["pmv"].append(mkv[:, MEM_W:].reshape(bp, n_mem, MEM_HEADS, MEM_HEAD_DIM))
        outs["ph"].append(h_p.reshape(bp, SSM_HEADS, SSM_HEAD_DIM, SSM_STATE))
        outs["pc"].append(u[n_p - (SSM_CONV - 1):n_p, c_xbc:c_xbc + conv_dim].reshape(bp, SSM_CONV - 1, conv_dim))
        outs["sk"].append(kn[n_p:].reshape(bs, ts, DA_HEADS, 2, DA_DK))
        outs["sv"].append(v_all[n_p:].reshape(bs, ts, DA_HEADS, DA_DV))
        outs["sh"].append(h_s.reshape(bs, SSM_HEADS, SSM_HEAD_DIM, SSM_STATE))
        outs["sc"].append(xbc_s[:, ts - (SSM_CONV - 1):])

    y_prompt = x[:n_p].reshape(bp, sp, d)
    y_sample = x[n_p:].reshape(bs, ts, d)
    st = lambda name: jnp.stack(outs[name])
    return (y_prompt, y_sample, st("pk"), st("pv"), st("pmk"), st("pmv"), st("ph"), st("pc"),
            st("sk"), st("sv"), st("sh"), st("sc"))
```

```python
import functools
import math

import jax
import jax.numpy as jnp
from jax import lax
from jax.experimental import pallas as pl
from jax.experimental.pallas import tpu as pltpu

F32 = jnp.float32
BF16 = jnp.bfloat16

NORM_EPS = 1e-6
NEG_BIG = -1e30
V7X_VMEM_LIMIT_BYTES = 56 * 1024 * 1024

DA_HEADS, DA_DK, DA_DV = 8, 64, 128
DA_QW = DA_HEADS * 2 * DA_DK
DA_VW = DA_HEADS * DA_DV
DA_SCALE = DA_DK ** -0.5
N_BUCKETS, MAX_DISTANCE = 32, 128
SSM_HEADS, SSM_HEAD_DIM, SSM_GROUPS, SSM_STATE, SSM_CONV, SSM_CHUNK = 32, 64, 4, 128, 4, 128
MEM_HEADS, MEM_HEAD_DIM = 4, 256
MEM_W = MEM_HEADS * MEM_HEAD_DIM
MEM_SCALE = MEM_HEAD_DIM ** -0.5
PEER_HEADS, PEER_KEYS, PEER_TOPK, PEER_HALF = 8, 128, 16, 128


def _cparams(*sem):
    return pltpu.CompilerParams(dimension_semantics=sem, vmem_limit_bytes=V7X_VMEM_LIMIT_BYTES)


def _dot_nt(a, b):
    return lax.dot_general(a, b, (((1,), (1,)), ((), ())), preferred_element_type=F32)


def _pick_tile(n, candidates):
    for c in candidates:
        if n % c == 0:
            return c
    return n


def _rmsnorm_kernel(x_ref, g_ref, o_ref):
    x = x_ref[...]
    ms = jnp.mean(x * x, axis=-1, keepdims=True)
    o_ref[...] = (x * lax.rsqrt(ms + NORM_EPS) * g_ref[...]).astype(o_ref.dtype)


def rmsnorm_rows(x, g, out_dtype=BF16):
    t, d = x.shape
    tm = _pick_tile(t, (1056, 1024, 512, 256, 128))
    return pl.pallas_call(
        _rmsnorm_kernel,
        grid=(t // tm,),
        in_specs=[pl.BlockSpec((tm, d), lambda i: (i, 0)), pl.BlockSpec((1, d), lambda i: (0, 0))],
        out_specs=pl.BlockSpec((tm, d), lambda i: (i, 0)),
        out_shape=jax.ShapeDtypeStruct((t, d), out_dtype),
        compiler_params=_cparams("parallel"),
        name="rmsnorm_rows",
    )(x, g.reshape(1, d).astype(F32))


def _mm_kernel(a_ref, w_ref, o_ref):
    o_ref[...] = jnp.dot(a_ref[...], w_ref[...], preferred_element_type=F32).astype(o_ref.dtype)


def _mm_res_kernel(a_ref, w_ref, r_ref, o_ref):
    o_ref[...] = r_ref[...] + jnp.dot(a_ref[...], w_ref[...], preferred_element_type=F32)


def matmul(a, w, residual=None, out_dtype=F32, name="matmul"):
    t, k = a.shape
    n = w.shape[1]
    tm = _pick_tile(t, (528, 512, 256, 128))
    tn = _pick_tile(n, (1536, 1024, 512, 256, 128))
    in_specs = [pl.BlockSpec((tm, k), lambda i, j: (i, 0)), pl.BlockSpec((k, tn), lambda i, j: (0, j))]
    args = [a, w]
    body = _mm_kernel
    if residual is not None:
        in_specs.append(pl.BlockSpec((tm, tn), lambda i, j: (i, j)))
        args.append(residual)
        body = _mm_res_kernel
    return pl.pallas_call(
        body,
        grid=(t // tm, n // tn),
        in_specs=in_specs,
        out_specs=pl.BlockSpec((tm, tn), lambda i, j: (i, j)),
        out_shape=jax.ShapeDtypeStruct((t, n), out_dtype),
        compiler_params=_cparams("parallel", "parallel"),
        name=name,
    )(*args)


def _merge_kernel(a_ref, s_ref, m_ref, ga_ref, gs_ref, gm_ref, wa_ref, ws_ref, wm_ref, o_ref):
    acc = jax.nn.sigmoid(ga_ref[...]) * jnp.dot(a_ref[...], wa_ref[...], preferred_element_type=F32)
    acc += jax.nn.sigmoid(gs_ref[...]) * jnp.dot(s_ref[...], ws_ref[...], preferred_element_type=F32)
    acc += jax.nn.sigmoid(gm_ref[...]) * jnp.dot(m_ref[...], wm_ref[...], preferred_element_type=F32)
    o_ref[...] = acc.astype(o_ref.dtype)


def merge_branches(a, s, m, u, gate_col0, wa, ws, wm):
    t = a.shape[0]
    d = wa.shape[1]
    tm = _pick_tile(t, (528, 512, 256, 128))
    tn = 512
    gb = gate_col0 // tn
    nd = d // tn
    row = lambda width: pl.BlockSpec((tm, width), lambda i, j: (i, 0))
    gate = lambda which: pl.BlockSpec((tm, tn), lambda i, j: (i, gb + which * nd + j))
    wcol = lambda kdim: pl.BlockSpec((kdim, tn), lambda i, j: (0, j))
    return pl.pallas_call(
        _merge_kernel,
        grid=(t // tm, nd),
        in_specs=[row(a.shape[1]), row(s.shape[1]), row(m.shape[1]), gate(0), gate(1), gate(2),
                  wcol(wa.shape[0]), wcol(ws.shape[0]), wcol(wm.shape[0])],
        out_specs=pl.BlockSpec((tm, tn), lambda i, j: (i, j)),
        out_shape=jax.ShapeDtypeStruct((t, d), BF16),
        compiler_params=_cparams("parallel", "parallel"),
        name="merge_branches",
    )(a, s, m, u, u, u, wa, ws, wm)


def _t5_bucket(dist):
    n = jnp.maximum(dist, 0)
    exact = N_BUCKETS // 2
    nf = jnp.maximum(n, exact).astype(F32)
    large = exact + (jnp.log(nf / exact) / math.log(MAX_DISTANCE / exact) * (N_BUCKETS - exact)).astype(jnp.int32)
    return jnp.where(n < exact, n, jnp.minimum(large, N_BUCKETS - 1))


def _far_bias(table):
    return table.astype(F32)[N_BUCKETS - 1]


LOG2E = math.log2(math.e)


def _toeplitz_bias(table, off, t):
    x = jnp.arange(2 * t, dtype=jnp.int32)
    dist = jnp.where(x < t, off - x, off + 2 * t - x)
    far = _far_bias(table)
    v = (jnp.moveaxis(table.astype(F32)[_t5_bucket(dist)], -1, 0) - far[:, None]) * LOG2E
    v = jnp.where(dist[None, :] >= 0, v, NEG_BIG)
    h = v.shape[0]
    return jnp.tile(v, (1, t))[:, :t * (2 * t - 1)].reshape(h, t, 2 * t - 1)[:, :, :t]


def prompt_bias_tiles(table, tile):
    assert tile >= MAX_DISTANCE
    return jnp.stack([_toeplitz_bias(table, 0, tile), _toeplitz_bias(table, tile, tile)], axis=1)


FA_HEADS_PER_STEP = 2


def _fa_kernel(qt_ref, kt_ref, lam_ref, q_ref, k_ref, v_ref, b_ref, g_ref, o_ref, m_sc, l_sc, acc_sc, *,
               post_scale):
    s_id = pl.program_id(1)
    qi = qt_ref[s_id]
    ki = kt_ref[s_id]

    @pl.when(ki == 0)
    def _():
        m_sc[...] = jnp.full(m_sc.shape, NEG_BIG, F32)
        l_sc[...] = jnp.zeros(l_sc.shape, F32)
        acc_sc[...] = jnp.zeros(acc_sc.shape, F32)

    def update(bias_idx):
        n_lane_tiles = k_ref.shape[0] // 128
        lane = lax.broadcasted_iota(jnp.int32, (q_ref.shape[0], 2 * DA_DK), 1)
        for hh in range(FA_HEADS_PER_STEP):
            hs = slice(hh * 2 * DA_DK, (hh + 1) * 2 * DA_DK)
            q = q_ref[:, hs]
            k = k_ref[:, hs]
            v = v_ref[:, hh * DA_DV:(hh + 1) * DA_DV].astype(BF16)
            logits = []
            for m in range(2):
                in_map = (lane >= m * DA_DK) & (lane < (m + 1) * DA_DK)
                s = _dot_nt(jnp.where(in_map, q, jnp.zeros_like(q)), k)
                logits.append(s if bias_idx is None else s + b_ref[hh, bias_idx])
            for m in range(2):
                s = logits[m]
                slot = 2 * hh + m
                m_prev = m_sc[slot]
                m_new = jnp.maximum(m_prev, jnp.max(s, axis=-1, keepdims=True))
                alpha = jnp.exp2(m_prev - m_new)
                p = jnp.exp2(s - jnp.tile(m_new, (1, n_lane_tiles)))
                l_part = p[:, 0:128]
                for t in range(1, n_lane_tiles):
                    l_part = l_part + p[:, t * 128:(t + 1) * 128]
                l_sc[slot] = alpha * l_sc[slot] + l_part
                acc_sc[slot] = alpha * acc_sc[slot] + jnp.dot(p.astype(BF16), v, preferred_element_type=F32)
                m_sc[slot] = m_new

    @pl.when(ki < qi - 1)
    def _():
        update(None)

    @pl.when(ki == qi - 1)
    def _():
        update(1)

    @pl.when(ki == qi)
    def _():
        update(0)
        for hh in range(FA_HEADS_PER_STEP):
            l0 = jnp.sum(l_sc[2 * hh], axis=-1, keepdims=True)
            l1 = jnp.sum(l_sc[2 * hh + 1], axis=-1, keepdims=True)
            o = acc_sc[2 * hh] / l0 - lam_ref[0] * (acc_sc[2 * hh + 1] / l1)
            y = o * lax.rsqrt(jnp.mean(o * o, axis=-1, keepdims=True) + NORM_EPS)
            o_ref[:, hh * DA_DV:(hh + 1) * DA_DV] = ((y * g_ref[...]) * post_scale).astype(o_ref.dtype)


def prompt_diff_attention(qn, kn, v_src, v_col0, s_len, lam, bias_tiles, g_sub, post_scale, tile):
    hp = FA_HEADS_PER_STEP
    width = hp * DA_DV
    assert v_col0 % width == 0 and DA_HEADS % hp == 0 and 2 * DA_DK == DA_DV
    vb = v_col0 // width
    nq = s_len // tile
    pairs = [(qi, ki) for qi in range(nq) for ki in range(qi + 1)]
    qt = jnp.asarray([p[0] for p in pairs], jnp.int32)
    kt = jnp.asarray([p[1] for p in pairs], jnp.int32)
    grid_spec = pltpu.PrefetchScalarGridSpec(
        num_scalar_prefetch=2,
        grid=(DA_HEADS // hp, len(pairs)),
        in_specs=[
            pl.BlockSpec(memory_space=pltpu.SMEM),
            pl.BlockSpec((tile, width), lambda h, s, qt, kt: (qt[s], h)),
            pl.BlockSpec((tile, width), lambda h, s, qt, kt: (kt[s], h)),
            pl.BlockSpec((tile, width), lambda h, s, qt, kt: (kt[s], vb + h)),
            pl.BlockSpec((hp, 2, tile, tile), lambda h, s, qt, kt: (h, 0, 0, 0)),
            pl.BlockSpec((1, DA_DV), lambda h, s, qt, kt: (0, 0)),
        ],
        out_specs=pl.BlockSpec((tile, width), lambda h, s, qt, kt: (qt[s], h)),
        scratch_shapes=[pltpu.VMEM((2 * hp, tile, 128), F32), pltpu.VMEM((2 * hp, tile, 128), F32),
                        pltpu.VMEM((2 * hp, tile, DA_DV), F32)],
    )
    return pl.pallas_call(
        functools.partial(_fa_kernel, post_scale=post_scale),
        grid_spec=grid_spec,
        out_shape=jax.ShapeDtypeStruct((s_len, DA_VW), BF16),
        compiler_params=_cparams("parallel", "arbitrary"),
        name="prompt_diff_attention",
    )(qt, kt, lam.reshape(1).astype(F32), qn, kn, v_src, bias_tiles, g_sub.reshape(1, DA_DV).astype(F32))


def _topk_rows(s, k):
    riota = lax.broadcasted_iota(jnp.int32, s.shape, 0).astype(F32)
    big = float(s.shape[0])
    vals, idxs = [], []
    for _ in range(k):
        m = jnp.max(s, axis=0, keepdims=True)
        idx = jnp.min(jnp.where(s == m, riota, big), axis=0, keepdims=True)
        vals.append(m)
        idxs.append(idx)
        s = jnp.where(riota == idx, -jnp.inf, s)
    return vals, idxs


def _pruned_pair_candidates(v1, i1, v2, i2):
    k = PEER_TOPK
    tt = v1[0].shape[1]
    cat = lambda rows: jnp.concatenate(rows, axis=0)
    v1m, i1m, v2m, i2m = cat(v1), cat(i1), cat(v2), cat(i2)
    row16 = lax.broadcasted_iota(jnp.int32, (k, tt), 0)
    row8 = lax.broadcasted_iota(jnp.int32, (k // 2, tt), 0)
    half = k // 2
    cand = [v1[0] + v2m]
    c1 = [jnp.broadcast_to(i1[0], (k, tt))]
    c2 = [i2m]
    rank = [row16]
    for a in range(1, half):
        cand.append(jnp.where(row8 < k // (a + 1), v1[a] + v2m[:half], -jnp.inf))
        c1.append(jnp.broadcast_to(i1[a], (half, tt)))
        c2.append(i2m[:half])
        rank.append(a * k + row8)
    cand.append(v1m[half:] + v2[0])
    c1.append(i1m[half:])
    c2.append(jnp.broadcast_to(i2[0], (half, tt)))
    rank.append((half + row8) * k)
    return cat(cand), cat(c1), cat(c2), cat(rank).astype(F32)


def _peer_route_kernel(q_ref, k1_ref, k2_ref, gate_ref, e1_ref, e2_ref):
    k1 = k1_ref[...]
    k2 = k2_ref[...]
    gates, e1s, e2s = [], [], []
    for h in range(PEER_HEADS):
        qa = q_ref[:, (2 * h) * PEER_HALF:(2 * h + 1) * PEER_HALF].astype(BF16)
        qb = q_ref[:, (2 * h + 1) * PEER_HALF:(2 * h + 2) * PEER_HALF].astype(BF16)
        v1, i1 = _topk_rows(_dot_nt(k1, qa), PEER_TOPK)
        v2, i2 = _topk_rows(_dot_nt(k2, qb), PEER_TOPK)
        cand, c1, c2, rank = _pruned_pair_candidates(v1, i1, v2, i2)
        sc, ea, eb = [], [], []
        for _ in range(PEER_TOPK):
            m = jnp.max(cand, axis=0, keepdims=True)
            first = jnp.min(jnp.where(cand == m, rank, float(PEER_TOPK * PEER_TOPK)), axis=0, keepdims=True)
            sel = rank == first
            sc.append(m)
            ea.append(jnp.sum(jnp.where(sel, c1, 0.0), axis=0, keepdims=True))
            eb.append(jnp.sum(jnp.where(sel, c2, 0.0), axis=0, keepdims=True))
            cand = jnp.where(sel, -jnp.inf, cand)
        scm = jnp.concatenate(sc, axis=0)
        ex = jnp.exp(scm - sc[0])
        gates.append(ex / jnp.sum(ex, axis=0, keepdims=True))
        e1s.append(jnp.concatenate(ea, axis=0))
        e2s.append(jnp.concatenate(eb, axis=0))
    gate_ref[...] = jnp.concatenate(gates, axis=0).T
    e1_ref[...] = jnp.concatenate(e1s, axis=0).T
    e2_ref[...] = jnp.concatenate(e2s, axis=0).T


def peer_route(q, k1, k2):
    t = q.shape[0]
    tt = 128
    hk = PEER_HEADS * PEER_TOPK
    out = jax.ShapeDtypeStruct((t, hk), F32)
    ospec = pl.BlockSpec((tt, hk), lambda i: (i, 0))
    kspec = pl.BlockSpec((PEER_KEYS, PEER_HALF), lambda i: (0, 0))
    return pl.pallas_call(
        _peer_route_kernel,
        grid=(t // tt,),
        in_specs=[pl.BlockSpec((tt, q.shape[1]), lambda i: (i, 0)), kspec, kspec],
        out_specs=[ospec, ospec, ospec],
        out_shape=[out, out, out],
        compiler_params=_cparams("parallel"),
        name="peer_route",
    )(q, k1, k2)


def _peer_w_kernel(gate_ref, e1_ref, e2_ref, w_ref):
    tb = gate_ref.shape[0]
    sub = lax.broadcasted_iota(jnp.int32, (PEER_KEYS, gate_ref.shape[1]), 0).astype(F32)

    def body(t, carry):
        g = gate_ref[pl.ds(t, 1), :]
        a = e1_ref[pl.ds(t, 1), :]
        b = e2_ref[pl.ds(t, 1), :]
        lhs = jnp.where(a == sub, g, 0.0).astype(BF16)
        rhs = jnp.where(b == sub, 1.0, 0.0).astype(BF16)
        w_ref[t] = _dot_nt(lhs, rhs)
        return carry

    lax.fori_loop(0, tb, body, 0, unroll=8)


def peer_dense_weights(gate, e1, e2):
    t, hk = gate.shape
    tb = 128
    spec = pl.BlockSpec((tb, hk), lambda i: (i, 0))
    return pl.pallas_call(
        _peer_w_kernel,
        grid=(t // tb,),
        in_specs=[spec, spec, spec],
        out_specs=pl.BlockSpec((tb, PEER_KEYS, PEER_KEYS), lambda i: (i, 0, 0)),
        out_shape=jax.ShapeDtypeStruct((t, PEER_KEYS, PEER_KEYS), F32),
        compiler_params=_cparams("parallel"),
        name="peer_dense_weights",
    )(gate, e1, e2)


PEER_KEY1_PER_STEP = 8


def _peer_ffn_kernel(x_ref, w_ref, u_ref, v_ref, r_ref, o_ref):
    e = pl.program_id(1)

    @pl.when(e == 0)
    def _():
        o_ref[...] = r_ref[...]

    a = _dot_nt(x_ref[...], u_ref[...])
    act = 0.5 * a * (1.0 + lax.erf(a * (2.0 ** -0.5)))
    hmat = jnp.concatenate(
        [(act[:, i * PEER_KEYS:(i + 1) * PEER_KEYS] * w_ref[:, i, :]).astype(BF16)
         for i in range(PEER_KEY1_PER_STEP)], axis=1)
    o_ref[...] += jnp.dot(hmat, v_ref[...], preferred_element_type=F32)


def peer_ffn_dense(xn, w, u_tab, v_tab, resid):
    t, d = xn.shape
    n_exp = u_tab.shape[0]
    tb = _pick_tile(t, (528, 512, 256, 128))
    eb = PEER_KEY1_PER_STEP * PEER_KEYS
    once = pl.Buffered(1)
    return pl.pallas_call(
        _peer_ffn_kernel,
        grid=(t // tb, n_exp // eb),
        in_specs=[pl.BlockSpec((tb, d), lambda i, e: (i, 0), pipeline_mode=once),
                  pl.BlockSpec((tb, PEER_KEY1_PER_STEP, PEER_KEYS), lambda i, e: (i, e, 0)),
                  pl.BlockSpec((eb, d), lambda i, e: (e, 0)),
                  pl.BlockSpec((eb, d), lambda i, e: (e, 0)),
                  pl.BlockSpec((tb, d), lambda i, e: (i, 0), pipeline_mode=once)],
        out_specs=pl.BlockSpec((tb, d), lambda i, e: (i, 0)),
        out_shape=jax.ShapeDtypeStruct((t, d), F32),
        compiler_params=_cparams("parallel", "arbitrary"),
        name="peer_ffn_dense",
    )(xn, w, u_tab, v_tab, resid)


def peer_layer(x, g_ffn, wq, k1, k2, u_tab, v_tab):
    xn = rmsnorm_rows(x, g_ffn)
    q = matmul(xn, wq, name="peer_query")
    gate, e1, e2 = peer_route(q, k1, k2)
    w = peer_dense_weights(gate, e1, e2)
    return peer_ffn_dense(xn, w, u_tab, v_tab, x)


def _group_norm_kernel(x_ref, bd_ref, g_ref, *out_refs, inv_group, scales):
    x = x_ref[...]
    sq = x * x
    hi = sq.astype(BF16)
    lo = (sq - hi.astype(F32)).astype(BF16)
    bd = bd_ref[...]
    ss = jnp.dot(hi, bd, preferred_element_type=F32) + jnp.dot(lo, bd, preferred_element_type=F32)
    y = x * lax.rsqrt(ss * inv_group + NORM_EPS) * g_ref[...]
    for o_ref, sc in zip(out_refs, scales):
        o_ref[...] = (y if sc == 1.0 else y * sc).astype(o_ref.dtype)


def group_norm(src, col0, width, group, gain, outs):
    t = src.shape[0]
    assert col0 % width == 0 and width % group == 0
    tm = _pick_tile(t, (528, 512, 256, 128))
    lane = jnp.arange(width, dtype=jnp.int32) // group
    bd = (lane[:, None] == lane[None, :]).astype(BF16)
    g = jnp.tile(gain.astype(F32), width // group).reshape(1, width)
    ospec = pl.BlockSpec((tm, width), lambda i: (i, 0))
    res = pl.pallas_call(
        functools.partial(_group_norm_kernel, inv_group=1.0 / group, scales=tuple(s for _, s in outs)),
        grid=(t // tm,),
        in_specs=[pl.BlockSpec((tm, width), lambda i: (i, col0 // width)),
                  pl.BlockSpec((width, width), lambda i: (0, 0)),
                  pl.BlockSpec((1, width), lambda i: (0, 0))],
        out_specs=[ospec] * len(outs),
        out_shape=[jax.ShapeDtypeStruct((t, width), dt) for dt, _ in outs],
        compiler_params=_cparams("parallel"),
        name="group_norm",
    )(src, bd, g)
    return res


CONV_HALO = 8


def _conv_kernel(x_ref, halo_ref, w_ref, b_ref, o_ref, sc, *, first_tile_has_no_history):
    tm = x_ref.shape[0]
    halo = halo_ref[...]
    if first_tile_has_no_history:
        halo = jnp.where(pl.program_id(0) == 0, 0.0, halo)
    sc[0:CONV_HALO, :] = halo
    sc[CONV_HALO:, :] = x_ref[...]
    acc = b_ref[...]
    for j in range(SSM_CONV):
        acc = acc + w_ref[j:j + 1, :] * sc[pl.ds(CONV_HALO - (SSM_CONV - 1 - j), tm), :]
    o_ref[...] = jax.nn.silu(acc)


def causal_conv_silu(src, col0, width, row0, rows, tm, halo_src, conv_w, conv_b):
    tc = 512
    assert col0 % tc == 0 and width % tc == 0 and row0 % tm == 0 and rows % tm == 0 and tm % CONV_HALO == 0
    cb, rb, hb = col0 // tc, row0 // tm, tm // CONV_HALO
    if halo_src is None:
        halo_arr = src
        halo_spec = pl.BlockSpec((CONV_HALO, tc), lambda i, j: (jnp.maximum((rb + i) * hb - 1, 0), cb + j))
    else:
        halo_arr = halo_src
        halo_spec = pl.BlockSpec((CONV_HALO, tc), lambda i, j: (i, j))
    return pl.pallas_call(
        functools.partial(_conv_kernel, first_tile_has_no_history=halo_src is None),
        grid=(rows // tm, width // tc),
        in_specs=[pl.BlockSpec((tm, tc), lambda i, j: (rb + i, cb + j)),
                  halo_spec,
                  pl.BlockSpec((SSM_CONV, tc), lambda i, j: (0, j)),
                  pl.BlockSpec((1, tc), lambda i, j: (0, j))],
        out_specs=pl.BlockSpec((tm, tc), lambda i, j: (i, j)),
        out_shape=jax.ShapeDtypeStruct((rows, width), F32),
        scratch_shapes=[pltpu.VMEM((tm + CONV_HALO, tc), F32)],
        compiler_params=_cparams("parallel", "parallel"),
        name="causal_conv_silu",
    )(src, halo_arr, conv_w.astype(F32), conv_b.reshape(1, width).astype(F32))


SSM_INNER = SSM_HEADS * SSM_HEAD_DIM
SSM_GN = SSM_GROUPS * SSM_STATE
HEADS_PER_GROUP = SSM_HEADS // SSM_GROUPS
GROUP_W = HEADS_PER_GROUP * SSM_HEAD_DIM


def _ssd_kernel(xs_ref, b_ref, c_ref, z_ref, dt_ref, dtt_ref, dtb_row_ref, dtb_col_ref, a_row_ref, a_col_ref,
                e_ref, dsk_ref, g_ref, h0_ref, y_ref, hout_ref, h_sc, y_sc, *pad_scs, valid):
    q = SSM_CHUNK
    hi = lax.Precision.HIGHEST
    c = pl.program_id(1)

    @pl.when(c == 0)
    def _():
        h_sc[...] = h0_ref[...]

    def rows(ref, sc):
        if valid == q:
            return ref[...]
        sc[...] = jnp.zeros(sc.shape, F32)
        sc[0:valid, :] = ref[...]
        return sc[...]

    if valid == q:
        pad_scs = (None,) * 5
    xs = rows(xs_ref, pad_scs[0])
    bm = rows(b_ref, pad_scs[1])
    cm = rows(c_ref, pad_scs[2])
    z = rows(z_ref, pad_scs[3])
    dt_raw = rows(dt_ref, pad_scs[4])

    row_i = lax.broadcasted_iota(jnp.int32, (q, q), 0)
    col_i = lax.broadcasted_iota(jnp.int32, (q, q), 1)
    causal = row_i >= col_i
    dt = jax.nn.softplus(dt_raw + dtb_row_ref[...])
    dtt = jax.nn.softplus(dtt_ref[...] + dtb_col_ref[...])
    if valid < q:
        dt = jnp.where(row_i < valid, dt, 0.0)
        dtt = jnp.where(lax.broadcasted_iota(jnp.int32, dtt.shape, 1) < valid, dtt, 0.0)
    acs = jnp.dot(causal.astype(F32), dt * a_row_ref[...], precision=hi, preferred_element_type=F32)
    acst = jnp.dot(dtt * a_col_ref[...], (row_i <= col_i).astype(F32), precision=hi,
                   preferred_element_type=F32)
    expand = e_ref[...]
    dt_e = jnp.dot(dt, expand, precision=hi, preferred_element_type=F32)
    dec_e = jnp.dot(jnp.exp(acs[q - 1:q, :] - acs), expand, precision=hi, preferred_element_type=F32)
    eacs_e = jnp.dot(jnp.exp(acs), expand, precision=hi, preferred_element_type=F32)
    xdt = xs * dt_e
    xdd = xdt * dec_e
    xdt_b = xdt.astype(BF16)
    lane = lax.broadcasted_iota(jnp.int32, (q, 2 * SSM_HEAD_DIM), 1)

    for g in range(SSM_GROUPS):
        bg = bm[:, g * SSM_STATE:(g + 1) * SSM_STATE].astype(BF16)
        cg = cm[:, g * SSM_STATE:(g + 1) * SSM_STATE].astype(BF16)
        cb = _dot_nt(cg, bg)
        for pair in range(HEADS_PER_GROUP // 2):
            h_a = g * HEADS_PER_GROUP + 2 * pair
            slab = slice(h_a * SSM_HEAD_DIM, (h_a + 2) * SSM_HEAD_DIM)
            xpair = xdt_b[:, slab]
            ypair = jnp.zeros((q, 2 * SSM_HEAD_DIM), F32)
            for which in range(2):
                h = h_a + which
                seg = acs[:, h:h + 1] - acst[h:h + 1, :]
                lmat = jnp.exp(jnp.where(causal, seg, -jnp.inf))
                mine = (lane >= which * SSM_HEAD_DIM) & (lane < (which + 1) * SSM_HEAD_DIM)
                ypair = ypair + jnp.dot((cb * lmat).astype(BF16), jnp.where(mine, xpair, jnp.zeros_like(xpair)),
                                        preferred_element_type=F32)
            y_sc[:, slab] = ypair
        gs = slice(g * GROUP_W, (g + 1) * GROUP_W)
        y_sc[:, gs] += _dot_nt(cg, h_sc[gs, :].astype(BF16)) * eacs_e[:, gs]
        st = jnp.dot(xdd[:, gs].T.astype(BF16), bg, preferred_element_type=F32)
        for hh in range(HEADS_PER_GROUP):
            h = g * HEADS_PER_GROUP + hh
            hs = slice(h * SSM_HEAD_DIM, (h + 1) * SSM_HEAD_DIM)
            dec = jnp.exp(acst[h:h + 1, q - 1:q])
            h_sc[hs, :] = h_sc[hs, :] * dec + st[hh * SSM_HEAD_DIM:(hh + 1) * SSM_HEAD_DIM, :]

    y = y_sc[...] + dsk_ref[...] * xs
    y = y * jax.nn.silu(z)
    gain = g_ref[...]
    for g in range(SSM_GROUPS):
        gs = slice(g * GROUP_W, (g + 1) * GROUP_W)
        yg = y[:, gs]
        yn = yg * lax.rsqrt(jnp.mean(yg * yg, axis=-1, keepdims=True) + NORM_EPS) * gain[:, gs]
        y_ref[:, gs] = yn[0:valid].astype(y_ref.dtype)

    @pl.when(c == pl.num_programs(1) - 1)
    def _():
        hout_ref[...] = h_sc[...]


def ssd_branch(xc, u, z_col0, u_row0, dt_pad, dt_t, h0, n_batch, n_chunks, valid, dt_bias, a_log, d_skip, g_ssm,
               out_dtype):
    q = SSM_CHUNK
    rows = xc.shape[0]
    assert rows == n_batch * n_chunks * valid and u_row0 % valid == 0 and z_col0 % SSM_INNER == 0
    rb = u_row0 // valid
    step = lambda b, c: b * n_chunks + c
    pad128 = lambda v: jnp.pad(v.astype(F32), (0, 128 - SSM_HEADS))
    a = -jnp.exp(a_log.astype(F32))
    head_lane = jnp.arange(SSM_INNER, dtype=jnp.int32) // SSM_HEAD_DIM
    expand = (jnp.arange(128, dtype=jnp.int32)[:, None] == head_lane[None, :]).astype(F32)
    const = lambda shape: pl.BlockSpec(shape, lambda b, c: (0,) * len(shape))
    pad_scs = [] if valid == q else [pltpu.VMEM((q, w), F32) for w in (SSM_INNER, SSM_GN, SSM_GN, SSM_INNER, 128)]
    y, h_fin = pl.pallas_call(
        functools.partial(_ssd_kernel, valid=valid),
        grid=(n_batch, n_chunks),
        in_specs=[
            pl.BlockSpec((valid, SSM_INNER), lambda b, c: (step(b, c), 0)),
            pl.BlockSpec((valid, SSM_GN), lambda b, c: (step(b, c), SSM_INNER // SSM_GN)),
            pl.BlockSpec((valid, SSM_GN), lambda b, c: (step(b, c), SSM_INNER // SSM_GN + 1)),
            pl.BlockSpec((valid, SSM_INNER), lambda b, c: (rb + step(b, c), z_col0 // SSM_INNER)),
            pl.BlockSpec((valid, 128), lambda b, c: (rb + step(b, c), 0)),
            pl.BlockSpec((None, SSM_HEADS, q), lambda b, c: (step(b, c), 0, 0)),
            const((1, 128)), const((SSM_HEADS, q)), const((1, 128)), const((SSM_HEADS, q)),
            const((128, SSM_INNER)), const((1, SSM_INNER)), const((1, SSM_INNER)),
            pl.BlockSpec((None, SSM_INNER, SSM_STATE), lambda b, c: (b, 0, 0)),
        ],
        out_specs=[pl.BlockSpec((valid, SSM_INNER), lambda b, c: (step(b, c), 0)),
                   pl.BlockSpec((None, SSM_INNER, SSM_STATE), lambda b, c: (b, 0, 0))],
        out_shape=[jax.ShapeDtypeStruct((rows, SSM_INNER), out_dtype),
                   jax.ShapeDtypeStruct((n_batch, SSM_INNER, SSM_STATE), F32)],
        scratch_shapes=[pltpu.VMEM((SSM_INNER, SSM_STATE), F32), pltpu.VMEM((q, SSM_INNER), F32)] + pad_scs,
        compiler_params=_cparams("parallel", "arbitrary"),
        name="ssd_branch",
    )(xc, xc, xc, u, dt_pad, dt_t,
      pad128(dt_bias).reshape(1, 128), jnp.broadcast_to(dt_bias.astype(F32)[:, None], (SSM_HEADS, q)),
      pad128(a).reshape(1, 128), jnp.broadcast_to(a[:, None], (SSM_HEADS, q)),
      expand, jnp.repeat(d_skip.astype(F32), SSM_HEAD_DIM).reshape(1, SSM_INNER),
      g_ssm.astype(F32).reshape(1, SSM_INNER), h0)
    return y, h_fin


def _mem_attn_kernel(q_ref, k_ref, v_ref, o_ref):
    for h in range(MEM_HEADS):
        hs = slice(h * MEM_HEAD_DIM, (h + 1) * MEM_HEAD_DIM)
        s = _dot_nt(q_ref[:, hs].astype(BF16), k_ref[:, hs].astype(BF16))
        p = jnp.exp(s - jnp.max(s, axis=-1, keepdims=True))
        o = jnp.dot(p.astype(BF16), v_ref[:, hs].astype(BF16), preferred_element_type=F32)
        o_ref[:, hs] = (o / jnp.sum(p, axis=-1, keepdims=True)).astype(o_ref.dtype)


def memory_attention(qn, q_row0, rows, tq, mk, mv, v_col_blk, tiles_per_batch, out_dtype):
    n_mem = mk.shape[0] * tq * tiles_per_batch // rows
    rb = q_row0 // tq
    return pl.pallas_call(
        _mem_attn_kernel,
        grid=(rows // tq,),
        in_specs=[pl.BlockSpec((tq, MEM_W), lambda i: (rb + i, 0)),
                  pl.BlockSpec((n_mem, MEM_W), lambda i: (i // tiles_per_batch, 0)),
                  pl.BlockSpec((n_mem, MEM_W), lambda i: (i // tiles_per_batch, v_col_blk))],
        out_specs=pl.BlockSpec((tq, MEM_W), lambda i: (i, 0)),
        out_shape=jax.ShapeDtypeStruct((rows, MEM_W), out_dtype),
        compiler_params=_cparams("parallel"),
        name="memory_attention",
    )(qn, mk, mv)


SAMPLE_PAGES_PER_STEP = 16
SAMPLE_ROWS = DA_HEADS * 2 * 8


def _sample_attn_kernel(pt_ref, lam_ref, q_ref, *refs, pages_per_step, post_scale):
    n = pages_per_step
    kt_refs, v_refs = refs[:n], refs[n:2 * n]
    knew_ref, vnew_ref, blast_ref, bnew_ref, rexp_ref, hmask_ref, g_ref, o_ref, m_sc, l_sc, acc_sc = refs[2 * n:]
    del pt_ref
    step = pl.program_id(1)
    n_new = o_ref.shape[0]
    rows_per_head = 2 * n_new

    @pl.when(step == 0)
    def _():
        m_sc[...] = jnp.full(m_sc.shape, NEG_BIG, F32)
        l_sc[...] = jnp.zeros(l_sc.shape, F32)
        acc_sc[...] = jnp.zeros(acc_sc.shape, F32)

    q = q_ref[...]

    def update(pages):
        logits = []
        for kt_ref, _, bias in pages:
            s = jnp.dot(q, kt_ref[...].astype(BF16), preferred_element_type=F32)
            logits.append(s if bias is None else s + bias)
        s_max = logits[0]
        for s in logits[1:]:
            s_max = jnp.maximum(s_max, s)
        m_prev = m_sc[...]
        m_new = jnp.maximum(m_prev, jnp.max(s_max, axis=-1, keepdims=True))
        alpha = jnp.exp2(m_prev - m_new)
        m_sc[...] = m_new
        l_new = alpha * l_sc[...]
        acc = alpha * acc_sc[...]
        rexp = rexp_ref[...]
        own_head = hmask_ref[...]
        for (_, v_ref, _), s in zip(pages, logits):
            p = jnp.exp2(s - m_new)
            l_new = l_new + p
            p_rep = jnp.dot(p.astype(BF16), rexp, preferred_element_type=F32)
            acc = acc + jnp.dot(p_rep.astype(BF16) * own_head, v_ref[...].astype(BF16),
                                preferred_element_type=F32)
        l_sc[...] = l_new
        acc_sc[...] = acc

    last = pl.num_programs(1) - 1
    past = [(kt_refs[i], v_refs[i], None) for i in range(n)]

    @pl.when(step < last)
    def _():
        update(past)

    @pl.when(step == last)
    def _():
        update(past[:-1] + [(kt_refs[n - 1], v_refs[n - 1], blast_ref[...]), (knew_ref, vnew_ref, bnew_ref[...])])
        attn = acc_sc[...] / jnp.sum(l_sc[...], axis=-1, keepdims=True)
        for h in range(DA_HEADS):
            r0 = h * rows_per_head
            o = attn[r0:r0 + n_new, :] - lam_ref[0] * attn[r0 + n_new:r0 + 2 * n_new, :]
            y = o * lax.rsqrt(jnp.mean(o * o, axis=-1, keepdims=True) + NORM_EPS)
            o_ref[:, h * DA_DV:(h + 1) * DA_DV] = (y * g_ref[...]) * post_scale


def sample_bias_tiles(table, n_new, page):
    def rows(off):
        b = _toeplitz_bias(table, off, page)[:, :n_new, :]
        return jnp.broadcast_to(b[:, None], (DA_HEADS, 2, n_new, page)).reshape(SAMPLE_ROWS, page)
    return rows(page), rows(0)


def sample_diff_attention(layer, qn_s, kn_s, v_s, lam, k_pool_t, v_pool, page_table, b_last, b_new, g_sub,
                          post_scale):
    bsz, n_new, qw = qn_s.shape
    page = v_pool.shape[2]
    n_pages = page_table.shape[1]
    n = SAMPLE_PAGES_PER_STEP
    assert n_pages % n == 0 and SAMPLE_ROWS == DA_HEADS * 2 * n_new
    qg = qn_s.reshape(bsz, n_new, DA_HEADS * 2, DA_DK)
    eye = jnp.eye(DA_HEADS * 2, dtype=F32)
    q_rows = (qg.transpose(0, 2, 1, 3)[:, :, :, None, :] * eye[None, :, None, :, None]).reshape(
        bsz, SAMPLE_ROWS, qw).astype(BF16)
    knew_t = jnp.pad(kn_s.transpose(0, 2, 1), ((0, 0), (0, 0), (0, page - n_new)))
    vnew = jnp.pad(v_s, ((0, 0), (0, page - n_new), (0, 0), (0, 0))).reshape(bsz, page * DA_HEADS, DA_DV)
    v_rows = v_pool.reshape(v_pool.shape[0], v_pool.shape[1], page * DA_HEADS, DA_DV)
    col = jnp.arange(page * DA_HEADS, dtype=jnp.int32)
    rexp = (col[None, :] // DA_HEADS == jnp.arange(page, dtype=jnp.int32)[:, None]).astype(BF16)
    hmask = (col[None, :] % DA_HEADS
             == jnp.arange(SAMPLE_ROWS, dtype=jnp.int32)[:, None] // (2 * n_new)).astype(BF16)
    kspec = lambda i: pl.BlockSpec((None, None, qw, page), lambda b, s, pt: (layer, pt[b, s * n + i], 0, 0))
    vspec = lambda i: pl.BlockSpec((None, None, page * DA_HEADS, DA_DV),
                                   lambda b, s, pt: (layer, pt[b, s * n + i], 0, 0))
    const2 = lambda shape: pl.BlockSpec(shape, lambda b, s, pt: (0, 0))
    grid_spec = pltpu.PrefetchScalarGridSpec(
        num_scalar_prefetch=1,
        grid=(bsz, n_pages // n),
        in_specs=[pl.BlockSpec(memory_space=pltpu.SMEM),
                  pl.BlockSpec((None, SAMPLE_ROWS, qw), lambda b, s, pt: (b, 0, 0))]
                 + [kspec(i) for i in range(n)] + [vspec(i) for i in range(n)]
                 + [pl.BlockSpec((None, qw, page), lambda b, s, pt: (b, 0, 0)),
                    pl.BlockSpec((None, page * DA_HEADS, DA_DV), lambda b, s, pt: (b, 0, 0)),
                    const2((SAMPLE_ROWS, page)), const2((SAMPLE_ROWS, page)),
                    const2((page, page * DA_HEADS)), const2((SAMPLE_ROWS, page * DA_HEADS)), const2((1, DA_DV))],
        out_specs=pl.BlockSpec((n_new, DA_VW), lambda b, s, pt: (b, 0)),
        scratch_shapes=[pltpu.VMEM((SAMPLE_ROWS, page), F32), pltpu.VMEM((SAMPLE_ROWS, page), F32),
                        pltpu.VMEM((SAMPLE_ROWS, DA_DV), F32)],
    )
    return pl.pallas_call(
        functools.partial(_sample_attn_kernel, pages_per_step=n, post_scale=post_scale),
        grid_spec=grid_spec,
        out_shape=jax.ShapeDtypeStruct((bsz * n_new, DA_VW), F32),
        compiler_params=_cparams("parallel", "arbitrary"),
        name="sample_diff_attention",
    )(page_table, lam.reshape(1).astype(F32), q_rows, *([k_pool_t] * n), *([v_rows] * n), knew_t, vnew,
      b_last, b_new, rexp, hmask, g_sub.reshape(1, DA_DV).astype(F32))


FA_TILE = 512


def kernel(x_prompt, x_sample, cache_attn_k, cache_attn_v, cache_mem_k, cache_mem_v, state_ssm, state_conv, page_table, mem_prompt, rel_bias, g_mix, w_in, g_q, g_k, lam_q1, lam_k1, lam_q2, lam_k2, g_sub, conv_w, conv_b, dt_bias, a_log, d_skip, g_ssm, g_mem, w_mem_kv, g_mq, g_mk, w_br_attn, w_br_ssm, w_br_mem, w_out, g_ffn, peer_wq, peer_k1, peer_k2, peer_u, peer_v):
    depth = w_in.shape[0]
    bp, sp, d = x_prompt.shape
    bs, ts, _ = x_sample.shape
    assert bp == 1 and sp % SSM_CHUNK == 0 and sp % FA_TILE == 0 and ts >= SSM_CONV - 1
    n_p = bp * sp
    n_s = bs * ts
    conv_dim = SSM_INNER + 2 * SSM_GN
    n_mem = mem_prompt.shape[1]
    n_pool, page = cache_attn_v.shape[1], cache_attn_v.shape[2]
    s_z = 2 * DA_QW + DA_VW
    s_dt = s_z + SSM_INNER + conv_dim
    s_mq = s_dt + SSM_HEADS
    c_z, c_xbc = 0, SSM_INNER
    c_q = c_xbc + conv_dim
    c_k = c_q + DA_QW
    c_v = c_k + DA_QW
    c_mq = c_v + DA_VW
    c_gate = c_mq + MEM_W

    x = jnp.concatenate([x_prompt.reshape(n_p, d), x_sample.reshape(n_s, d)], axis=0)
    bias_tiles = prompt_bias_tiles(rel_bias, FA_TILE)
    b_last, b_new = sample_bias_tiles(rel_bias, ts, page)
    k_pool_t = jnp.transpose(cache_attn_k, (0, 1, 3, 4, 5, 2)).reshape(depth, n_pool, DA_QW, page)
    zero_h = jnp.zeros((bp, SSM_INNER, SSM_STATE), F32)

    outs = {k: [] for k in ("pk", "pv", "pmk", "pmv", "ph", "pc", "sk", "sv", "sh", "sc")}
    for l in range(depth):
        lam_init = 0.8 - 0.6 * math.exp(-0.3 * l)
        lam = (jnp.exp(jnp.sum(lam_q1[l].astype(F32) * lam_k1[l].astype(F32)))
               - jnp.exp(jnp.sum(lam_q2[l].astype(F32) * lam_k2[l].astype(F32))) + lam_init)
        wl = w_in[l]
        w_main = jnp.concatenate([wl[:, s_z:s_dt], wl[:, :s_z], wl[:, s_mq:]], axis=1).astype(BF16)
        w_dt = jnp.pad(wl[:, s_dt:s_mq], ((0, 0), (0, 128 - SSM_HEADS))).astype(BF16)

        h = rmsnorm_rows(x, g_mix[l])
        u = matmul(h, w_main, name="in_proj")
        dt_pad = matmul(h, w_dt, name="dt_proj")
        (qn,) = group_norm(u, c_q, DA_QW, DA_DK, g_q[l], [(BF16, DA_SCALE * LOG2E)])
        kn, kn_b = group_norm(u, c_k, DA_QW, DA_DK, g_k[l], [(F32, 1.0), (BF16, 1.0)])
        (mqn,) = group_norm(u, c_mq, MEM_W, MEM_HEAD_DIM, g_mq[l], [(F32, MEM_SCALE)])
        v_all = u[:, c_v:c_v + DA_VW]
        xbc_s = u[n_p:, c_xbc:c_xbc + conv_dim].reshape(bs, ts, conv_dim)

        a_p = prompt_diff_attention(qn, kn_b, u, c_v, n_p, lam, bias_tiles, g_sub[l], 1.0 - lam_init, FA_TILE)
        a_s = sample_diff_attention(l, qn[n_p:].astype(F32).reshape(bs, ts, DA_QW), kn[n_p:].reshape(bs, ts, DA_QW),
                                    v_all[n_p:].reshape(bs, ts, DA_HEADS, DA_DV), lam, k_pool_t, cache_attn_v,
                                    page_table, b_last, b_new, g_sub[l], 1.0 - lam_init)

        hm = rmsnorm_rows(mem_prompt.reshape(bp * n_mem, d), g_mem[l])
        mkv = matmul(hm, w_mem_kv[l].astype(BF16), name="mem_kv")
        (mk_p,) = group_norm(mkv, 0, MEM_W, MEM_HEAD_DIM, g_mk[l], [(F32, 1.0)])
        tq = 512
        m_p = memory_attention(mqn, 0, n_p, tq, mk_p, mkv, 1, sp // tq, BF16)
        m_s = memory_attention(mqn, n_p, n_s, ts, cache_mem_k[l].reshape(bs * n_mem, MEM_W),
                               cache_mem_v[l].reshape(bs * n_mem, MEM_W), 0, 1, F32)

        xc_p = causal_conv_silu(u, c_xbc, conv_dim, 0, n_p, 512, None, conv_w[l], conv_b[l])
        halo_s = jnp.pad(state_conv[l], ((0, 0), (CONV_HALO - (SSM_CONV - 1), 0), (0, 0))).reshape(
            bs * CONV_HALO, conv_dim)
        xc_s = causal_conv_silu(u, c_xbc, conv_dim, n_p, n_s, ts, halo_s, conv_w[l], conv_b[l])
        dt_raw = dt_pad[:, :SSM_HEADS]
        dtt_p = dt_raw[:n_p].reshape(n_p // SSM_CHUNK, SSM_CHUNK, SSM_HEADS).transpose(0, 2, 1)
        dtt_s = jnp.pad(dt_raw[n_p:].reshape(bs, ts, SSM_HEADS).transpose(0, 2, 1),
                        ((0, 0), (0, 0), (0, SSM_CHUNK - ts)))
        s_p, h_p = ssd_branch(xc_p, u, c_z, 0, dt_pad, dtt_p, zero_h, bp, sp // SSM_CHUNK, SSM_CHUNK,
                              dt_bias[l], a_log[l], d_skip[l], g_ssm[l], BF16)
        s_s, h_s = ssd_branch(xc_s, u, c_z, n_p, dt_pad, dtt_s, state_ssm[l].reshape(bs, SSM_INNER, SSM_STATE),
                              bs, 1, ts, dt_bias[l], a_log[l], d_skip[l], g_ssm[l], F32)

        a_out = jnp.concatenate([a_p, a_s.astype(BF16)], axis=0)
        s_out = jnp.concatenate([s_p, s_s.astype(BF16)], axis=0)
        m_out = jnp.concatenate([m_p, m_s.astype(BF16)], axis=0)
        merged = merge_branches(a_out, s_out, m_out, u, c_gate, w_br_attn[l].astype(BF16),
                                w_br_ssm[l].astype(BF16), w_br_mem[l].astype(BF16))
        x = matmul(merged, w_out[l].astype(BF16), residual=x, name="out_proj")
        x = peer_layer(x, g_ffn[l], peer_wq[l].astype(BF16), peer_k1[l].astype(BF16), peer_k2[l].astype(BF16),
                       peer_u[l].astype(BF16), peer_v[l].astype(BF16))

        outs["pk"].append(kn[:n_p].reshape(bp, sp, DA_HEADS, 2, DA_DK))
        outs["pv"].append(v_all[:n_p].reshape(bp, sp, DA_HEADS, DA_DV))
        outs["pmk"].append(mk_p.reshape(bp, n_mem, MEM_HEADS, MEM_HEAD_DIM))
        outs["pmv"].append(mkv[:, MEM_W:].reshape(bp, n_mem, MEM_HEADS, MEM_HEAD_DIM))
        outs["ph"].append(h_p.reshape(bp, SSM_HEADS, SSM_HEAD_DIM, SSM_STATE))
        outs["pc"].append(u[n_p - (SSM_CONV - 1):n_p, c_xbc:c_xbc + conv_dim].reshape(bp, SSM_CONV - 1, conv_dim))
        outs["sk"].append(kn[n_p:].reshape(bs, ts, DA_HEADS, 2, DA_DK))
        outs["sv"].append(v_all[n_p:].reshape(bs, ts, DA_HEADS, DA_DV))
        outs["sh"].append(h_s.reshape(bs, SSM_HEADS, SSM_HEAD_DIM, SSM_STATE))
        outs["sc"].append(xbc_s[:, ts - (SSM_CONV - 1):])

    y_prompt = x[:n_p].reshape(bp, sp, d)
    y_sample = x[n_p:].reshape(bs, ts, d)
    st = lambda name: jnp.stack(outs[name])
    return (y_prompt, y_sample, st("pk"), st("pv"), st("pmk"), st("pmv"), st("ph"), st("pc"),
            st("sk"), st("sv"), st("sh"), st("sc"))
```

```python
import functools
import math

import jax
import jax.numpy as jnp
from jax import lax
from jax.experimental import pallas as pl
from jax.experimental.pallas import tpu as pltpu

F32 = jnp.float32
BF16 = jnp.bfloat16

NORM_EPS = 1e-6
NEG_BIG = -1e30
V7X_VMEM_LIMIT_BYTES = 56 * 1024 * 1024

DA_HEADS, DA_DK, DA_DV = 8, 64, 128
DA_QW = DA_HEADS * 2 * DA_DK
DA_VW = DA_HEADS * DA_DV
DA_SCALE = DA_DK ** -0.5
N_BUCKETS, MAX_DISTANCE = 32, 128
SSM_HEADS, SSM_HEAD_DIM, SSM_GROUPS, SSM_STATE, SSM_CONV, SSM_CHUNK = 32, 64, 4, 128, 4, 128
MEM_HEADS, MEM_HEAD_DIM = 4, 256
MEM_W = MEM_HEADS * MEM_HEAD_DIM
MEM_SCALE = MEM_HEAD_DIM ** -0.5
PEER_HEADS, PEER_KEYS, PEER_TOPK, PEER_HALF = 8, 128, 16, 128


def _cparams(*sem):
    return pltpu.CompilerParams(dimension_semantics=sem, vmem_limit_bytes=V7X_VMEM_LIMIT_BYTES)


def _dot_nt(a, b):
    return lax.dot_general(a, b, (((1,), (1,)), ((), ())), preferred_element_type=F32)


def _pick_tile(n, candidates):
    for c in candidates:
        if n % c == 0:
            return c
    return n


def _norm_mm_kernel(x_ref, g_ref, w_ref, *rest, has_side, emit_h):
    rest = list(rest)
    side_w_ref = rest.pop(0) if has_side else None
    o_ref = rest.pop(0)
    side_o_ref = rest.pop(0) if has_side else None
    h_o_ref = rest.pop(0) if emit_h else None
    (h_sc,) = rest

    @pl.when(pl.program_id(1) == 0)
    def _():
        x = x_ref[...]
        ms = jnp.mean(x * x, axis=-1, keepdims=True)
        h = (x * lax.rsqrt(ms + NORM_EPS) * g_ref[...]).astype(BF16)
        h_sc[...] = h
        if has_side:
            side_o_ref[...] = jnp.dot(h, side_w_ref[...], preferred_element_type=F32)
        if emit_h:
            h_o_ref[...] = h

    o_ref[...] = jnp.dot(h_sc[...], w_ref[...], preferred_element_type=F32)


def norm_matmul(x, g, w, side_w=None, emit_h=False, name="norm_matmul"):
    t, d = x.shape
    n = w.shape[1]
    tm = _pick_tile(t, (528, 512, 256, 128))
    tn = _pick_tile(n, (1536, 1024, 512, 256, 128))
    row = lambda width: pl.BlockSpec((tm, width), lambda i, j: (i, 0))
    in_specs = [row(d), pl.BlockSpec((1, d), lambda i, j: (0, 0)), pl.BlockSpec((d, tn), lambda i, j: (0, j))]
    args = [x, g.reshape(1, d).astype(F32), w]
    out_specs = [pl.BlockSpec((tm, tn), lambda i, j: (i, j))]
    out_shape = [jax.ShapeDtypeStruct((t, n), F32)]
    if side_w is not None:
        ns = side_w.shape[1]
        in_specs.append(pl.BlockSpec((d, ns), lambda i, j: (0, 0)))
        args.append(side_w)
        out_specs.append(row(ns))
        out_shape.append(jax.ShapeDtypeStruct((t, ns), F32))
    if emit_h:
        out_specs.append(row(d))
        out_shape.append(jax.ShapeDtypeStruct((t, d), BF16))
    return pl.pallas_call(
        functools.partial(_norm_mm_kernel, has_side=side_w is not None, emit_h=emit_h),
        grid=(t // tm, n // tn),
        in_specs=in_specs,
        out_specs=out_specs,
        out_shape=out_shape,
        scratch_shapes=[pltpu.VMEM((tm, d), BF16)],
        compiler_params=_cparams("parallel", "arbitrary"),
        name=name,
    )(*args)


def _mm_kernel(a_ref, w_ref, o_ref):
    o_ref[...] = jnp.dot(a_ref[...], w_ref[...], preferred_element_type=F32).astype(o_ref.dtype)


def _mm_res_kernel(a_ref, w_ref, r_ref, o_ref):
    o_ref[...] = r_ref[...] + jnp.dot(a_ref[...], w_ref[...], preferred_element_type=F32)


def matmul(a, w, residual=None, out_dtype=F32, name="matmul"):
    t, k = a.shape
    n = w.shape[1]
    tm = _pick_tile(t, (528, 512, 256, 128))
    tn = _pick_tile(n, (1536, 1024, 512, 256, 128))
    in_specs = [pl.BlockSpec((tm, k), lambda i, j: (i, 0)), pl.BlockSpec((k, tn), lambda i, j: (0, j))]
    args = [a, w]
    body = _mm_kernel
    if residual is not None:
        in_specs.append(pl.BlockSpec((tm, tn), lambda i, j: (i, j)))
        args.append(residual)
        body = _mm_res_kernel
    return pl.pallas_call(
        body,
        grid=(t // tm, n // tn),
        in_specs=in_specs,
        out_specs=pl.BlockSpec((tm, tn), lambda i, j: (i, j)),
        out_shape=jax.ShapeDtypeStruct((t, n), out_dtype),
        compiler_params=_cparams("parallel", "parallel"),
        name=name,
    )(*args)


def _merge_kernel(a_ref, s_ref, m_ref, ga_ref, gs_ref, gm_ref, wa_ref, ws_ref, wm_ref, o_ref):
    acc = jax.nn.sigmoid(ga_ref[...]) * jnp.dot(a_ref[...], wa_ref[...], preferred_element_type=F32)
    acc += jax.nn.sigmoid(gs_ref[...]) * jnp.dot(s_ref[...], ws_ref[...], preferred_element_type=F32)
    acc += jax.nn.sigmoid(gm_ref[...]) * jnp.dot(m_ref[...], wm_ref[...], preferred_element_type=F32)
    o_ref[...] = acc.astype(o_ref.dtype)


def merge_branches(a, s, m, u, gate_col0, wa, ws, wm):
    t = a.shape[0]
    d = wa.shape[1]
    tm = _pick_tile(t, (528, 512, 256, 128))
    tn = 512
    gb = gate_col0 // tn
    nd = d // tn
    row = lambda width: pl.BlockSpec((tm, width), lambda i, j: (i, 0))
    gate = lambda which: pl.BlockSpec((tm, tn), lambda i, j: (i, gb + which * nd + j))
    wcol = lambda kdim: pl.BlockSpec((kdim, tn), lambda i, j: (0, j))
    return pl.pallas_call(
        _merge_kernel,
        grid=(t // tm, nd),
        in_specs=[row(a.shape[1]), row(s.shape[1]), row(m.shape[1]), gate(0), gate(1), gate(2),
                  wcol(wa.shape[0]), wcol(ws.shape[0]), wcol(wm.shape[0])],
        out_specs=pl.BlockSpec((tm, tn), lambda i, j: (i, j)),
        out_shape=jax.ShapeDtypeStruct((t, d), BF16),
        compiler_params=_cparams("parallel", "parallel"),
        name="merge_branches",
    )(a, s, m, u, u, u, wa, ws, wm)


def _t5_bucket(dist):
    n = jnp.maximum(dist, 0)
    exact = N_BUCKETS // 2
    nf = jnp.maximum(n, exact).astype(F32)
    large = exact + (jnp.log(nf / exact) / math.log(MAX_DISTANCE / exact) * (N_BUCKETS - exact)).astype(jnp.int32)
    return jnp.where(n < exact, n, jnp.minimum(large, N_BUCKETS - 1))


def _far_bias(table):
    return table.astype(F32)[N_BUCKETS - 1]


LOG2E = math.log2(math.e)


def _toeplitz_bias(table, off, t):
    x = jnp.arange(2 * t, dtype=jnp.int32)
    dist = jnp.where(x < t, off - x, off + 2 * t - x)
    far = _far_bias(table)
    v = (jnp.moveaxis(table.astype(F32)[_t5_bucket(dist)], -1, 0) - far[:, None]) * LOG2E
    v = jnp.where(dist[None, :] >= 0, v, NEG_BIG)
    h = v.shape[0]
    return jnp.tile(v, (1, t))[:, :t * (2 * t - 1)].reshape(h, t, 2 * t - 1)[:, :, :t]


def prompt_bias_tiles(table, tile):
    assert tile >= MAX_DISTANCE
    return jnp.stack([_toeplitz_bias(table, 0, tile), _toeplitz_bias(table, tile, tile)], axis=1)


FA_HEADS_PER_STEP = 2


def _fa_kernel(qt_ref, kt_ref, lam_ref, q_ref, k_ref, v_ref, b_ref, g_ref, o_ref, m_sc, l_sc, acc_sc, *,
               post_scale):
    s_id = pl.program_id(1)
    qi = qt_ref[s_id]
    ki = kt_ref[s_id]

    @pl.when(ki == 0)
    def _():
        m_sc[...] = jnp.full(m_sc.shape, NEG_BIG, F32)
        l_sc[...] = jnp.zeros(l_sc.shape, F32)
        acc_sc[...] = jnp.zeros(acc_sc.shape, F32)

    def update(bias_idx):
        n_lane_tiles = k_ref.shape[0] // 128
        lane = lax.broadcasted_iota(jnp.int32, (q_ref.shape[0], 2 * DA_DK), 1)
        for hh in range(FA_HEADS_PER_STEP):
            hs = slice(hh * 2 * DA_DK, (hh + 1) * 2 * DA_DK)
            q = q_ref[:, hs]
            k = k_ref[:, hs]
            v = v_ref[:, hh * DA_DV:(hh + 1) * DA_DV].astype(BF16)
            logits = []
            for m in range(2):
                in_map = (lane >= m * DA_DK) & (lane < (m + 1) * DA_DK)
                s = _dot_nt(jnp.where(in_map, q, jnp.zeros_like(q)), k)
                logits.append(s if bias_idx is None else s + b_ref[hh, bias_idx])
            for m in range(2):
                s = logits[m]
                slot = 2 * hh + m
                m_prev = m_sc[slot]
                m_new = jnp.maximum(m_prev, jnp.max(s, axis=-1, keepdims=True))
                alpha = jnp.exp2(m_prev - m_new)
                p = jnp.exp2(s - jnp.tile(m_new, (1, n_lane_tiles)))
                l_part = p[:, 0:128]
                for t in range(1, n_lane_tiles):
                    l_part = l_part + p[:, t * 128:(t + 1) * 128]
                l_sc[slot] = alpha * l_sc[slot] + l_part
                acc_sc[slot] = alpha * acc_sc[slot] + jnp.dot(p.astype(BF16), v, preferred_element_type=F32)
                m_sc[slot] = m_new

    @pl.when(ki < qi - 1)
    def _():
        update(None)

    @pl.when(ki == qi - 1)
    def _():
        update(1)

    @pl.when(ki == qi)
    def _():
        update(0)
        for hh in range(FA_HEADS_PER_STEP):
            l0 = jnp.sum(l_sc[2 * hh], axis=-1, keepdims=True)
            l1 = jnp.sum(l_sc[2 * hh + 1], axis=-1, keepdims=True)
            o = acc_sc[2 * hh] / l0 - lam_ref[0] * (acc_sc[2 * hh + 1] / l1)
            y = o * lax.rsqrt(jnp.mean(o * o, axis=-1, keepdims=True) + NORM_EPS)
            o_ref[:, hh * DA_DV:(hh + 1) * DA_DV] = ((y * g_ref[...]) * post_scale).astype(o_ref.dtype)


def prompt_diff_attention(qn, kn, v_src, v_col0, s_len, lam, bias_tiles, g_sub, post_scale, tile):
    hp = FA_HEADS_PER_STEP
    width = hp * DA_DV
    assert v_col0 % width == 0 and DA_HEADS % hp == 0 and 2 * DA_DK == DA_DV
    vb = v_col0 // width
    nq = s_len // tile
    pairs = [(qi, ki) for qi in range(nq) for ki in range(qi + 1)]
    qt = jnp.asarray([p[0] for p in pairs], jnp.int32)
    kt = jnp.asarray([p[1] for p in pairs], jnp.int32)
    grid_spec = pltpu.PrefetchScalarGridSpec(
        num_scalar_prefetch=2,
        grid=(DA_HEADS // hp, len(pairs)),
        in_specs=[
            pl.BlockSpec(memory_space=pltpu.SMEM),
            pl.BlockSpec((tile, width), lambda h, s, qt, kt: (qt[s], h)),
            pl.BlockSpec((tile, width), lambda h, s, qt, kt: (kt[s], h)),
            pl.BlockSpec((tile, width), lambda h, s, qt, kt: (kt[s], vb + h)),
            pl.BlockSpec((hp, 2, tile, tile), lambda h, s, qt, kt: (h, 0, 0, 0)),
            pl.BlockSpec((1, DA_DV), lambda h, s, qt, kt: (0, 0)),
        ],
        out_specs=pl.BlockSpec((tile, width), lambda h, s, qt, kt: (qt[s], h)),
        scratch_shapes=[pltpu.VMEM((2 * hp, tile, 128), F32), pltpu.VMEM((2 * hp, tile, 128), F32),
                        pltpu.VMEM((2 * hp, tile, DA_DV), F32)],
    )
    return pl.pallas_call(
        functools.partial(_fa_kernel, post_scale=post_scale),
        grid_spec=grid_spec,
        out_shape=jax.ShapeDtypeStruct((s_len, DA_VW), BF16),
        compiler_params=_cparams("parallel", "arbitrary"),
        name="prompt_diff_attention",
    )(qt, kt, lam.reshape(1).astype(F32), qn, kn, v_src, bias_tiles, g_sub.reshape(1, DA_DV).astype(F32))


def _topk_rows(s, k):
    riota = lax.broadcasted_iota(jnp.int32, s.shape, 0).astype(F32)
    big = float(s.shape[0])
    vals, idxs = [], []
    for _ in range(k):
        m = jnp.max(s, axis=0, keepdims=True)
        idx = jnp.min(jnp.where(s == m, riota, big), axis=0, keepdims=True)
        vals.append(m)
        idxs.append(idx)
        s = jnp.where(riota == idx, -jnp.inf, s)
    return vals, idxs


def _pruned_pair_candidates(v1, i1, v2, i2):
    k = PEER_TOPK
    tt = v1[0].shape[1]
    cat = lambda rows: jnp.concatenate(rows, axis=0)
    v1m, i1m, v2m, i2m = cat(v1), cat(i1), cat(v2), cat(i2)
    row16 = lax.broadcasted_iota(jnp.int32, (k, tt), 0)
    row8 = lax.broadcasted_iota(jnp.int32, (k // 2, tt), 0)
    half = k // 2
    cand = [v1[0] + v2m]
    c1 = [jnp.broadcast_to(i1[0], (k, tt))]
    c2 = [i2m]
    rank = [row16]
    for a in range(1, half):
        cand.append(jnp.where(row8 < k // (a + 1), v1[a] + v2m[:half], -jnp.inf))
        c1.append(jnp.broadcast_to(i1[a], (half, tt)))
        c2.append(i2m[:half])
        rank.append(a * k + row8)
    cand.append(v1m[half:] + v2[0])
    c1.append(i1m[half:])
    c2.append(jnp.broadcast_to(i2[0], (half, tt)))
    rank.append((half + row8) * k)
    return cat(cand), cat(c1), cat(c2), cat(rank).astype(F32)


def _peer_route_kernel(q_ref, k1_ref, k2_ref, gate_ref, e1_ref, e2_ref):
    k1 = k1_ref[...]
    k2 = k2_ref[...]
    gates, e1s, e2s = [], [], []
    for h in range(PEER_HEADS):
        qa = q_ref[:, (2 * h) * PEER_HALF:(2 * h + 1) * PEER_HALF].astype(BF16)
        qb = q_ref[:, (2 * h + 1) * PEER_HALF:(2 * h + 2) * PEER_HALF].astype(BF16)
        v1, i1 = _topk_rows(_dot_nt(k1, qa), PEER_TOPK)
        v2, i2 = _topk_rows(_dot_nt(k2, qb), PEER_TOPK)
        cand, c1, c2, rank = _pruned_pair_candidates(v1, i1, v2, i2)
        sc, ea, eb = [], [], []
        for _ in range(PEER_TOPK):
            m = jnp.max(cand, axis=0, keepdims=True)
            first = jnp.min(jnp.where(cand == m, rank, float(PEER_TOPK * PEER_TOPK)), axis=0, keepdims=True)
            sel = rank == first
            sc.append(m)
            ea.append(jnp.sum(jnp.where(sel, c1, 0.0), axis=0, keepdims=True))
            eb.append(jnp.sum(jnp.where(sel, c2, 0.0), axis=0, keepdims=True))
            cand = jnp.where(sel, -jnp.inf, cand)
        scm = jnp.concatenate(sc, axis=0)
        ex = jnp.exp(scm - sc[0])
        gates.append(ex / jnp.sum(ex, axis=0, keepdims=True))
        e1s.append(jnp.concatenate(ea, axis=0))
        e2s.append(jnp.concatenate(eb, axis=0))
    gate_ref[...] = jnp.concatenate(gates, axis=0).T
    e1_ref[...] = jnp.concatenate(e1s, axis=0).T
    e2_ref[...] = jnp.concatenate(e2s, axis=0).T


def peer_route(q, k1, k2):
    t = q.shape[0]
    tt = 128
    hk = PEER_HEADS * PEER_TOPK
    out = jax.ShapeDtypeStruct((t, hk), F32)
    ospec = pl.BlockSpec((tt, hk), lambda i: (i, 0))
    kspec = pl.BlockSpec((PEER_KEYS, PEER_HALF), lambda i: (0, 0))
    return pl.pallas_call(
        _peer_route_kernel,
        grid=(t // tt,),
        in_specs=[pl.BlockSpec((tt, q.shape[1]), lambda i: (i, 0)), kspec, kspec],
        out_specs=[ospec, ospec, ospec],
        out_shape=[out, out, out],
        compiler_params=_cparams("parallel"),
        name="peer_route",
    )(q, k1, k2)


def _peer_w_kernel(gate_ref, e1_ref, e2_ref, w_ref):
    tb = gate_ref.shape[0]
    sub = lax.broadcasted_iota(jnp.int32, (PEER_KEYS, gate_ref.shape[1]), 0).astype(F32)

    def body(t, carry):
        g = gate_ref[pl.ds(t, 1), :]
        a = e1_ref[pl.ds(t, 1), :]
        b = e2_ref[pl.ds(t, 1), :]
        lhs = jnp.where(a == sub, g, 0.0).astype(BF16)
        rhs = jnp.where(b == sub, 1.0, 0.0).astype(BF16)
        w_ref[t] = _dot_nt(lhs, rhs)
        return carry

    lax.fori_loop(0, tb, body, 0, unroll=16)


def peer_dense_weights(gate, e1, e2):
    t, hk = gate.shape
    tb = 128
    spec = pl.BlockSpec((tb, hk), lambda i: (i, 0))
    return pl.pallas_call(
        _peer_w_kernel,
        grid=(t // tb,),
        in_specs=[spec, spec, spec],
        out_specs=pl.BlockSpec((tb, PEER_KEYS, PEER_KEYS), lambda i: (i, 0, 0)),
        out_shape=jax.ShapeDtypeStruct((t, PEER_KEYS, PEER_KEYS), F32),
        compiler_params=_cparams("parallel"),
        name="peer_dense_weights",
    )(gate, e1, e2)


PEER_KEY1_PER_STEP = 8


def _peer_ffn_kernel(x_ref, w_ref, u_ref, v_ref, r_ref, o_ref):
    e = pl.program_id(1)

    @pl.when(e == 0)
    def _():
        o_ref[...] = r_ref[...]

    a = _dot_nt(x_ref[...], u_ref[...])
    act = 0.5 * a * (1.0 + lax.erf(a * (2.0 ** -0.5)))
    hmat = jnp.concatenate(
        [(act[:, i * PEER_KEYS:(i + 1) * PEER_KEYS] * w_ref[:, i, :]).astype(BF16)
         for i in range(PEER_KEY1_PER_STEP)], axis=1)
    o_ref[...] += jnp.dot(hmat, v_ref[...], preferred_element_type=F32)


def peer_ffn_dense(xn, w, u_tab, v_tab, resid):
    t, d = xn.shape
    n_exp = u_tab.shape[0]
    tb = _pick_tile(t, (528, 512, 256, 128))
    eb = PEER_KEY1_PER_STEP * PEER_KEYS
    once = pl.Buffered(1)
    return pl.pallas_call(
        _peer_ffn_kernel,
        grid=(t // tb, n_exp // eb),
        in_specs=[pl.BlockSpec((tb, d), lambda i, e: (i, 0), pipeline_mode=once),
                  pl.BlockSpec((tb, PEER_KEY1_PER_STEP, PEER_KEYS), lambda i, e: (i, e, 0)),
                  pl.BlockSpec((eb, d), lambda i, e: (e, 0)),
                  pl.BlockSpec((eb, d), lambda i, e: (e, 0)),
                  pl.BlockSpec((tb, d), lambda i, e: (i, 0), pipeline_mode=once)],
        out_specs=pl.BlockSpec((tb, d), lambda i, e: (i, 0)),
        out_shape=jax.ShapeDtypeStruct((t, d), F32),
        compiler_params=_cparams("parallel", "arbitrary"),
        name="peer_ffn_dense",
    )(xn, w, u_tab, v_tab, resid)


def peer_layer(x, g_ffn, wq, k1, k2, u_tab, v_tab):
    q, xn = norm_matmul(x, g_ffn, wq, emit_h=True, name="peer_query")
    gate, e1, e2 = peer_route(q, k1, k2)
    w = peer_dense_weights(gate, e1, e2)
    return peer_ffn_dense(xn, w, u_tab, v_tab, x)


def _group_norm_kernel(x_ref, bd_ref, g_ref, *out_refs, inv_group, scales):
    x = x_ref[...]
    sq = x * x
    hi = sq.astype(BF16)
    lo = (sq - hi.astype(F32)).astype(BF16)
    bd = bd_ref[...]
    ss = jnp.dot(hi, bd, preferred_element_type=F32) + jnp.dot(lo, bd, preferred_element_type=F32)
    y = x * lax.rsqrt(ss * inv_group + NORM_EPS) * g_ref[...]
    for o_ref, sc in zip(out_refs, scales):
        o_ref[...] = (y if sc == 1.0 else y * sc).astype(o_ref.dtype)


def group_norm(src, col0, width, group, gain, outs):
    t = src.shape[0]
    assert col0 % width == 0 and width % group == 0
    tm = _pick_tile(t, (528, 512, 256, 128))
    lane = jnp.arange(width, dtype=jnp.int32) // group
    bd = (lane[:, None] == lane[None, :]).astype(BF16)
    g = jnp.tile(gain.astype(F32), width // group).reshape(1, width)
    ospec = pl.BlockSpec((tm, width), lambda i: (i, 0))
    res = pl.pallas_call(
        functools.partial(_group_norm_kernel, inv_group=1.0 / group, scales=tuple(s for _, s in outs)),
        grid=(t // tm,),
        in_specs=[pl.BlockSpec((tm, width), lambda i: (i, col0 // width)),
                  pl.BlockSpec((width, width), lambda i: (0, 0)),
                  pl.BlockSpec((1, width), lambda i: (0, 0))],
        out_specs=[ospec] * len(outs),
        out_shape=[jax.ShapeDtypeStruct((t, width), dt) for dt, _ in outs],
        compiler_params=_cparams("parallel"),
        name="group_norm",
    )(src, bd, g)
    return res


CONV_HALO = 8


def _conv_kernel(x_ref, halo_ref, w_ref, b_ref, o_ref, sc, *, first_tile_has_no_history):
    tm = x_ref.shape[0]
    halo = halo_ref[...]
    if first_tile_has_no_history:
        halo = jnp.where(pl.program_id(0) == 0, 0.0, halo)
    sc[0:CONV_HALO, :] = halo
    sc[CONV_HALO:, :] = x_ref[...]
    acc = b_ref[...]
    for j in range(SSM_CONV):
        acc = acc + w_ref[j:j + 1, :] * sc[pl.ds(CONV_HALO - (SSM_CONV - 1 - j), tm), :]
    o_ref[...] = jax.nn.silu(acc)


def causal_conv_silu(src, col0, width, row0, rows, tm, tc, halo_src, conv_w, conv_b):
    assert col0 % tc == 0 and width % tc == 0 and row0 % tm == 0 and rows % tm == 0 and tm % CONV_HALO == 0
    cb, rb, hb = col0 // tc, row0 // tm, tm // CONV_HALO
    if halo_src is None:
        halo_arr = src
        halo_spec = pl.BlockSpec((CONV_HALO, tc), lambda i, j: (jnp.maximum((rb + i) * hb - 1, 0), cb + j))
    else:
        halo_arr = halo_src
        halo_spec = pl.BlockSpec((CONV_HALO, tc), lambda i, j: (i, j))
    return pl.pallas_call(
        functools.partial(_conv_kernel, first_tile_has_no_history=halo_src is None),
        grid=(rows // tm, width // tc),
        in_specs=[pl.BlockSpec((tm, tc), lambda i, j: (rb + i, cb + j)),
                  halo_spec,
                  pl.BlockSpec((SSM_CONV, tc), lambda i, j: (0, j)),
                  pl.BlockSpec((1, tc), lambda i, j: (0, j))],
        out_specs=pl.BlockSpec((tm, tc), lambda i, j: (i, j)),
        out_shape=jax.ShapeDtypeStruct((rows, width), F32),
        scratch_shapes=[pltpu.VMEM((tm + CONV_HALO, tc), F32)],
        compiler_params=_cparams("parallel", "parallel"),
        name="causal_conv_silu",
    )(src, halo_arr, conv_w.astype(F32), conv_b.reshape(1, width).astype(F32))


SSM_INNER = SSM_HEADS * SSM_HEAD_DIM
SSM_GN = SSM_GROUPS * SSM_STATE
HEADS_PER_GROUP = SSM_HEADS // SSM_GROUPS
GROUP_W = HEADS_PER_GROUP * SSM_HEAD_DIM


def _ssd_kernel(xs_ref, b_ref, c_ref, z_ref, dt_ref, dtt_ref, dtb_row_ref, dtb_col_ref, a_row_ref, a_col_ref,
                e_ref, dsk_ref, g_ref, h0_ref, y_ref, hout_ref, h_sc, y_sc, *pad_scs, valid):
    q = SSM_CHUNK
    hi = lax.Precision.HIGHEST
    c = pl.program_id(1)

    @pl.when(c == 0)
    def _():
        h_sc[...] = h0_ref[...]

    def rows(ref, sc):
        if valid == q:
            return ref[...]
        sc[...] = jnp.zeros(sc.shape, F32)
        sc[0:valid, :] = ref[...]
        return sc[...]

    if valid == q:
        pad_scs = (None,) * 5
    xs = rows(xs_ref, pad_scs[0])
    bm = rows(b_ref, pad_scs[1])
    cm = rows(c_ref, pad_scs[2])
    z = rows(z_ref, pad_scs[3])
    dt_raw = rows(dt_ref, pad_scs[4])

    row_i = lax.broadcasted_iota(jnp.int32, (q, q), 0)
    col_i = lax.broadcasted_iota(jnp.int32, (q, q), 1)
    causal = row_i >= col_i
    dt = jax.nn.softplus(dt_raw + dtb_row_ref[...])
    dtt = jax.nn.softplus(dtt_ref[...] + dtb_col_ref[...])
    if valid < q:
        dt = jnp.where(row_i < valid, dt, 0.0)
        dtt = jnp.where(lax.broadcasted_iota(jnp.int32, dtt.shape, 1) < valid, dtt, 0.0)
    acs = jnp.dot(causal.astype(F32), dt * a_row_ref[...], precision=hi, preferred_element_type=F32)
    acst = jnp.dot(dtt * a_col_ref[...], (row_i <= col_i).astype(F32), precision=hi,
                   preferred_element_type=F32)
    per_head = jnp.concatenate([dt, jnp.exp(acs[q - 1:q, :] - acs), jnp.exp(acs)], axis=0)
    expand = e_ref[...]
    piece = per_head.astype(BF16)
    expanded = jnp.dot(piece, expand, preferred_element_type=F32)
    rest = per_head - piece.astype(F32)
    for _ in range(2):
        piece = rest.astype(BF16)
        expanded = expanded + jnp.dot(piece, expand, preferred_element_type=F32)
        rest = rest - piece.astype(F32)
    dt_e, dec_e, eacs_e = expanded[0:q], expanded[q:2 * q], expanded[2 * q:3 * q]
    xdt = xs * dt_e
    xdd = xdt * dec_e
    xdt_b = xdt.astype(BF16)
    lane = lax.broadcasted_iota(jnp.int32, (q, 2 * SSM_HEAD_DIM), 1)

    for g in range(SSM_GROUPS):
        bg = bm[:, g * SSM_STATE:(g + 1) * SSM_STATE].astype(BF16)
        cg = cm[:, g * SSM_STATE:(g + 1) * SSM_STATE].astype(BF16)
        cb = _dot_nt(cg, bg)
        for pair in range(HEADS_PER_GROUP // 2):
            h_a = g * HEADS_PER_GROUP + 2 * pair
            slab = slice(h_a * SSM_HEAD_DIM, (h_a + 2) * SSM_HEAD_DIM)
            xpair = xdt_b[:, slab]
            ypair = jnp.zeros((q, 2 * SSM_HEAD_DIM), F32)
            for which in range(2):
                h = h_a + which
                seg = acs[:, h:h + 1] - acst[h:h + 1, :]
                lmat = jnp.exp(jnp.where(causal, seg, -jnp.inf))
                mine = (lane >= which * SSM_HEAD_DIM) & (lane < (which + 1) * SSM_HEAD_DIM)
                ypair = ypair + jnp.dot((cb * lmat).astype(BF16), jnp.where(mine, xpair, jnp.zeros_like(xpair)),
                                        preferred_element_type=F32)
            y_sc[:, slab] = ypair
        gs = slice(g * GROUP_W, (g + 1) * GROUP_W)
        y_sc[:, gs] += _dot_nt(cg, h_sc[gs, :].astype(BF16)) * eacs_e[:, gs]
        st = jnp.dot(xdd[:, gs].T.astype(BF16), bg, preferred_element_type=F32)
        for hh in range(HEADS_PER_GROUP):
            h = g * HEADS_PER_GROUP + hh
            hs = slice(h * SSM_HEAD_DIM, (h + 1) * SSM_HEAD_DIM)
            dec = jnp.exp(acst[h:h + 1, q - 1:q])
            h_sc[hs, :] = h_sc[hs, :] * dec + st[hh * SSM_HEAD_DIM:(hh + 1) * SSM_HEAD_DIM, :]

    y = y_sc[...] + dsk_ref[...] * xs
    y = y * jax.nn.silu(z)
    gain = g_ref[...]
    for g in range(SSM_GROUPS):
        gs = slice(g * GROUP_W, (g + 1) * GROUP_W)
        yg = y[:, gs]
        yn = yg * lax.rsqrt(jnp.mean(yg * yg, axis=-1, keepdims=True) + NORM_EPS) * gain[:, gs]
        y_ref[:, gs] = yn[0:valid].astype(y_ref.dtype)

    @pl.when(c == pl.num_programs(1) - 1)
    def _():
        hout_ref[...] = h_sc[...]


def ssd_branch(xc, u, z_col0, u_row0, dt_pad, dt_t, h0, n_batch, n_chunks, valid, dt_bias, a_log, d_skip, g_ssm,
               out_dtype):
    q = SSM_CHUNK
    rows = xc.shape[0]
    assert rows == n_batch * n_chunks * valid and u_row0 % valid == 0 and z_col0 % SSM_INNER == 0
    rb = u_row0 // valid
    step = lambda b, c: b * n_chunks + c
    pad128 = lambda v: jnp.pad(v.astype(F32), (0, 128 - SSM_HEADS))
    a = -jnp.exp(a_log.astype(F32))
    head_lane = jnp.arange(SSM_INNER, dtype=jnp.int32) // SSM_HEAD_DIM
    expand = (jnp.arange(128, dtype=jnp.int32)[:, None] == head_lane[None, :]).astype(BF16)
    const = lambda shape: pl.BlockSpec(shape, lambda b, c: (0,) * len(shape))
    pad_scs = [] if valid == q else [pltpu.VMEM((q, w), F32) for w in (SSM_INNER, SSM_GN, SSM_GN, SSM_INNER, 128)]
    y, h_fin = pl.pallas_call(
        functools.partial(_ssd_kernel, valid=valid),
        grid=(n_batch, n_chunks),
        in_specs=[
            pl.BlockSpec((valid, SSM_INNER), lambda b, c: (step(b, c), 0)),
            pl.BlockSpec((valid, SSM_GN), lambda b, c: (step(b, c), SSM_INNER // SSM_GN)),
            pl.BlockSpec((valid, SSM_GN), lambda b, c: (step(b, c), SSM_INNER // SSM_GN + 1)),
            pl.BlockSpec((valid, SSM_INNER), lambda b, c: (rb + step(b, c), z_col0 // SSM_INNER)),
            pl.BlockSpec((valid, 128), lambda b, c: (rb + step(b, c), 0)),
            pl.BlockSpec((None, SSM_HEADS, q), lambda b, c: (step(b, c), 0, 0)),
            const((1, 128)), const((SSM_HEADS, q)), const((1, 128)), const((SSM_HEADS, q)),
            const((128, SSM_INNER)), const((1, SSM_INNER)), const((1, SSM_INNER)),
            pl.BlockSpec((None, SSM_INNER, SSM_STATE), lambda b, c: (b, 0, 0)),
        ],
        out_specs=[pl.BlockSpec((valid, SSM_INNER), lambda b, c: (step(b, c), 0)),
                   pl.BlockSpec((None, SSM_INNER, SSM_STATE), lambda b, c: (b, 0, 0))],
        out_shape=[jax.ShapeDtypeStruct((rows, SSM_INNER), out_dtype),
                   jax.ShapeDtypeStruct((n_batch, SSM_INNER, SSM_STATE), F32)],
        scratch_shapes=[pltpu.VMEM((SSM_INNER, SSM_STATE), F32), pltpu.VMEM((q, SSM_INNER), F32)] + pad_scs,
        compiler_params=_cparams("parallel", "arbitrary"),
        name="ssd_branch",
    )(xc, xc, xc, u, dt_pad, dt_t,
      pad128(dt_bias).reshape(1, 128), jnp.broadcast_to(dt_bias.astype(F32)[:, None], (SSM_HEADS, q)),
      pad128(a).reshape(1, 128), jnp.broadcast_to(a[:, None], (SSM_HEADS, q)),
      expand, jnp.repeat(d_skip.astype(F32), SSM_HEAD_DIM).reshape(1, SSM_INNER),
      g_ssm.astype(F32).reshape(1, SSM_INNER), h0)
    return y, h_fin


def _mem_attn_kernel(q_ref, k_ref, v_ref, o_ref):
    for h in range(MEM_HEADS):
        hs = slice(h * MEM_HEAD_DIM, (h + 1) * MEM_HEAD_DIM)
        s = _dot_nt(q_ref[:, hs].astype(BF16), k_ref[:, hs].astype(BF16))
        p = jnp.exp(s - jnp.max(s, axis=-1, keepdims=True))
        o = jnp.dot(p.astype(BF16), v_ref[:, hs].astype(BF16), preferred_element_type=F32)
        o_ref[:, hs] = (o / jnp.sum(p, axis=-1, keepdims=True)).astype(o_ref.dtype)


def memory_attention(qn, q_row0, rows, tq, mk, mv, v_col_blk, tiles_per_batch, out_dtype):
    n_mem = mk.shape[0] * tq * tiles_per_batch // rows
    rb = q_row0 // tq
    return pl.pallas_call(
        _mem_attn_kernel,
        grid=(rows // tq,),
        in_specs=[pl.BlockSpec((tq, MEM_W), lambda i: (rb + i, 0)),
                  pl.BlockSpec((n_mem, MEM_W), lambda i: (i // tiles_per_batch, 0)),
                  pl.BlockSpec((n_mem, MEM_W), lambda i: (i // tiles_per_batch, v_col_blk))],
        out_specs=pl.BlockSpec((tq, MEM_W), lambda i: (i, 0)),
        out_shape=jax.ShapeDtypeStruct((rows, MEM_W), out_dtype),
        compiler_params=_cparams("parallel"),
        name="memory_attention",
    )(qn, mk, mv)


SAMPLE_PAGES_PER_STEP = 16
SAMPLE_ROWS = DA_HEADS * 2 * 8


def _sample_attn_kernel(pt_ref, lam_ref, q_ref, *refs, pages_per_step, post_scale):
    n = pages_per_step
    kt_refs, v_refs = refs[:n], refs[n:2 * n]
    knew_ref, vnew_ref, blast_ref, bnew_ref, rexp_ref, hmask_ref, g_ref, o_ref, m_sc, l_sc, acc_sc = refs[2 * n:]
    del pt_ref
    step = pl.program_id(1)
    n_new = o_ref.shape[0]
    rows_per_head = 2 * n_new

    @pl.when(step == 0)
    def _():
        m_sc[...] = jnp.full(m_sc.shape, NEG_BIG, F32)
        l_sc[...] = jnp.zeros(l_sc.shape, F32)
        acc_sc[...] = jnp.zeros(acc_sc.shape, F32)

    q = q_ref[...]

    def update(pages):
        logits = []
        for kt_ref, _, bias in pages:
            s = jnp.dot(q, kt_ref[...].astype(BF16), preferred_element_type=F32)
            logits.append(s if bias is None else s + bias)
        s_max = logits[0]
        for s in logits[1:]:
            s_max = jnp.maximum(s_max, s)
        m_prev = m_sc[...]
        m_new = jnp.maximum(m_prev, jnp.max(s_max, axis=-1, keepdims=True))
        alpha = jnp.exp2(m_prev - m_new)
        m_sc[...] = m_new
        l_new = alpha * l_sc[...]
        acc = alpha * acc_sc[...]
        rexp = rexp_ref[...]
        own_head = hmask_ref[...]
        for (_, v_ref, _), s in zip(pages, logits):
            p = jnp.exp2(s - m_new)
            l_new = l_new + p
            p_rep = jnp.dot(p.astype(BF16), rexp, preferred_element_type=F32)
            acc = acc + jnp.dot(p_rep.astype(BF16) * own_head, v_ref[...].astype(BF16),
                                preferred_element_type=F32)
        l_sc[...] = l_new
        acc_sc[...] = acc

    last = pl.num_programs(1) - 1
    past = [(kt_refs[i], v_refs[i], None) for i in range(n)]

    @pl.when(step < last)
    def _():
        update(past)

    @pl.when(step == last)
    def _():
        update(past[:-1] + [(kt_refs[n - 1], v_refs[n - 1], blast_ref[...]), (knew_ref, vnew_ref, bnew_ref[...])])
        attn = acc_sc[...] / jnp.sum(l_sc[...], axis=-1, keepdims=True)
        for h in range(DA_HEADS):
            r0 = h * rows_per_head
            o = attn[r0:r0 + n_new, :] - lam_ref[0] * attn[r0 + n_new:r0 + 2 * n_new, :]
            y = o * lax.rsqrt(jnp.mean(o * o, axis=-1, keepdims=True) + NORM_EPS)
            o_ref[:, h * DA_DV:(h + 1) * DA_DV] = (y * g_ref[...]) * post_scale


def sample_bias_tiles(table, n_new, page):
    def rows(off):
        b = _toeplitz_bias(table, off, page)[:, :n_new, :]
        return jnp.broadcast_to(b[:, None], (DA_HEADS, 2, n_new, page)).reshape(SAMPLE_ROWS, page)
    return rows(page), rows(0)


def sample_diff_attention(layer, qn_s, kn_s, v_s, lam, k_pool_t, v_pool, page_table, b_last, b_new, g_sub,
                          post_scale):
    bsz, n_new, qw = qn_s.shape
    page = v_pool.shape[2]
    n_pages = page_table.shape[1]
    n = SAMPLE_PAGES_PER_STEP
    assert n_pages % n == 0 and SAMPLE_ROWS == DA_HEADS * 2 * n_new
    qg = qn_s.reshape(bsz, n_new, DA_HEADS * 2, DA_DK)
    eye = jnp.eye(DA_HEADS * 2, dtype=F32)
    q_rows = (qg.transpose(0, 2, 1, 3)[:, :, :, None, :] * eye[None, :, None, :, None]).reshape(
        bsz, SAMPLE_ROWS, qw).astype(BF16)
    knew_t = jnp.pad(kn_s.transpose(0, 2, 1), ((0, 0), (0, 0), (0, page - n_new)))
    vnew = jnp.pad(v_s, ((0, 0), (0, page - n_new), (0, 0), (0, 0))).reshape(bsz, page * DA_HEADS, DA_DV)
    v_rows = v_pool.reshape(v_pool.shape[0], v_pool.shape[1], page * DA_HEADS, DA_DV)
    col = jnp.arange(page * DA_HEADS, dtype=jnp.int32)
    rexp = (col[None, :] // DA_HEADS == jnp.arange(page, dtype=jnp.int32)[:, None]).astype(BF16)
    hmask = (col[None, :] % DA_HEADS
             == jnp.arange(SAMPLE_ROWS, dtype=jnp.int32)[:, None] // (2 * n_new)).astype(BF16)
    kspec = lambda i: pl.BlockSpec((None, None, qw, page), lambda b, s, pt: (layer, pt[b, s * n + i], 0, 0))
    vspec = lambda i: pl.BlockSpec((None, None, page * DA_HEADS, DA_DV),
                                   lambda b, s, pt: (layer, pt[b, s * n + i], 0, 0))
    const2 = lambda shape: pl.BlockSpec(shape, lambda b, s, pt: (0, 0))
    grid_spec = pltpu.PrefetchScalarGridSpec(
        num_scalar_prefetch=1,
        grid=(bsz, n_pages // n),
        in_specs=[pl.BlockSpec(memory_space=pltpu.SMEM),
                  pl.BlockSpec((None, SAMPLE_ROWS, qw), lambda b, s, pt: (b, 0, 0))]
                 + [kspec(i) for i in range(n)] + [vspec(i) for i in range(n)]
                 + [pl.BlockSpec((None, qw, page), lambda b, s, pt: (b, 0, 0)),
                    pl.BlockSpec((None, page * DA_HEADS, DA_DV), lambda b, s, pt: (b, 0, 0)),
                    const2((SAMPLE_ROWS, page)), const2((SAMPLE_ROWS, page)),
                    const2((page, page * DA_HEADS)), const2((SAMPLE_ROWS, page * DA_HEADS)), const2((1, DA_DV))],
        out_specs=pl.BlockSpec((n_new, DA_VW), lambda b, s, pt: (b, 0)),
        scratch_shapes=[pltpu.VMEM((SAMPLE_ROWS, page), F32), pltpu.VMEM((SAMPLE_ROWS, page), F32),
                        pltpu.VMEM((SAMPLE_ROWS, DA_DV), F32)],
    )
    return pl.pallas_call(
        functools.partial(_sample_attn_kernel, pages_per_step=n, post_scale=post_scale),
        grid_spec=grid_spec,
        out_shape=jax.ShapeDtypeStruct((bsz * n_new, DA_VW), F32),
        compiler_params=_cparams("parallel", "arbitrary"),
        name="sample_diff_attention",
    )(page_table, lam.reshape(1).astype(F32), q_rows, *([k_pool_t] * n), *([v_rows] * n), knew_t, vnew,
      b_last, b_new, rexp, hmask, g_sub.reshape(1, DA_DV).astype(F32))


FA_TILE = 512


def kernel(x_prompt, x_sample, cache_attn_k, cache_attn_v, cache_mem_k, cache_mem_v, state_ssm, state_conv, page_table, mem_prompt, rel_bias, g_mix, w_in, g_q, g_k, lam_q1, lam_k1, lam_q2, lam_k2, g_sub, conv_w, conv_b, dt_bias, a_log, d_skip, g_ssm, g_mem, w_mem_kv, g_mq, g_mk, w_br_attn, w_br_ssm, w_br_mem, w_out, g_ffn, peer_wq, peer_k1, peer_k2, peer_u, peer_v):
    depth = w_in.shape[0]
    bp, sp, d = x_prompt.shape
    bs, ts, _ = x_sample.shape
    assert bp == 1 and sp % SSM_CHUNK == 0 and sp % FA_TILE == 0 and ts >= SSM_CONV - 1
    n_p = bp * sp
    n_s = bs * ts
    conv_dim = SSM_INNER + 2 * SSM_GN
    n_mem = mem_prompt.shape[1]
    n_pool, page = cache_attn_v.shape[1], cache_attn_v.shape[2]
    s_z = 2 * DA_QW + DA_VW
    s_dt = s_z + SSM_INNER + conv_dim
    s_mq = s_dt + SSM_HEADS
    c_z, c_xbc = 0, SSM_INNER
    c_q = c_xbc + conv_dim
    c_k = c_q + DA_QW
    c_v = c_k + DA_QW
    c_mq = c_v + DA_VW
    c_gate = c_mq + MEM_W

    x = jnp.concatenate([x_prompt.reshape(n_p, d), x_sample.reshape(n_s, d)], axis=0)
    bias_tiles = prompt_bias_tiles(rel_bias, FA_TILE)
    b_last, b_new = sample_bias_tiles(rel_bias, ts, page)
    k_pool_t = jnp.transpose(cache_attn_k, (0, 1, 3, 4, 5, 2)).reshape(depth, n_pool, DA_QW, page)
    zero_h = jnp.zeros((bp, SSM_INNER, SSM_STATE), F32)

    outs = {k: [] for k in ("pk", "pv", "pmk", "pmv", "ph", "pc", "sk", "sv", "sh", "sc")}
    for l in range(depth):
        lam_init = 0.8 - 0.6 * math.exp(-0.3 * l)
        lam = (jnp.exp(jnp.sum(lam_q1[l].astype(F32) * lam_k1[l].astype(F32)))
               - jnp.exp(jnp.sum(lam_q2[l].astype(F32) * lam_k2[l].astype(F32))) + lam_init)
        wl = w_in[l]
        w_main = jnp.concatenate([wl[:, s_z:s_dt], wl[:, :s_z], wl[:, s_mq:]], axis=1).astype(BF16)
        w_dt = jnp.pad(wl[:, s_dt:s_mq], ((0, 0), (0, 128 - SSM_HEADS))).astype(BF16)

        u, dt_pad = norm_matmul(x, g_mix[l], w_main, side_w=w_dt, name="in_proj")
        (qn,) = group_norm(u, c_q, DA_QW, DA_DK, g_q[l], [(BF16, DA_SCALE * LOG2E)])
        kn, kn_b = group_norm(u, c_k, DA_QW, DA_DK, g_k[l], [(F32, 1.0), (BF16, 1.0)])
        (mqn,) = group_norm(u, c_mq, MEM_W, MEM_HEAD_DIM, g_mq[l], [(F32, MEM_SCALE)])
        v_all = u[:, c_v:c_v + DA_VW]
        xbc_s = u[n_p:, c_xbc:c_xbc + conv_dim].reshape(bs, ts, conv_dim)

        a_p = prompt_diff_attention(qn, kn_b, u, c_v, n_p, lam, bias_tiles, g_sub[l], 1.0 - lam_init, FA_TILE)
        a_s = sample_diff_attention(l, qn[n_p:].astype(F32).reshape(bs, ts, DA_QW), kn[n_p:].reshape(bs, ts, DA_QW),
                                    v_all[n_p:].reshape(bs, ts, DA_HEADS, DA_DV), lam, k_pool_t, cache_attn_v,
                                    page_table, b_last, b_new, g_sub[l], 1.0 - lam_init)

        (mkv,) = norm_matmul(mem_prompt.reshape(bp * n_mem, d), g_mem[l], w_mem_kv[l].astype(BF16), name="mem_kv")
        (mk_p,) = group_norm(mkv, 0, MEM_W, MEM_HEAD_DIM, g_mk[l], [(F32, 1.0)])
        tq = 512
        m_p = memory_attention(mqn, 0, n_p, tq, mk_p, mkv, 1, sp // tq, BF16)
        m_s = memory_attention(mqn, n_p, n_s, ts, cache_mem_k[l].reshape(bs * n_mem, MEM_W),
                               cache_mem_v[l].reshape(bs * n_mem, MEM_W), 0, 1, F32)

        xc_p = causal_conv_silu(u, c_xbc, conv_dim, 0, n_p, 512, 1024, None, conv_w[l], conv_b[l])
        halo_s = jnp.pad(state_conv[l], ((0, 0), (CONV_HALO - (SSM_CONV - 1), 0), (0, 0))).reshape(
            bs * CONV_HALO, conv_dim)
        xc_s = causal_conv_silu(xbc_s.reshape(n_s, conv_dim), 0, conv_dim, 0, n_s, ts, conv_dim, halo_s,
                                conv_w[l], conv_b[l])
        dt_raw = dt_pad[:, :SSM_HEADS]
        dtt_p = dt_raw[:n_p].reshape(n_p // SSM_CHUNK, SSM_CHUNK, SSM_HEADS).transpose(0, 2, 1)
        dtt_s = jnp.pad(dt_raw[n_p:].reshape(bs, ts, SSM_HEADS).transpose(0, 2, 1),
                        ((0, 0), (0, 0), (0, SSM_CHUNK - ts)))
        s_p, h_p = ssd_branch(xc_p, u, c_z, 0, dt_pad, dtt_p, zero_h, bp, sp // SSM_CHUNK, SSM_CHUNK,
                              dt_bias[l], a_log[l], d_skip[l], g_ssm[l], BF16)
        s_s, h_s = ssd_branch(xc_s, u, c_z, n_p, dt_pad, dtt_s, state_ssm[l].reshape(bs, SSM_INNER, SSM_STATE),
                              bs, 1, ts, dt_bias[l], a_log[l], d_skip[l], g_ssm[l], F32)

        a_out = jnp.concatenate([a_p, a_s.astype(BF16)], axis=0)
        s_out = jnp.concatenate([s_p, s_s.astype(BF16)], axis=0)
        m_out = jnp.concatenate([m_p, m_s.astype(BF16)], axis=0)
        merged = merge_branches(a_out, s_out, m_out, u, c_gate, w_br_attn[l].astype(BF16),
                                w_br_ssm[l].astype(BF16), w_br_mem[l].astype(BF16))
        x = matmul(merged, w_out[l].astype(BF16), residual=x, name="out_proj")
        x = peer_layer(x, g_ffn[l], peer_wq[l].astype(BF16), peer_k1[l].astype(BF16), peer_k2[l].astype(BF16),
                       peer_u[l].astype(BF16), peer_v[l].astype(BF16))

        outs["pk"].append(kn[:n_p].reshape(bp, sp, DA_HEADS, 2, DA_DK))
        outs["pv"].append(v_all[:n_p].reshape(bp, sp, DA_HEADS, DA_DV))
        outs["pmk"].append(mk_p.reshape(bp, n_mem, MEM_HEADS, MEM_HEAD_DIM))
        outs["pmv"].append(mkv[:, MEM_W:].reshape(bp, n_mem, MEM_HEADS, MEM_HEAD_DIM))
        outs["ph"].append(h_p.reshape(bp, SSM_HEADS, SSM_HEAD_DIM, SSM_STATE))
        outs["pc"].append(u[n_p - (SSM_CONV - 1):n_p, c_xbc:c_xbc + conv_dim].reshape(bp, SSM_CONV - 1, conv_dim))
        outs["sk"].append(kn[n_p:].reshape(bs, ts, DA_HEADS, 2, DA_DK))
        outs["sv"].append(v_all[n_p:].reshape(bs, ts, DA_HEADS, DA_DV))
        outs["sh"].append(h_s.reshape(bs, SSM_HEADS, SSM_HEAD_DIM, SSM_STATE))
        outs["sc"].append(xbc_s[:, ts - (SSM_CONV - 1):])

    y_prompt = x[:n_p].reshape(bp, sp, d)
    y_sample = x[n_p:].reshape(bs, ts, d)
    st = lambda name: jnp.stack(outs[name])
    return (y_prompt, y_sample, st("pk"), st("pv"), st("pmk"), st("pmv"), st("ph"), st("pc"),
            st("sk"), st("sv"), st("sh"), st("sc"))
```

```python
import functools
import math

import jax
import jax.numpy as jnp
from jax import lax
from jax.experimental import pallas as pl
from jax.experimental.pallas import tpu as pltpu

F32 = jnp.float32
BF16 = jnp.bfloat16

NORM_EPS = 1e-6
NEG_BIG = -1e30
V7X_VMEM_LIMIT_BYTES = 56 * 1024 * 1024

DA_HEADS, DA_DK, DA_DV = 8, 64, 128
DA_QW = DA_HEADS * 2 * DA_DK
DA_VW = DA_HEADS * DA_DV
DA_SCALE = DA_DK ** -0.5
N_BUCKETS, MAX_DISTANCE = 32, 128
SSM_HEADS, SSM_HEAD_DIM, SSM_GROUPS, SSM_STATE, SSM_CONV, SSM_CHUNK = 32, 64, 4, 128, 4, 128
MEM_HEADS, MEM_HEAD_DIM = 4, 256
MEM_W = MEM_HEADS * MEM_HEAD_DIM
MEM_SCALE = MEM_HEAD_DIM ** -0.5
PEER_HEADS, PEER_KEYS, PEER_TOPK, PEER_HALF = 8, 128, 16, 128


def _cparams(*sem):
    return pltpu.CompilerParams(dimension_semantics=sem, vmem_limit_bytes=V7X_VMEM_LIMIT_BYTES)


def _dot_nt(a, b):
    return lax.dot_general(a, b, (((1,), (1,)), ((), ())), preferred_element_type=F32)


def _pick_tile(n, candidates):
    for c in candidates:
        if n % c == 0:
            return c
    return n


MXU_ROW_TILES = (1056, 1024, 512, 256, 128)
ONCE = pl.Buffered(1)


def _norm_mm_kernel(x_ref, g_ref, w_ref, *rest, has_side, emit_h):
    rest = list(rest)
    side_w_ref = rest.pop(0) if has_side else None
    o_ref = rest.pop(0)
    side_o_ref = rest.pop(0) if has_side else None
    h_o_ref = rest.pop(0) if emit_h else None
    (h_sc,) = rest

    @pl.when(pl.program_id(1) == 0)
    def _():
        x = x_ref[...]
        ms = jnp.mean(x * x, axis=-1, keepdims=True)
        h = (x * lax.rsqrt(ms + NORM_EPS) * g_ref[...]).astype(BF16)
        h_sc[...] = h
        if has_side:
            side_o_ref[...] = jnp.dot(h, side_w_ref[...], preferred_element_type=F32)
        if emit_h:
            h_o_ref[...] = h

    o_ref[...] = jnp.dot(h_sc[...], w_ref[...], preferred_element_type=F32)


def norm_matmul(x, g, w, side_w=None, emit_h=False, name="norm_matmul"):
    t, d = x.shape
    n = w.shape[1]
    tm = _pick_tile(t, MXU_ROW_TILES)
    tn = _pick_tile(n, (1536, 1024, 512, 256, 128))
    row = lambda width: pl.BlockSpec((tm, width), lambda i, j: (i, 0))
    in_specs = [pl.BlockSpec((tm, d), lambda i, j: (i, 0), pipeline_mode=ONCE),
                pl.BlockSpec((1, d), lambda i, j: (0, 0)), pl.BlockSpec((d, tn), lambda i, j: (0, j))]
    args = [x, g.reshape(1, d).astype(F32), w]
    out_specs = [pl.BlockSpec((tm, tn), lambda i, j: (i, j))]
    out_shape = [jax.ShapeDtypeStruct((t, n), F32)]
    if side_w is not None:
        ns = side_w.shape[1]
        in_specs.append(pl.BlockSpec((d, ns), lambda i, j: (0, 0)))
        args.append(side_w)
        out_specs.append(row(ns))
        out_shape.append(jax.ShapeDtypeStruct((t, ns), F32))
    if emit_h:
        out_specs.append(row(d))
        out_shape.append(jax.ShapeDtypeStruct((t, d), BF16))
    return pl.pallas_call(
        functools.partial(_norm_mm_kernel, has_side=side_w is not None, emit_h=emit_h),
        grid=(t // tm, n // tn),
        in_specs=in_specs,
        out_specs=out_specs,
        out_shape=out_shape,
        scratch_shapes=[pltpu.VMEM((tm, d), BF16)],
        compiler_params=_cparams("parallel", "arbitrary"),
        name=name,
    )(*args)


def _mm_kernel(a_ref, w_ref, o_ref):
    o_ref[...] = jnp.dot(a_ref[...], w_ref[...], preferred_element_type=F32).astype(o_ref.dtype)


def _mm_res_kernel(a_ref, w_ref, r_ref, o_ref):
    o_ref[...] = r_ref[...] + jnp.dot(a_ref[...], w_ref[...], preferred_element_type=F32)


def matmul(a, w, residual=None, out_dtype=F32, name="matmul"):
    t, k = a.shape
    n = w.shape[1]
    tm = _pick_tile(t, MXU_ROW_TILES)
    tn = _pick_tile(n, (1024, 512, 256, 128))
    in_specs = [pl.BlockSpec((tm, k), lambda i, j: (i, 0)), pl.BlockSpec((k, tn), lambda i, j: (0, j))]
    args = [a, w]
    body = _mm_kernel
    if residual is not None:
        in_specs.append(pl.BlockSpec((tm, tn), lambda i, j: (i, j)))
        args.append(residual)
        body = _mm_res_kernel
    return pl.pallas_call(
        body,
        grid=(t // tm, n // tn),
        in_specs=in_specs,
        out_specs=pl.BlockSpec((tm, tn), lambda i, j: (i, j)),
        out_shape=jax.ShapeDtypeStruct((t, n), out_dtype),
        compiler_params=_cparams("parallel", "parallel"),
        name=name,
    )(*args)


def _merge_kernel(a_ref, s_ref, m_ref, ga_ref, gs_ref, gm_ref, wa_ref, ws_ref, wm_ref, o_ref):
    acc = jax.nn.sigmoid(ga_ref[...]) * jnp.dot(a_ref[...], wa_ref[...], preferred_element_type=F32)
    acc += jax.nn.sigmoid(gs_ref[...]) * jnp.dot(s_ref[...], ws_ref[...], preferred_element_type=F32)
    acc += jax.nn.sigmoid(gm_ref[...]) * jnp.dot(m_ref[...], wm_ref[...], preferred_element_type=F32)
    o_ref[...] = acc.astype(o_ref.dtype)


def merge_branches(a, s, m, u, gate_col0, wa, ws, wm):
    t = a.shape[0]
    d = wa.shape[1]
    tm = _pick_tile(t, MXU_ROW_TILES)
    tn = 512
    gb = gate_col0 // tn
    nd = d // tn
    row = lambda width: pl.BlockSpec((tm, width), lambda i, j: (i, 0))
    gate = lambda which: pl.BlockSpec((tm, tn), lambda i, j: (i, gb + which * nd + j))
    wcol = lambda kdim: pl.BlockSpec((kdim, tn), lambda i, j: (0, j))
    return pl.pallas_call(
        _merge_kernel,
        grid=(t // tm, nd),
        in_specs=[row(a.shape[1]), row(s.shape[1]), row(m.shape[1]), gate(0), gate(1), gate(2),
                  wcol(wa.shape[0]), wcol(ws.shape[0]), wcol(wm.shape[0])],
        out_specs=pl.BlockSpec((tm, tn), lambda i, j: (i, j)),
        out_shape=jax.ShapeDtypeStruct((t, d), BF16),
        compiler_params=_cparams("parallel", "parallel"),
        name="merge_branches",
    )(a, s, m, u, u, u, wa, ws, wm)


def _t5_bucket(dist):
    n = jnp.maximum(dist, 0)
    exact = N_BUCKETS // 2
    nf = jnp.maximum(n, exact).astype(F32)
    large = exact + (jnp.log(nf / exact) / math.log(MAX_DISTANCE / exact) * (N_BUCKETS - exact)).astype(jnp.int32)
    return jnp.where(n < exact, n, jnp.minimum(large, N_BUCKETS - 1))


def _far_bias(table):
    return table.astype(F32)[N_BUCKETS - 1]


LOG2E = math.log2(math.e)


def _toeplitz_bias(table, off, t):
    x = jnp.arange(2 * t, dtype=jnp.int32)
    dist = jnp.where(x < t, off - x, off + 2 * t - x)
    far = _far_bias(table)
    v = (jnp.moveaxis(table.astype(F32)[_t5_bucket(dist)], -1, 0) - far[:, None]) * LOG2E
    v = jnp.where(dist[None, :] >= 0, v, NEG_BIG)
    h = v.shape[0]
    return jnp.tile(v, (1, t))[:, :t * (2 * t - 1)].reshape(h, t, 2 * t - 1)[:, :, :t]


def prompt_bias_tiles(table, tile):
    assert tile >= MAX_DISTANCE
    return jnp.stack([_toeplitz_bias(table, 0, tile), _toeplitz_bias(table, tile, tile)], axis=1)


FA_HEADS_PER_STEP = 2


def _fa_kernel(qt_ref, kt_ref, lam_ref, q_ref, k_ref, v_ref, b_ref, g_ref, o_ref, m_sc, l_sc, acc_sc, *,
               post_scale):
    s_id = pl.program_id(1)
    qi = qt_ref[s_id]
    ki = kt_ref[s_id]

    @pl.when(ki == 0)
    def _():
        m_sc[...] = jnp.full(m_sc.shape, NEG_BIG, F32)
        l_sc[...] = jnp.zeros(l_sc.shape, F32)
        acc_sc[...] = jnp.zeros(acc_sc.shape, F32)

    def update(bias_idx):
        n_lane_tiles = k_ref.shape[0] // 128
        lane = lax.broadcasted_iota(jnp.int32, (q_ref.shape[0], 2 * DA_DK), 1)
        for hh in range(FA_HEADS_PER_STEP):
            hs = slice(hh * 2 * DA_DK, (hh + 1) * 2 * DA_DK)
            q = q_ref[:, hs]
            k = k_ref[:, hs]
            v = v_ref[:, hh * DA_DV:(hh + 1) * DA_DV].astype(BF16)
            logits = []
            for m in range(2):
                in_map = (lane >= m * DA_DK) & (lane < (m + 1) * DA_DK)
                s = _dot_nt(jnp.where(in_map, q, jnp.zeros_like(q)), k)
                logits.append(s if bias_idx is None else s + b_ref[hh, bias_idx])
            for m in range(2):
                s = logits[m]
                slot = 2 * hh + m
                m_prev = m_sc[slot]
                m_new = jnp.maximum(m_prev, jnp.max(s, axis=-1, keepdims=True))
                alpha = jnp.exp2(m_prev - m_new)
                p = jnp.exp2(s - jnp.tile(m_new, (1, n_lane_tiles)))
                l_part = p[:, 0:128]
                for t in range(1, n_lane_tiles):
                    l_part = l_part + p[:, t * 128:(t + 1) * 128]
                l_sc[slot] = alpha * l_sc[slot] + l_part
                acc_sc[slot] = alpha * acc_sc[slot] + jnp.dot(p.astype(BF16), v, preferred_element_type=F32)
                m_sc[slot] = m_new

    @pl.when(ki < qi - 1)
    def _():
        update(None)

    @pl.when(ki == qi - 1)
    def _():
        update(1)

    @pl.when(ki == qi)
    def _():
        update(0)
        for hh in range(FA_HEADS_PER_STEP):
            l0 = jnp.sum(l_sc[2 * hh], axis=-1, keepdims=True)
            l1 = jnp.sum(l_sc[2 * hh + 1], axis=-1, keepdims=True)
            o = acc_sc[2 * hh] / l0 - lam_ref[0] * (acc_sc[2 * hh + 1] / l1)
            y = o * lax.rsqrt(jnp.mean(o * o, axis=-1, keepdims=True) + NORM_EPS)
            o_ref[:, hh * DA_DV:(hh + 1) * DA_DV] = ((y * g_ref[...]) * post_scale).astype(o_ref.dtype)


def prompt_diff_attention(qn, kn, v_src, v_col0, s_len, lam, bias_tiles, g_sub, post_scale, tile):
    hp = FA_HEADS_PER_STEP
    width = hp * DA_DV
    assert v_col0 % width == 0 and DA_HEADS % hp == 0 and 2 * DA_DK == DA_DV
    vb = v_col0 // width
    nq = s_len // tile
    pairs = [(qi, ki) for qi in range(nq) for ki in range(qi + 1)]
    qt = jnp.asarray([p[0] for p in pairs], jnp.int32)
    kt = jnp.asarray([p[1] for p in pairs], jnp.int32)
    grid_spec = pltpu.PrefetchScalarGridSpec(
        num_scalar_prefetch=2,
        grid=(DA_HEADS // hp, len(pairs)),
        in_specs=[
            pl.BlockSpec(memory_space=pltpu.SMEM),
            pl.BlockSpec((tile, width), lambda h, s, qt, kt: (qt[s], h)),
            pl.BlockSpec((tile, width), lambda h, s, qt, kt: (kt[s], h)),
            pl.BlockSpec((tile, width), lambda h, s, qt, kt: (kt[s], vb + h)),
            pl.BlockSpec((hp, 2, tile, tile), lambda h, s, qt, kt: (h, 0, 0, 0)),
            pl.BlockSpec((1, DA_DV), lambda h, s, qt, kt: (0, 0)),
        ],
        out_specs=pl.BlockSpec((tile, width), lambda h, s, qt, kt: (qt[s], h)),
        scratch_shapes=[pltpu.VMEM((2 * hp, tile, 128), F32), pltpu.VMEM((2 * hp, tile, 128), F32),
                        pltpu.VMEM((2 * hp, tile, DA_DV), F32)],
    )
    return pl.pallas_call(
        functools.partial(_fa_kernel, post_scale=post_scale),
        grid_spec=grid_spec,
        out_shape=jax.ShapeDtypeStruct((s_len, DA_VW), BF16),
        compiler_params=_cparams("parallel", "arbitrary"),
        name="prompt_diff_attention",
    )(qt, kt, lam.reshape(1).astype(F32), qn, kn, v_src, bias_tiles, g_sub.reshape(1, DA_DV).astype(F32))


def _topk_rows(s, k):
    riota = lax.broadcasted_iota(jnp.int32, s.shape, 0).astype(F32)
    big = float(s.shape[0])
    vals, idxs = [], []
    for _ in range(k):
        m = jnp.max(s, axis=0, keepdims=True)
        idx = jnp.min(jnp.where(s == m, riota, big), axis=0, keepdims=True)
        vals.append(m)
        idxs.append(idx)
        s = jnp.where(riota == idx, -jnp.inf, s)
    return vals, idxs


def _pruned_pair_candidates(v1, i1, v2, i2):
    k = PEER_TOPK
    tt = v1[0].shape[1]
    cat = lambda rows: jnp.concatenate(rows, axis=0)
    v1m, i1m, v2m, i2m = cat(v1), cat(i1), cat(v2), cat(i2)
    row16 = lax.broadcasted_iota(jnp.int32, (k, tt), 0)
    row8 = lax.broadcasted_iota(jnp.int32, (k // 2, tt), 0)
    half = k // 2
    cand = [v1[0] + v2m]
    c1 = [jnp.broadcast_to(i1[0], (k, tt))]
    c2 = [i2m]
    rank = [row16]
    for a in range(1, half):
        cand.append(jnp.where(row8 < k // (a + 1), v1[a] + v2m[:half], -jnp.inf))
        c1.append(jnp.broadcast_to(i1[a], (half, tt)))
        c2.append(i2m[:half])
        rank.append(a * k + row8)
    cand.append(v1m[half:] + v2[0])
    c1.append(i1m[half:])
    c2.append(jnp.broadcast_to(i2[0], (half, tt)))
    rank.append((half + row8) * k)
    return cat(cand), cat(c1), cat(c2), cat(rank).astype(F32)


def _peer_route_kernel(q_ref, k1_ref, k2_ref, gate_ref, e1_ref, e2_ref):
    k1 = k1_ref[...]
    k2 = k2_ref[...]
    gates, e1s, e2s = [], [], []
    for h in range(PEER_HEADS):
        qa = q_ref[:, (2 * h) * PEER_HALF:(2 * h + 1) * PEER_HALF].astype(BF16)
        qb = q_ref[:, (2 * h + 1) * PEER_HALF:(2 * h + 2) * PEER_HALF].astype(BF16)
        v1, i1 = _topk_rows(_dot_nt(k1, qa), PEER_TOPK)
        v2, i2 = _topk_rows(_dot_nt(k2, qb), PEER_TOPK)
        cand, c1, c2, rank = _pruned_pair_candidates(v1, i1, v2, i2)
        sc, ea, eb = [], [], []
        for _ in range(PEER_TOPK):
            m = jnp.max(cand, axis=0, keepdims=True)
            first = jnp.min(jnp.where(cand == m, rank, float(PEER_TOPK * PEER_TOPK)), axis=0, keepdims=True)
            sel = rank == first
            sc.append(m)
            ea.append(jnp.sum(jnp.where(sel, c1, 0.0), axis=0, keepdims=True))
            eb.append(jnp.sum(jnp.where(sel, c2, 0.0), axis=0, keepdims=True))
            cand = jnp.where(sel, -jnp.inf, cand)
        scm = jnp.concatenate(sc, axis=0)
        ex = jnp.exp(scm - sc[0])
        gates.append(ex / jnp.sum(ex, axis=0, keepdims=True))
        e1s.append(jnp.concatenate(ea, axis=0))
        e2s.append(jnp.concatenate(eb, axis=0))
    gate_ref[...] = jnp.concatenate(gates, axis=0).T
    e1_ref[...] = jnp.concatenate(e1s, axis=0).T
    e2_ref[...] = jnp.concatenate(e2s, axis=0).T


def peer_route(q, k1, k2):
    t = q.shape[0]
    tt = 128
    hk = PEER_HEADS * PEER_TOPK
    out = jax.ShapeDtypeStruct((t, hk), F32)
    ospec = pl.BlockSpec((tt, hk), lambda i: (i, 0))
    kspec = pl.BlockSpec((PEER_KEYS, PEER_HALF), lambda i: (0, 0))
    return pl.pallas_call(
        _peer_route_kernel,
        grid=(t // tt,),
        in_specs=[pl.BlockSpec((tt, q.shape[1]), lambda i: (i, 0)), kspec, kspec],
        out_specs=[ospec, ospec, ospec],
        out_shape=[out, out, out],
        compiler_params=_cparams("parallel"),
        name="peer_route",
    )(q, k1, k2)


def _peer_w_kernel(gate_ref, e1_ref, e2_ref, w_ref):
    tb = gate_ref.shape[0]
    sub = lax.broadcasted_iota(jnp.int32, (PEER_KEYS, gate_ref.shape[1]), 0).astype(F32)

    def body(t, carry):
        g = gate_ref[pl.ds(t, 1), :]
        a = e1_ref[pl.ds(t, 1), :]
        b = e2_ref[pl.ds(t, 1), :]
        lhs = jnp.where(a == sub, g, 0.0).astype(BF16)
        rhs = jnp.where(b == sub, 1.0, 0.0).astype(BF16)
        w_ref[t] = _dot_nt(lhs, rhs)
        return carry

    lax.fori_loop(0, tb, body, 0, unroll=16)


def peer_dense_weights(gate, e1, e2):
    t, hk = gate.shape
    tb = 128
    spec = pl.BlockSpec((tb, hk), lambda i: (i, 0))
    return pl.pallas_call(
        _peer_w_kernel,
        grid=(t // tb,),
        in_specs=[spec, spec, spec],
        out_specs=pl.BlockSpec((tb, PEER_KEYS, PEER_KEYS), lambda i: (i, 0, 0)),
        out_shape=jax.ShapeDtypeStruct((t, PEER_KEYS, PEER_KEYS), F32),
        compiler_params=_cparams("parallel"),
        name="peer_dense_weights",
    )(gate, e1, e2)


PEER_KEY1_PER_STEP = 8


def _peer_ffn_kernel(x_ref, w_ref, u_ref, v_ref, r_ref, o_ref):
    e = pl.program_id(1)

    @pl.when(e == 0)
    def _():
        o_ref[...] = r_ref[...]

    a = _dot_nt(x_ref[...], u_ref[...])
    act = 0.5 * a * (1.0 + lax.erf(a * (2.0 ** -0.5)))
    hmat = jnp.concatenate(
        [(act[:, i * PEER_KEYS:(i + 1) * PEER_KEYS] * w_ref[:, i, :]).astype(BF16)
         for i in range(PEER_KEY1_PER_STEP)], axis=1)
    o_ref[...] += jnp.dot(hmat, v_ref[...], preferred_element_type=F32)


def peer_ffn_dense(xn, w, u_tab, v_tab, resid):
    t, d = xn.shape
    n_exp = u_tab.shape[0]
    tb = _pick_tile(t, MXU_ROW_TILES)
    eb = PEER_KEY1_PER_STEP * PEER_KEYS
    return pl.pallas_call(
        _peer_ffn_kernel,
        grid=(t // tb, n_exp // eb),
        in_specs=[pl.BlockSpec((tb, d), lambda i, e: (i, 0), pipeline_mode=ONCE),
                  pl.BlockSpec((tb, PEER_KEY1_PER_STEP, PEER_KEYS), lambda i, e: (i, e, 0)),
                  pl.BlockSpec((eb, d), lambda i, e: (e, 0)),
                  pl.BlockSpec((eb, d), lambda i, e: (e, 0)),
                  pl.BlockSpec((tb, d), lambda i, e: (i, 0), pipeline_mode=ONCE)],
        out_specs=pl.BlockSpec((tb, d), lambda i, e: (i, 0), pipeline_mode=ONCE),
        out_shape=jax.ShapeDtypeStruct((t, d), F32),
        compiler_params=_cparams("parallel", "arbitrary"),
        name="peer_ffn_dense",
    )(xn, w, u_tab, v_tab, resid)


def peer_layer(x, g_ffn, wq, k1, k2, u_tab, v_tab):
    q, xn = norm_matmul(x, g_ffn, wq, emit_h=True, name="peer_query")
    gate, e1, e2 = peer_route(q, k1, k2)
    w = peer_dense_weights(gate, e1, e2)
    return peer_ffn_dense(xn, w, u_tab, v_tab, x)


def _group_norm_kernel(x_ref, bd_ref, g_ref, *out_refs, inv_group, scales):
    x = x_ref[...]
    sq = x * x
    hi = sq.astype(BF16)
    lo = (sq - hi.astype(F32)).astype(BF16)
    bd = bd_ref[...]
    ss = jnp.dot(hi, bd, preferred_element_type=F32) + jnp.dot(lo, bd, preferred_element_type=F32)
    y = x * lax.rsqrt(ss * inv_group + NORM_EPS) * g_ref[...]
    for o_ref, sc in zip(out_refs, scales):
        o_ref[...] = (y if sc == 1.0 else y * sc).astype(o_ref.dtype)


def group_norm(src, col0, width, group, gain, outs):
    t = src.shape[0]
    assert col0 % width == 0 and width % group == 0
    tm = _pick_tile(t, (528, 512, 256, 128))
    lane = jnp.arange(width, dtype=jnp.int32) // group
    bd = (lane[:, None] == lane[None, :]).astype(BF16)
    g = jnp.tile(gain.astype(F32), width // group).reshape(1, width)
    ospec = pl.BlockSpec((tm, width), lambda i: (i, 0))
    res = pl.pallas_call(
        functools.partial(_group_norm_kernel, inv_group=1.0 / group, scales=tuple(s for _, s in outs)),
        grid=(t // tm,),
        in_specs=[pl.BlockSpec((tm, width), lambda i: (i, col0 // width)),
                  pl.BlockSpec((width, width), lambda i: (0, 0)),
                  pl.BlockSpec((1, width), lambda i: (0, 0))],
        out_specs=[ospec] * len(outs),
        out_shape=[jax.ShapeDtypeStruct((t, width), dt) for dt, _ in outs],
        compiler_params=_cparams("parallel"),
        name="group_norm",
    )(src, bd, g)
    return res


CONV_HALO = 8


def _conv_kernel(x_ref, halo_ref, w_ref, b_ref, o_ref, sc, *, first_tile_has_no_history):
    tm = x_ref.shape[0]
    halo = halo_ref[...]
    if first_tile_has_no_history:
        halo = jnp.where(pl.program_id(0) == 0, 0.0, halo)
    sc[0:CONV_HALO, :] = halo
    sc[CONV_HALO:, :] = x_ref[...]
    acc = b_ref[...]
    for j in range(SSM_CONV):
        acc = acc + w_ref[j:j + 1, :] * sc[pl.ds(CONV_HALO - (SSM_CONV - 1 - j), tm), :]
    o_ref[...] = jax.nn.silu(acc)


def causal_conv_silu(src, col0, width, row0, rows, tm, tc, halo_src, conv_w, conv_b):
    assert col0 % tc == 0 and width % tc == 0 and row0 % tm == 0 and rows % tm == 0 and tm % CONV_HALO == 0
    cb, rb, hb = col0 // tc, row0 // tm, tm // CONV_HALO
    if halo_src is None:
        halo_arr = src
        halo_spec = pl.BlockSpec((CONV_HALO, tc), lambda i, j: (jnp.maximum((rb + i) * hb - 1, 0), cb + j))
    else:
        halo_arr = halo_src
        halo_spec = pl.BlockSpec((CONV_HALO, tc), lambda i, j: (i, j))
    return pl.pallas_call(
        functools.partial(_conv_kernel, first_tile_has_no_history=halo_src is None),
        grid=(rows // tm, width // tc),
        in_specs=[pl.BlockSpec((tm, tc), lambda i, j: (rb + i, cb + j)),
                  halo_spec,
                  pl.BlockSpec((SSM_CONV, tc), lambda i, j: (0, j)),
                  pl.BlockSpec((1, tc), lambda i, j: (0, j))],
        out_specs=pl.BlockSpec((tm, tc), lambda i, j: (i, j)),
        out_shape=jax.ShapeDtypeStruct((rows, width), F32),
        scratch_shapes=[pltpu.VMEM((tm + CONV_HALO, tc), F32)],
        compiler_params=_cparams("parallel", "parallel"),
        name="causal_conv_silu",
    )(src, halo_arr, conv_w.astype(F32), conv_b.reshape(1, width).astype(F32))


SSM_INNER = SSM_HEADS * SSM_HEAD_DIM
SSM_GN = SSM_GROUPS * SSM_STATE
HEADS_PER_GROUP = SSM_HEADS // SSM_GROUPS
GROUP_W = HEADS_PER_GROUP * SSM_HEAD_DIM


def _ssd_kernel(xs_ref, b_ref, c_ref, z_ref, dt_ref, dtt_ref, dtb_row_ref, dtb_col_ref, a_row_ref, a_col_ref,
                e_ref, dsk_ref, g_ref, h0_ref, y_ref, hout_ref, h_sc, y_sc, *pad_scs, valid):
    q = SSM_CHUNK
    hi = lax.Precision.HIGHEST
    c = pl.program_id(1)

    @pl.when(c == 0)
    def _():
        h_sc[...] = h0_ref[...]

    def rows(ref, sc):
        if valid == q:
            return ref[...]
        sc[...] = jnp.zeros(sc.shape, F32)
        sc[0:valid, :] = ref[...]
        return sc[...]

    if valid == q:
        pad_scs = (None,) * 5
    xs = rows(xs_ref, pad_scs[0])
    bm = rows(b_ref, pad_scs[1])
    cm = rows(c_ref, pad_scs[2])
    z = rows(z_ref, pad_scs[3])
    dt_raw = rows(dt_ref, pad_scs[4])

    row_i = lax.broadcasted_iota(jnp.int32, (q, q), 0)
    col_i = lax.broadcasted_iota(jnp.int32, (q, q), 1)
    causal = row_i >= col_i
    dt = jax.nn.softplus(dt_raw + dtb_row_ref[...])
    dtt = jax.nn.softplus(dtt_ref[...] + dtb_col_ref[...])
    if valid < q:
        dt = jnp.where(row_i < valid, dt, 0.0)
        dtt = jnp.where(lax.broadcasted_iota(jnp.int32, dtt.shape, 1) < valid, dtt, 0.0)
    acs = jnp.dot(causal.astype(F32), dt * a_row_ref[...], precision=hi, preferred_element_type=F32)
    acst = jnp.dot(dtt * a_col_ref[...], (row_i <= col_i).astype(F32), precision=hi,
                   preferred_element_type=F32)
    per_head = jnp.concatenate([dt, jnp.exp(acs[q - 1:q, :] - acs), jnp.exp(acs)], axis=0)
    expand = e_ref[...]
    piece = per_head.astype(BF16)
    expanded = jnp.dot(piece, expand, preferred_element_type=F32)
    rest = per_head - piece.astype(F32)
    for _ in range(2):
        piece = rest.astype(BF16)
        expanded = expanded + jnp.dot(piece, expand, preferred_element_type=F32)
        rest = rest - piece.astype(F32)
    dt_e, dec_e, eacs_e = expanded[0:q], expanded[q:2 * q], expanded[2 * q:3 * q]
    xdt = xs * dt_e
    xdd = xdt * dec_e
    xdt_b = xdt.astype(BF16)
    lane = lax.broadcasted_iota(jnp.int32, (q, 2 * SSM_HEAD_DIM), 1)

    for g in range(SSM_GROUPS):
        bg = bm[:, g * SSM_STATE:(g + 1) * SSM_STATE].astype(BF16)
        cg = cm[:, g * SSM_STATE:(g + 1) * SSM_STATE].astype(BF16)
        cb = _dot_nt(cg, bg)
        for pair in range(HEADS_PER_GROUP // 2):
            h_a = g * HEADS_PER_GROUP + 2 * pair
            slab = slice(h_a * SSM_HEAD_DIM, (h_a + 2) * SSM_HEAD_DIM)
            xpair = xdt_b[:, slab]
            ypair = jnp.zeros((q, 2 * SSM_HEAD_DIM), F32)
            for which in range(2):
                h = h_a + which
                seg = acs[:, h:h + 1] - acst[h:h + 1, :]
                lmat = jnp.exp(jnp.where(causal, seg, -jnp.inf))
                mine = (lane >= which * SSM_HEAD_DIM) & (lane < (which + 1) * SSM_HEAD_DIM)
                ypair = ypair + jnp.dot((cb * lmat).astype(BF16), jnp.where(mine, xpair, jnp.zeros_like(xpair)),
                                        preferred_element_type=F32)
            y_sc[:, slab] = ypair
        gs = slice(g * GROUP_W, (g + 1) * GROUP_W)
        y_sc[:, gs] += _dot_nt(cg, h_sc[gs, :].astype(BF16)) * eacs_e[:, gs]
        st = jnp.dot(xdd[:, gs].T.astype(BF16), bg, preferred_element_type=F32)
        for hh in range(HEADS_PER_GROUP):
            h = g * HEADS_PER_GROUP + hh
            hs = slice(h * SSM_HEAD_DIM, (h + 1) * SSM_HEAD_DIM)
            dec = jnp.exp(acst[h:h + 1, q - 1:q])
            h_sc[hs, :] = h_sc[hs, :] * dec + st[hh * SSM_HEAD_DIM:(hh + 1) * SSM_HEAD_DIM, :]

    y = y_sc[...] + dsk_ref[...] * xs
    y = y * jax.nn.silu(z)
    gain = g_ref[...]
    for g in range(SSM_GROUPS):
        gs = slice(g * GROUP_W, (g + 1) * GROUP_W)
        yg = y[:, gs]
        yn = yg * lax.rsqrt(jnp.mean(yg * yg, axis=-1, keepdims=True) + NORM_EPS) * gain[:, gs]
        y_ref[:, gs] = yn[0:valid].astype(y_ref.dtype)

    @pl.when(c == pl.num_programs(1) - 1)
    def _():
        hout_ref[...] = h_sc[...]


def ssd_branch(xc, u, z_col0, u_row0, dt_pad, dt_t, h0, n_batch, n_chunks, valid, dt_bias, a_log, d_skip, g_ssm,
               out_dtype):
    q = SSM_CHUNK
    rows = xc.shape[0]
    assert rows == n_batch * n_chunks * valid and u_row0 % valid == 0 and z_col0 % SSM_INNER == 0
    rb = u_row0 // valid
    step = lambda b, c: b * n_chunks + c
    pad128 = lambda v: jnp.pad(v.astype(F32), (0, 128 - SSM_HEADS))
    a = -jnp.exp(a_log.astype(F32))
    head_lane = jnp.arange(SSM_INNER, dtype=jnp.int32) // SSM_HEAD_DIM
    expand = (jnp.arange(128, dtype=jnp.int32)[:, None] == head_lane[None, :]).astype(BF16)
    const = lambda shape: pl.BlockSpec(shape, lambda b, c: (0,) * len(shape))
    pad_scs = [] if valid == q else [pltpu.VMEM((q, w), F32) for w in (SSM_INNER, SSM_GN, SSM_GN, SSM_INNER, 128)]
    y, h_fin = pl.pallas_call(
        functools.partial(_ssd_kernel, valid=valid),
        grid=(n_batch, n_chunks),
        in_specs=[
            pl.BlockSpec((valid, SSM_INNER), lambda b, c: (step(b, c), 0)),
            pl.BlockSpec((valid, SSM_GN), lambda b, c: (step(b, c), SSM_INNER // SSM_GN)),
            pl.BlockSpec((valid, SSM_GN), lambda b, c: (step(b, c), SSM_INNER // SSM_GN + 1)),
            pl.BlockSpec((valid, SSM_INNER), lambda b, c: (rb + step(b, c), z_col0 // SSM_INNER)),
            pl.BlockSpec((valid, 128), lambda b, c: (rb + step(b, c), 0)),
            pl.BlockSpec((None, SSM_HEADS, q), lambda b, c: (step(b, c), 0, 0)),
            const((1, 128)), const((SSM_HEADS, q)), const((1, 128)), const((SSM_HEADS, q)),
            const((128, SSM_INNER)), const((1, SSM_INNER)), const((1, SSM_INNER)),
            pl.BlockSpec((None, SSM_INNER, SSM_STATE), lambda b, c: (b, 0, 0)),
        ],
        out_specs=[pl.BlockSpec((valid, SSM_INNER), lambda b, c: (step(b, c), 0)),
                   pl.BlockSpec((None, SSM_INNER, SSM_STATE), lambda b, c: (b, 0, 0))],
        out_shape=[jax.ShapeDtypeStruct((rows, SSM_INNER), out_dtype),
                   jax.ShapeDtypeStruct((n_batch, SSM_INNER, SSM_STATE), F32)],
        scratch_shapes=[pltpu.VMEM((SSM_INNER, SSM_STATE), F32), pltpu.VMEM((q, SSM_INNER), F32)] + pad_scs,
        compiler_params=_cparams("parallel", "arbitrary"),
        name="ssd_branch",
    )(xc, xc, xc, u, dt_pad, dt_t,
      pad128(dt_bias).reshape(1, 128), jnp.broadcast_to(dt_bias.astype(F32)[:, None], (SSM_HEADS, q)),
      pad128(a).reshape(1, 128), jnp.broadcast_to(a[:, None], (SSM_HEADS, q)),
      expand, jnp.repeat(d_skip.astype(F32), SSM_HEAD_DIM).reshape(1, SSM_INNER),
      g_ssm.astype(F32).reshape(1, SSM_INNER), h0)
    return y, h_fin


def _mem_attn_kernel(q_ref, k_ref, v_ref, o_ref):
    for h in range(MEM_HEADS):
        hs = slice(h * MEM_HEAD_DIM, (h + 1) * MEM_HEAD_DIM)
        s = _dot_nt(q_ref[:, hs].astype(BF16), k_ref[:, hs].astype(BF16))
        p = jnp.exp(s - jnp.max(s, axis=-1, keepdims=True))
        o = jnp.dot(p.astype(BF16), v_ref[:, hs].astype(BF16), preferred_element_type=F32)
        o_ref[:, hs] = (o / jnp.sum(p, axis=-1, keepdims=True)).astype(o_ref.dtype)


def memory_attention(qn, q_row0, rows, tq, mk, mv, v_col_blk, tiles_per_batch, out_dtype):
    n_mem = mk.shape[0] * tq * tiles_per_batch // rows
    rb = q_row0 // tq
    return pl.pallas_call(
        _mem_attn_kernel,
        grid=(rows // tq,),
        in_specs=[pl.BlockSpec((tq, MEM_W), lambda i: (rb + i, 0)),
                  pl.BlockSpec((n_mem, MEM_W), lambda i: (i // tiles_per_batch, 0)),
                  pl.BlockSpec((n_mem, MEM_W), lambda i: (i // tiles_per_batch, v_col_blk))],
        out_specs=pl.BlockSpec((tq, MEM_W), lambda i: (i, 0)),
        out_shape=jax.ShapeDtypeStruct((rows, MEM_W), out_dtype),
        compiler_params=_cparams("parallel"),
        name="memory_attention",
    )(qn, mk, mv)


SAMPLE_PAGES_PER_STEP = 16
SAMPLE_ROWS = DA_HEADS * 2 * 8


def _sample_attn_kernel(pt_ref, lam_ref, q_ref, *refs, pages_per_step, post_scale):
    n = pages_per_step
    kt_refs, v_refs = refs[:n], refs[n:2 * n]
    knew_ref, vnew_ref, blast_ref, bnew_ref, rexp_ref, hmask_ref, g_ref, o_ref, m_sc, l_sc, acc_sc = refs[2 * n:]
    del pt_ref
    step = pl.program_id(1)
    n_new = o_ref.shape[0]
    rows_per_head = 2 * n_new

    @pl.when(step == 0)
    def _():
        m_sc[...] = jnp.full(m_sc.shape, NEG_BIG, F32)
        l_sc[...] = jnp.zeros(l_sc.shape, F32)
        acc_sc[...] = jnp.zeros(acc_sc.shape, F32)

    q = q_ref[...]

    def update(pages):
        logits = []
        for kt_ref, _, bias in pages:
            s = jnp.dot(q, kt_ref[...].astype(BF16), preferred_element_type=F32)
            logits.append(s if bias is None else s + bias)
        s_max = logits[0]
        for s in logits[1:]:
            s_max = jnp.maximum(s_max, s)
        m_prev = m_sc[...]
        m_new = jnp.maximum(m_prev, jnp.max(s_max, axis=-1, keepdims=True))
        alpha = jnp.exp2(m_prev - m_new)
        m_sc[...] = m_new
        l_new = alpha * l_sc[...]
        acc = alpha * acc_sc[...]
        rexp = rexp_ref[...]
        own_head = hmask_ref[...]
        for (_, v_ref, _), s in zip(pages, logits):
            p = jnp.exp2(s - m_new)
            l_new = l_new + p
            p_rep = jnp.dot(p.astype(BF16), rexp, preferred_element_type=F32)
            acc = acc + jnp.dot(p_rep.astype(BF16) * own_head, v_ref[...].astype(BF16),
                                preferred_element_type=F32)
        l_sc[...] = l_new
        acc_sc[...] = acc

    last = pl.num_programs(1) - 1
    past = [(kt_refs[i], v_refs[i], None) for i in range(n)]

    @pl.when(step < last)
    def _():
        update(past)

    @pl.when(step == last)
    def _():
        update(past[:-1] + [(kt_refs[n - 1], v_refs[n - 1], blast_ref[...]), (knew_ref, vnew_ref, bnew_ref[...])])
        attn = acc_sc[...] / jnp.sum(l_sc[...], axis=-1, keepdims=True)
        for h in range(DA_HEADS):
            r0 = h * rows_per_head
            o = attn[r0:r0 + n_new, :] - lam_ref[0] * attn[r0 + n_new:r0 + 2 * n_new, :]
            y = o * lax.rsqrt(jnp.mean(o * o, axis=-1, keepdims=True) + NORM_EPS)
            o_ref[:, h * DA_DV:(h + 1) * DA_DV] = (y * g_ref[...]) * post_scale


def sample_bias_tiles(table, n_new, page):
    def rows(off):
        b = _toeplitz_bias(table, off, page)[:, :n_new, :]
        return jnp.broadcast_to(b[:, None], (DA_HEADS, 2, n_new, page)).reshape(SAMPLE_ROWS, page)
    return rows(page), rows(0)


def sample_diff_attention(layer, qn_s, kn_s, v_s, lam, k_pool_t, v_pool, page_table, b_last, b_new, g_sub,
                          post_scale):
    bsz, n_new, qw = qn_s.shape
    page = v_pool.shape[2]
    n_pages = page_table.shape[1]
    n = SAMPLE_PAGES_PER_STEP
    assert n_pages % n == 0 and SAMPLE_ROWS == DA_HEADS * 2 * n_new
    qg = qn_s.reshape(bsz, n_new, DA_HEADS * 2, DA_DK)
    eye = jnp.eye(DA_HEADS * 2, dtype=F32)
    q_rows = (qg.transpose(0, 2, 1, 3)[:, :, :, None, :] * eye[None, :, None, :, None]).reshape(
        bsz, SAMPLE_ROWS, qw).astype(BF16)
    knew_t = jnp.pad(kn_s.transpose(0, 2, 1), ((0, 0), (0, 0), (0, page - n_new)))
    vnew = jnp.pad(v_s, ((0, 0), (0, page - n_new), (0, 0), (0, 0))).reshape(bsz, page * DA_HEADS, DA_DV)
    v_rows = v_pool.reshape(v_pool.shape[0], v_pool.shape[1], page * DA_HEADS, DA_DV)
    col = jnp.arange(page * DA_HEADS, dtype=jnp.int32)
    rexp = (col[None, :] // DA_HEADS == jnp.arange(page, dtype=jnp.int32)[:, None]).astype(BF16)
    hmask = (col[None, :] % DA_HEADS
             == jnp.arange(SAMPLE_ROWS, dtype=jnp.int32)[:, None] // (2 * n_new)).astype(BF16)
    kspec = lambda i: pl.BlockSpec((None, None, qw, page), lambda b, s, pt: (layer, pt[b, s * n + i], 0, 0))
    vspec = lambda i: pl.BlockSpec((None, None, page * DA_HEADS, DA_DV),
                                   lambda b, s, pt: (layer, pt[b, s * n + i], 0, 0))
    const2 = lambda shape: pl.BlockSpec(shape, lambda b, s, pt: (0, 0))
    grid_spec = pltpu.PrefetchScalarGridSpec(
        num_scalar_prefetch=1,
        grid=(bsz, n_pages // n),
        in_specs=[pl.BlockSpec(memory_space=pltpu.SMEM),
                  pl.BlockSpec((None, SAMPLE_ROWS, qw), lambda b, s, pt: (b, 0, 0))]
                 + [kspec(i) for i in range(n)] + [vspec(i) for i in range(n)]
                 + [pl.BlockSpec((None, qw, page), lambda b, s, pt: (b, 0, 0)),
                    pl.BlockSpec((None, page * DA_HEADS, DA_DV), lambda b, s, pt: (b, 0, 0)),
                    const2((SAMPLE_ROWS, page)), const2((SAMPLE_ROWS, page)),
                    const2((page, page * DA_HEADS)), const2((SAMPLE_ROWS, page * DA_HEADS)), const2((1, DA_DV))],
        out_specs=pl.BlockSpec((n_new, DA_VW), lambda b, s, pt: (b, 0)),
        scratch_shapes=[pltpu.VMEM((SAMPLE_ROWS, page), F32), pltpu.VMEM((SAMPLE_ROWS, page), F32),
                        pltpu.VMEM((SAMPLE_ROWS, DA_DV), F32)],
    )
    return pl.pallas_call(
        functools.partial(_sample_attn_kernel, pages_per_step=n, post_scale=post_scale),
        grid_spec=grid_spec,
        out_shape=jax.ShapeDtypeStruct((bsz * n_new, DA_VW), F32),
        compiler_params=_cparams("parallel", "arbitrary"),
        name="sample_diff_attention",
    )(page_table, lam.reshape(1).astype(F32), q_rows, *([k_pool_t] * n), *([v_rows] * n), knew_t, vnew,
      b_last, b_new, rexp, hmask, g_sub.reshape(1, DA_DV).astype(F32))


FA_TILE = 512


def kernel(x_prompt, x_sample, cache_attn_k, cache_attn_v, cache_mem_k, cache_mem_v, state_ssm, state_conv, page_table, mem_prompt, rel_bias, g_mix, w_in, g_q, g_k, lam_q1, lam_k1, lam_q2, lam_k2, g_sub, conv_w, conv_b, dt_bias, a_log, d_skip, g_ssm, g_mem, w_mem_kv, g_mq, g_mk, w_br_attn, w_br_ssm, w_br_mem, w_out, g_ffn, peer_wq, peer_k1, peer_k2, peer_u, peer_v):
    depth = w_in.shape[0]
    bp, sp, d = x_prompt.shape
    bs, ts, _ = x_sample.shape
    assert bp == 1 and sp % SSM_CHUNK == 0 and sp % FA_TILE == 0 and ts >= SSM_CONV - 1
    n_p = bp * sp
    n_s = bs * ts
    conv_dim = SSM_INNER + 2 * SSM_GN
    n_mem = mem_prompt.shape[1]
    n_pool, page = cache_attn_v.shape[1], cache_attn_v.shape[2]
    s_z = 2 * DA_QW + DA_VW
    s_dt = s_z + SSM_INNER + conv_dim
    s_mq = s_dt + SSM_HEADS
    c_z, c_xbc = 0, SSM_INNER
    c_q = c_xbc + conv_dim
    c_k = c_q + DA_QW
    c_v = c_k + DA_QW
    c_mq = c_v + DA_VW
    c_gate = c_mq + MEM_W

    x = jnp.concatenate([x_prompt.reshape(n_p, d), x_sample.reshape(n_s, d)], axis=0)
    bias_tiles = prompt_bias_tiles(rel_bias, FA_TILE)
    b_last, b_new = sample_bias_tiles(rel_bias, ts, page)
    k_pool_t = jnp.transpose(cache_attn_k, (0, 1, 3, 4, 5, 2)).reshape(depth, n_pool, DA_QW, page)
    zero_h = jnp.zeros((bp, SSM_INNER, SSM_STATE), F32)

    outs = {k: [] for k in ("pk", "pv", "pmk", "pmv", "ph", "pc", "sk", "sv", "sh", "sc")}
    for l in range(depth):
        lam_init = 0.8 - 0.6 * math.exp(-0.3 * l)
        lam = (jnp.exp(jnp.sum(lam_q1[l].astype(F32) * lam_k1[l].astype(F32)))
               - jnp.exp(jnp.sum(lam_q2[l].astype(F32) * lam_k2[l].astype(F32))) + lam_init)
        wl = w_in[l]
        w_main = jnp.concatenate([wl[:, s_z:s_dt], wl[:, :s_z], wl[:, s_mq:]], axis=1).astype(BF16)
        w_dt = jnp.pad(wl[:, s_dt:s_mq], ((0, 0), (0, 128 - SSM_HEADS))).astype(BF16)

        u, dt_pad = norm_matmul(x, g_mix[l], w_main, side_w=w_dt, name="in_proj")
        (qn,) = group_norm(u, c_q, DA_QW, DA_DK, g_q[l], [(BF16, DA_SCALE * LOG2E)])
        kn, kn_b = group_norm(u, c_k, DA_QW, DA_DK, g_k[l], [(F32, 1.0), (BF16, 1.0)])
        (mqn,) = group_norm(u, c_mq, MEM_W, MEM_HEAD_DIM, g_mq[l], [(F32, MEM_SCALE)])
        v_all = u[:, c_v:c_v + DA_VW]
        xbc_s = u[n_p:, c_xbc:c_xbc + conv_dim].reshape(bs, ts, conv_dim)

        a_p = prompt_diff_attention(qn, kn_b, u, c_v, n_p, lam, bias_tiles, g_sub[l], 1.0 - lam_init, FA_TILE)
        a_s = sample_diff_attention(l, qn[n_p:].astype(F32).reshape(bs, ts, DA_QW), kn[n_p:].reshape(bs, ts, DA_QW),
                                    v_all[n_p:].reshape(bs, ts, DA_HEADS, DA_DV), lam, k_pool_t, cache_attn_v,
                                    page_table, b_last, b_new, g_sub[l], 1.0 - lam_init)

        (mkv,) = norm_matmul(mem_prompt.reshape(bp * n_mem, d), g_mem[l], w_mem_kv[l].astype(BF16), name="mem_kv")
        (mk_p,) = group_norm(mkv, 0, MEM_W, MEM_HEAD_DIM, g_mk[l], [(F32, 1.0)])
        tq = 512
        m_p = memory_attention(mqn, 0, n_p, tq, mk_p, mkv, 1, sp // tq, BF16)
        m_s = memory_attention(mqn, n_p, n_s, ts, cache_mem_k[l].reshape(bs * n_mem, MEM_W),
                               cache_mem_v[l].reshape(bs * n_mem, MEM_W), 0, 1, F32)

        xc_p = causal_conv_silu(u, c_xbc, conv_dim, 0, n_p, 512, 1024, None, conv_w[l], conv_b[l])
        halo_s = jnp.pad(state_conv[l], ((0, 0), (CONV_HALO - (SSM_CONV - 1), 0), (0, 0))).reshape(
            bs * CONV_HALO, conv_dim)
        xc_s = causal_conv_silu(xbc_s.reshape(n_s, conv_dim), 0, conv_dim, 0, n_s, ts, conv_dim, halo_s,
                                conv_w[l], conv_b[l])
        dt_raw = dt_pad[:, :SSM_HEADS]
        dtt_p = dt_raw[:n_p].reshape(n_p // SSM_CHUNK, SSM_CHUNK, SSM_HEADS).transpose(0, 2, 1)
        dtt_s = jnp.pad(dt_raw[n_p:].reshape(bs, ts, SSM_HEADS).transpose(0, 2, 1),
                        ((0, 0), (0, 0), (0, SSM_CHUNK - ts)))
        s_p, h_p = ssd_branch(xc_p, u, c_z, 0, dt_pad, dtt_p, zero_h, bp, sp // SSM_CHUNK, SSM_CHUNK,
                              dt_bias[l], a_log[l], d_skip[l], g_ssm[l], BF16)
        s_s, h_s = ssd_branch(xc_s, u, c_z, n_p, dt_pad, dtt_s, state_ssm[l].reshape(bs, SSM_INNER, SSM_STATE),
                              bs, 1, ts, dt_bias[l], a_log[l], d_skip[l], g_ssm[l], F32)

        a_out = jnp.concatenate([a_p, a_s.astype(BF16)], axis=0)
        s_out = jnp.concatenate([s_p, s_s.astype(BF16)], axis=0)
        m_out = jnp.concatenate([m_p, m_s.astype(BF16)], axis=0)
        merged = merge_branches(a_out, s_out, m_out, u, c_gate, w_br_attn[l].astype(BF16),
                                w_br_ssm[l].astype(BF16), w_br_mem[l].astype(BF16))
        x = matmul(merged, w_out[l].astype(BF16), residual=x, name="out_proj")
        x = peer_layer(x, g_ffn[l], peer_wq[l].astype(BF16), peer_k1[l].astype(BF16), peer_k2[l].astype(BF16),
                       peer_u[l].astype(BF16), peer_v[l].astype(BF16))

        outs["pk"].append(kn[:n_p].reshape(bp, sp, DA_HEADS, 2, DA_DK))
        outs["pv"].append(v_all[:n_p].reshape(bp, sp, DA_HEADS, DA_DV))
        outs["pmk"].append(mk_p.reshape(bp, n_mem, MEM_HEADS, MEM_HEAD_DIM))
        outs["pmv"].append(mkv[:, MEM_W:].reshape(bp, n_mem, MEM_HEADS, MEM_HEAD_DIM))
        outs["ph"].append(h_p.reshape(bp, SSM_HEADS, SSM_HEAD_DIM, SSM_STATE))
        outs["pc"].append(u[n_p - (SSM_CONV - 1):n_p, c_xbc:c_xbc + conv_dim].reshape(bp, SSM_CONV - 1, conv_dim))
        outs["sk"].append(kn[n_p:].reshape(bs, ts, DA_HEADS, 2, DA_DK))
        outs["sv"].append(v_all[n_p:].reshape(bs, ts, DA_HEADS, DA_DV))
        outs["sh"].append(h_s.reshape(bs, SSM_HEADS, SSM_HEAD_DIM, SSM_STATE))
        outs["sc"].append(xbc_s[:, ts - (SSM_CONV - 1):])

    y_prompt = x[:n_p].reshape(bp, sp, d)
    y_sample = x[n_p:].reshape(bs, ts, d)
    st = lambda name: jnp.stack(outs[name])
    return (y_prompt, y_sample, st("pk"), st("pv"), st("pmk"), st("pmv"), st("ph"), st("pc"),
            st("sk"), st("sv"), st("sh"), st("sc"))
```

```python
import functools
import math

import jax
import jax.numpy as jnp
from jax import lax
from jax.experimental import pallas as pl
from jax.experimental.pallas import tpu as pltpu

F32 = jnp.float32
BF16 = jnp.bfloat16

NORM_EPS = 1e-6
NEG_BIG = -1e30
V7X_VMEM_LIMIT_BYTES = 56 * 1024 * 1024

DA_HEADS, DA_DK, DA_DV = 8, 64, 128
DA_QW = DA_HEADS * 2 * DA_DK
DA_VW = DA_HEADS * DA_DV
DA_SCALE = DA_DK ** -0.5
N_BUCKETS, MAX_DISTANCE = 32, 128
SSM_HEADS, SSM_HEAD_DIM, SSM_GROUPS, SSM_STATE, SSM_CONV, SSM_CHUNK = 32, 64, 4, 128, 4, 128
MEM_HEADS, MEM_HEAD_DIM = 4, 256
MEM_W = MEM_HEADS * MEM_HEAD_DIM
MEM_SCALE = MEM_HEAD_DIM ** -0.5
PEER_HEADS, PEER_KEYS, PEER_TOPK, PEER_HALF = 8, 128, 16, 128


def _cparams(*sem):
    return pltpu.CompilerParams(dimension_semantics=sem, vmem_limit_bytes=V7X_VMEM_LIMIT_BYTES)


def _dot_nt(a, b):
    return lax.dot_general(a, b, (((1,), (1,)), ((), ())), preferred_element_type=F32)


def _pick_tile(n, candidates):
    for c in candidates:
        if n % c == 0:
            return c
    return n


MXU_ROW_TILES = (1056, 1024, 512, 256, 128)
ONCE = pl.Buffered(1)


def _norm_mm_kernel(x_ref, g_ref, w_ref, *rest, has_side, emit_h):
    rest = list(rest)
    side_w_ref = rest.pop(0) if has_side else None
    o_ref = rest.pop(0)
    side_o_ref = rest.pop(0) if has_side else None
    h_o_ref = rest.pop(0) if emit_h else None
    (h_sc,) = rest

    @pl.when(pl.program_id(1) == 0)
    def _():
        x = x_ref[...]
        ms = jnp.mean(x * x, axis=-1, keepdims=True)
        h = (x * lax.rsqrt(ms + NORM_EPS) * g_ref[...]).astype(BF16)
        h_sc[...] = h
        if has_side:
            side_o_ref[...] = jnp.dot(h, side_w_ref[...], preferred_element_type=F32)
        if emit_h:
            h_o_ref[...] = h

    o_ref[...] = jnp.dot(h_sc[...], w_ref[...], preferred_element_type=F32)


def norm_matmul(x, g, w, side_w=None, emit_h=False, name="norm_matmul"):
    t, d = x.shape
    n = w.shape[1]
    tm = _pick_tile(t, MXU_ROW_TILES)
    tn = _pick_tile(n, (1536, 1024, 512, 256, 128))
    row = lambda width: pl.BlockSpec((tm, width), lambda i, j: (i, 0))
    in_specs = [pl.BlockSpec((tm, d), lambda i, j: (i, 0), pipeline_mode=ONCE),
                pl.BlockSpec((1, d), lambda i, j: (0, 0)), pl.BlockSpec((d, tn), lambda i, j: (0, j))]
    args = [x, g.reshape(1, d).astype(F32), w]
    out_specs = [pl.BlockSpec((tm, tn), lambda i, j: (i, j))]
    out_shape = [jax.ShapeDtypeStruct((t, n), F32)]
    if side_w is not None:
        ns = side_w.shape[1]
        in_specs.append(pl.BlockSpec((d, ns), lambda i, j: (0, 0)))
        args.append(side_w)
        out_specs.append(row(ns))
        out_shape.append(jax.ShapeDtypeStruct((t, ns), F32))
    if emit_h:
        out_specs.append(row(d))
        out_shape.append(jax.ShapeDtypeStruct((t, d), BF16))
    return pl.pallas_call(
        functools.partial(_norm_mm_kernel, has_side=side_w is not None, emit_h=emit_h),
        grid=(t // tm, n // tn),
        in_specs=in_specs,
        out_specs=out_specs,
        out_shape=out_shape,
        scratch_shapes=[pltpu.VMEM((tm, d), BF16)],
        compiler_params=_cparams("parallel", "arbitrary"),
        name=name,
    )(*args)


def _mm_kernel(a_ref, w_ref, o_ref):
    o_ref[...] = jnp.dot(a_ref[...], w_ref[...], preferred_element_type=F32).astype(o_ref.dtype)


def _mm_res_kernel(a_ref, w_ref, r_ref, o_ref):
    o_ref[...] = r_ref[...] + jnp.dot(a_ref[...], w_ref[...], preferred_element_type=F32)


def matmul(a, w, residual=None, out_dtype=F32, name="matmul"):
    t, k = a.shape
    n = w.shape[1]
    tm = _pick_tile(t, MXU_ROW_TILES)
    tn = _pick_tile(n, (1024, 512, 256, 128))
    in_specs = [pl.BlockSpec((tm, k), lambda i, j: (i, 0)), pl.BlockSpec((k, tn), lambda i, j: (0, j))]
    args = [a, w]
    body = _mm_kernel
    if residual is not None:
        in_specs.append(pl.BlockSpec((tm, tn), lambda i, j: (i, j)))
        args.append(residual)
        body = _mm_res_kernel
    return pl.pallas_call(
        body,
        grid=(t // tm, n // tn),
        in_specs=in_specs,
        out_specs=pl.BlockSpec((tm, tn), lambda i, j: (i, j)),
        out_shape=jax.ShapeDtypeStruct((t, n), out_dtype),
        compiler_params=_cparams("parallel", "parallel"),
        name=name,
    )(*args)


def _merge_kernel(a_ref, s_ref, m_ref, ga_ref, gs_ref, gm_ref, wa_ref, ws_ref, wm_ref, o_ref):
    acc = jax.nn.sigmoid(ga_ref[...]) * jnp.dot(a_ref[...], wa_ref[...], preferred_element_type=F32)
    acc += jax.nn.sigmoid(gs_ref[...]) * jnp.dot(s_ref[...], ws_ref[...], preferred_element_type=F32)
    acc += jax.nn.sigmoid(gm_ref[...]) * jnp.dot(m_ref[...], wm_ref[...], preferred_element_type=F32)
    o_ref[...] = acc.astype(o_ref.dtype)


def merge_branches(a, s, m, u, gate_col0, wa, ws, wm):
    t = a.shape[0]
    d = wa.shape[1]
    tm = _pick_tile(t, MXU_ROW_TILES)
    tn = 512
    gb = gate_col0 // tn
    nd = d // tn
    row = lambda width: pl.BlockSpec((tm, width), lambda i, j: (i, 0))
    gate = lambda which: pl.BlockSpec((tm, tn), lambda i, j: (i, gb + which * nd + j))
    wcol = lambda kdim: pl.BlockSpec((kdim, tn), lambda i, j: (0, j))
    return pl.pallas_call(
        _merge_kernel,
        grid=(t // tm, nd),
        in_specs=[row(a.shape[1]), row(s.shape[1]), row(m.shape[1]), gate(0), gate(1), gate(2),
                  wcol(wa.shape[0]), wcol(ws.shape[0]), wcol(wm.shape[0])],
        out_specs=pl.BlockSpec((tm, tn), lambda i, j: (i, j)),
        out_shape=jax.ShapeDtypeStruct((t, d), BF16),
        compiler_params=_cparams("parallel", "parallel"),
        name="merge_branches",
    )(a, s, m, u, u, u, wa, ws, wm)


def _t5_bucket(dist):
    n = jnp.maximum(dist, 0)
    exact = N_BUCKETS // 2
    nf = jnp.maximum(n, exact).astype(F32)
    large = exact + (jnp.log(nf / exact) / math.log(MAX_DISTANCE / exact) * (N_BUCKETS - exact)).astype(jnp.int32)
    return jnp.where(n < exact, n, jnp.minimum(large, N_BUCKETS - 1))


def _far_bias(table):
    return table.astype(F32)[N_BUCKETS - 1]


LOG2E = math.log2(math.e)


def _toeplitz_bias(table, off, t):
    x = jnp.arange(2 * t, dtype=jnp.int32)
    dist = jnp.where(x < t, off - x, off + 2 * t - x)
    far = _far_bias(table)
    v = (jnp.moveaxis(table.astype(F32)[_t5_bucket(dist)], -1, 0) - far[:, None]) * LOG2E
    v = jnp.where(dist[None, :] >= 0, v, NEG_BIG)
    h = v.shape[0]
    return jnp.tile(v, (1, t))[:, :t * (2 * t - 1)].reshape(h, t, 2 * t - 1)[:, :, :t]


def prompt_bias_tiles(table, tile):
    assert tile >= MAX_DISTANCE
    return jnp.stack([_toeplitz_bias(table, 0, tile), _toeplitz_bias(table, tile, tile)], axis=1)


FA_HEADS_PER_STEP = 2


def _fa_kernel(qt_ref, kt_ref, lam_ref, q_ref, k_ref, v_ref, b_ref, g_ref, o_ref, m_sc, l_sc, acc_sc, *,
               post_scale):
    s_id = pl.program_id(1)
    qi = qt_ref[s_id]
    ki = kt_ref[s_id]

    @pl.when(ki == 0)
    def _():
        m_sc[...] = jnp.full(m_sc.shape, NEG_BIG, F32)
        l_sc[...] = jnp.zeros(l_sc.shape, F32)
        acc_sc[...] = jnp.zeros(acc_sc.shape, F32)

    def update(bias_idx):
        n_lane_tiles = k_ref.shape[0] // 128
        lane = lax.broadcasted_iota(jnp.int32, (q_ref.shape[0], 2 * DA_DK), 1)
        for hh in range(FA_HEADS_PER_STEP):
            hs = slice(hh * 2 * DA_DK, (hh + 1) * 2 * DA_DK)
            q = q_ref[:, hs]
            k = k_ref[:, hs]
            v = v_ref[:, hh * DA_DV:(hh + 1) * DA_DV].astype(BF16)
            logits = []
            for m in range(2):
                in_map = (lane >= m * DA_DK) & (lane < (m + 1) * DA_DK)
                s = _dot_nt(jnp.where(in_map, q, jnp.zeros_like(q)), k)
                logits.append(s if bias_idx is None else s + b_ref[hh, bias_idx])
            for m in range(2):
                s = logits[m]
                slot = 2 * hh + m
                m_prev = m_sc[slot]
                m_new = jnp.maximum(m_prev, jnp.max(s, axis=-1, keepdims=True))
                alpha = jnp.exp2(m_prev - m_new)
                p = jnp.exp2(s - jnp.tile(m_new, (1, n_lane_tiles)))
                l_part = p[:, 0:128]
                for t in range(1, n_lane_tiles):
                    l_part = l_part + p[:, t * 128:(t + 1) * 128]
                l_sc[slot] = alpha * l_sc[slot] + l_part
                acc_sc[slot] = alpha * acc_sc[slot] + jnp.dot(p.astype(BF16), v, preferred_element_type=F32)
                m_sc[slot] = m_new

    @pl.when(ki < qi - 1)
    def _():
        update(None)

    @pl.when(ki == qi - 1)
    def _():
        update(1)

    @pl.when(ki == qi)
    def _():
        update(0)
        for hh in range(FA_HEADS_PER_STEP):
            l0 = jnp.sum(l_sc[2 * hh], axis=-1, keepdims=True)
            l1 = jnp.sum(l_sc[2 * hh + 1], axis=-1, keepdims=True)
            o = acc_sc[2 * hh] / l0 - lam_ref[0] * (acc_sc[2 * hh + 1] / l1)
            y = o * lax.rsqrt(jnp.mean(o * o, axis=-1, keepdims=True) + NORM_EPS)
            o_ref[:, hh * DA_DV:(hh + 1) * DA_DV] = ((y * g_ref[...]) * post_scale).astype(o_ref.dtype)


def prompt_diff_attention(qn, kn, v_src, v_col0, s_len, lam, bias_tiles, g_sub, post_scale, tile):
    hp = FA_HEADS_PER_STEP
    width = hp * DA_DV
    assert v_col0 % width == 0 and DA_HEADS % hp == 0 and 2 * DA_DK == DA_DV
    vb = v_col0 // width
    nq = s_len // tile
    pairs = [(qi, ki) for qi in range(nq) for ki in range(qi + 1)]
    qt = jnp.asarray([p[0] for p in pairs], jnp.int32)
    kt = jnp.asarray([p[1] for p in pairs], jnp.int32)
    grid_spec = pltpu.PrefetchScalarGridSpec(
        num_scalar_prefetch=2,
        grid=(DA_HEADS // hp, len(pairs)),
        in_specs=[
            pl.BlockSpec(memory_space=pltpu.SMEM),
            pl.BlockSpec((tile, width), lambda h, s, qt, kt: (qt[s], h)),
            pl.BlockSpec((tile, width), lambda h, s, qt, kt: (kt[s], h)),
            pl.BlockSpec((tile, width), lambda h, s, qt, kt: (kt[s], vb + h)),
            pl.BlockSpec((hp, 2, tile, tile), lambda h, s, qt, kt: (h, 0, 0, 0)),
            pl.BlockSpec((1, DA_DV), lambda h, s, qt, kt: (0, 0)),
        ],
        out_specs=pl.BlockSpec((tile, width), lambda h, s, qt, kt: (qt[s], h)),
        scratch_shapes=[pltpu.VMEM((2 * hp, tile, 128), F32), pltpu.VMEM((2 * hp, tile, 128), F32),
                        pltpu.VMEM((2 * hp, tile, DA_DV), F32)],
    )
    return pl.pallas_call(
        functools.partial(_fa_kernel, post_scale=post_scale),
        grid_spec=grid_spec,
        out_shape=jax.ShapeDtypeStruct((s_len, DA_VW), BF16),
        compiler_params=_cparams("parallel", "arbitrary"),
        name="prompt_diff_attention",
    )(qt, kt, lam.reshape(1).astype(F32), qn, kn, v_src, bias_tiles, g_sub.reshape(1, DA_DV).astype(F32))


def _topk_rows(s, k):
    riota = lax.broadcasted_iota(jnp.int32, s.shape, 0).astype(F32)
    big = float(s.shape[0])
    vals, idxs = [], []
    for _ in range(k):
        m = jnp.max(s, axis=0, keepdims=True)
        idx = jnp.min(jnp.where(s == m, riota, big), axis=0, keepdims=True)
        vals.append(m)
        idxs.append(idx)
        s = jnp.where(riota == idx, -jnp.inf, s)
    return vals, idxs


def _pruned_pair_candidates(v1, i1, v2, i2):
    k = PEER_TOPK
    tt = v1[0].shape[1]
    cat = lambda rows: jnp.concatenate(rows, axis=0)
    v1m, i1m, v2m, i2m = cat(v1), cat(i1), cat(v2), cat(i2)
    row16 = lax.broadcasted_iota(jnp.int32, (k, tt), 0)
    row8 = lax.broadcasted_iota(jnp.int32, (k // 2, tt), 0)
    half = k // 2
    cand = [v1[0] + v2m]
    c1 = [jnp.broadcast_to(i1[0], (k, tt))]
    c2 = [i2m]
    rank = [row16]
    for a in range(1, half):
        cand.append(jnp.where(row8 < k // (a + 1), v1[a] + v2m[:half], -jnp.inf))
        c1.append(jnp.broadcast_to(i1[a], (half, tt)))
        c2.append(i2m[:half])
        rank.append(a * k + row8)
    cand.append(v1m[half:] + v2[0])
    c1.append(i1m[half:])
    c2.append(jnp.broadcast_to(i2[0], (half, tt)))
    rank.append((half + row8) * k)
    return cat(cand), cat(c1), cat(c2), cat(rank).astype(F32)


def _peer_route_kernel(q_ref, k1_ref, k2_ref, gate_ref, e1_ref, e2_ref):
    k1 = k1_ref[...]
    k2 = k2_ref[...]
    gates, e1s, e2s = [], [], []
    for h in range(PEER_HEADS):
        qa = q_ref[:, (2 * h) * PEER_HALF:(2 * h + 1) * PEER_HALF].astype(BF16)
        qb = q_ref[:, (2 * h + 1) * PEER_HALF:(2 * h + 2) * PEER_HALF].astype(BF16)
        v1, i1 = _topk_rows(_dot_nt(k1, qa), PEER_TOPK)
        v2, i2 = _topk_rows(_dot_nt(k2, qb), PEER_TOPK)
        cand, c1, c2, rank = _pruned_pair_candidates(v1, i1, v2, i2)
        sc, ea, eb = [], [], []
        for _ in range(PEER_TOPK):
            m = jnp.max(cand, axis=0, keepdims=True)
            first = jnp.min(jnp.where(cand == m, rank, float(PEER_TOPK * PEER_TOPK)), axis=0, keepdims=True)
            sel = rank == first
            sc.append(m)
            ea.append(jnp.sum(jnp.where(sel, c1, 0.0), axis=0, keepdims=True))
            eb.append(jnp.sum(jnp.where(sel, c2, 0.0), axis=0, keepdims=True))
            cand = jnp.where(sel, -jnp.inf, cand)
        scm = jnp.concatenate(sc, axis=0)
        ex = jnp.exp(scm - sc[0])
        gates.append(ex / jnp.sum(ex, axis=0, keepdims=True))
        e1s.append(jnp.concatenate(ea, axis=0))
        e2s.append(jnp.concatenate(eb, axis=0))
    gate_ref[...] = jnp.concatenate(gates, axis=0).T
    e1_ref[...] = jnp.concatenate(e1s, axis=0).T
    e2_ref[...] = jnp.concatenate(e2s, axis=0).T


def peer_route(q, k1, k2):
    t = q.shape[0]
    tt = 128
    hk = PEER_HEADS * PEER_TOPK
    out = jax.ShapeDtypeStruct((t, hk), F32)
    ospec = pl.BlockSpec((tt, hk), lambda i: (i, 0))
    kspec = pl.BlockSpec((PEER_KEYS, PEER_HALF), lambda i: (0, 0))
    return pl.pallas_call(
        _peer_route_kernel,
        grid=(t // tt,),
        in_specs=[pl.BlockSpec((tt, q.shape[1]), lambda i: (i, 0)), kspec, kspec],
        out_specs=[ospec, ospec, ospec],
        out_shape=[out, out, out],
        compiler_params=_cparams("parallel"),
        name="peer_route",
    )(q, k1, k2)


def _peer_w_kernel(gate_ref, e1_ref, e2_ref, w_ref):
    tb = gate_ref.shape[0]
    sub = lax.broadcasted_iota(jnp.int32, (PEER_KEYS, gate_ref.shape[1]), 0).astype(F32)

    def body(t, carry):
        g = gate_ref[pl.ds(t, 1), :]
        a = e1_ref[pl.ds(t, 1), :]
        b = e2_ref[pl.ds(t, 1), :]
        lhs = jnp.where(a == sub, g, 0.0).astype(BF16)
        rhs = jnp.where(b == sub, 1.0, 0.0).astype(BF16)
        w_ref[t] = _dot_nt(lhs, rhs)
        return carry

    lax.fori_loop(0, tb, body, 0, unroll=16)


def peer_dense_weights(gate, e1, e2):
    t, hk = gate.shape
    tb = 128
    spec = pl.BlockSpec((tb, hk), lambda i: (i, 0))
    return pl.pallas_call(
        _peer_w_kernel,
        grid=(t // tb,),
        in_specs=[spec, spec, spec],
        out_specs=pl.BlockSpec((tb, PEER_KEYS, PEER_KEYS), lambda i: (i, 0, 0)),
        out_shape=jax.ShapeDtypeStruct((t, PEER_KEYS, PEER_KEYS), F32),
        compiler_params=_cparams("parallel"),
        name="peer_dense_weights",
    )(gate, e1, e2)


PEER_KEY1_PER_STEP = 8


def _peer_ffn_kernel(x_ref, w_ref, u_ref, v_ref, r_ref, o_ref):
    e = pl.program_id(1)

    @pl.when(e == 0)
    def _():
        o_ref[...] = r_ref[...]

    a = _dot_nt(x_ref[...], u_ref[...])
    act = 0.5 * a * (1.0 + lax.erf(a * (2.0 ** -0.5)))
    hmat = jnp.concatenate(
        [(act[:, i * PEER_KEYS:(i + 1) * PEER_KEYS] * w_ref[:, i, :]).astype(BF16)
         for i in range(PEER_KEY1_PER_STEP)], axis=1)
    o_ref[...] += jnp.dot(hmat, v_ref[...], preferred_element_type=F32)


def peer_ffn_dense(xn, w, u_tab, v_tab, layer, resid):
    t, d = xn.shape
    n_exp = u_tab.shape[1]
    tb = _pick_tile(t, MXU_ROW_TILES)
    eb = PEER_KEY1_PER_STEP * PEER_KEYS
    return pl.pallas_call(
        _peer_ffn_kernel,
        grid=(t // tb, n_exp // eb),
        in_specs=[pl.BlockSpec((tb, d), lambda i, e: (i, 0), pipeline_mode=ONCE),
                  pl.BlockSpec((tb, PEER_KEY1_PER_STEP, PEER_KEYS), lambda i, e: (i, e, 0)),
                  pl.BlockSpec((None, eb, d), lambda i, e: (layer, e, 0)),
                  pl.BlockSpec((None, eb, d), lambda i, e: (layer, e, 0)),
                  pl.BlockSpec((tb, d), lambda i, e: (i, 0), pipeline_mode=ONCE)],
        out_specs=pl.BlockSpec((tb, d), lambda i, e: (i, 0), pipeline_mode=ONCE),
        out_shape=jax.ShapeDtypeStruct((t, d), F32),
        compiler_params=_cparams("parallel", "arbitrary"),
        name="peer_ffn_dense",
    )(xn, w, u_tab, v_tab, resid)


def peer_layer(x, g_ffn, wq, k1, k2, u_tab, v_tab, layer):
    q, xn = norm_matmul(x, g_ffn, wq, emit_h=True, name="peer_query")
    gate, e1, e2 = peer_route(q, k1, k2)
    w = peer_dense_weights(gate, e1, e2)
    return peer_ffn_dense(xn, w, u_tab, v_tab, layer, x)


def _group_norm_kernel(x_ref, bd_ref, g_ref, *out_refs, inv_group, scales):
    x = x_ref[...]
    sq = x * x
    hi = sq.astype(BF16)
    lo = (sq - hi.astype(F32)).astype(BF16)
    bd = bd_ref[...]
    ss = jnp.dot(hi, bd, preferred_element_type=F32) + jnp.dot(lo, bd, preferred_element_type=F32)
    y = x * lax.rsqrt(ss * inv_group + NORM_EPS) * g_ref[...]
    for o_ref, sc in zip(out_refs, scales):
        o_ref[...] = (y if sc == 1.0 else y * sc).astype(o_ref.dtype)


def group_norm(src, col0, width, group, gain, outs):
    t = src.shape[0]
    assert col0 % width == 0 and width % group == 0
    tm = _pick_tile(t, (528, 512, 256, 128))
    lane = jnp.arange(width, dtype=jnp.int32) // group
    bd = (lane[:, None] == lane[None, :]).astype(BF16)
    g = jnp.tile(gain.astype(F32), width // group).reshape(1, width)
    ospec = pl.BlockSpec((tm, width), lambda i: (i, 0))
    res = pl.pallas_call(
        functools.partial(_group_norm_kernel, inv_group=1.0 / group, scales=tuple(s for _, s in outs)),
        grid=(t // tm,),
        in_specs=[pl.BlockSpec((tm, width), lambda i: (i, col0 // width)),
                  pl.BlockSpec((width, width), lambda i: (0, 0)),
                  pl.BlockSpec((1, width), lambda i: (0, 0))],
        out_specs=[ospec] * len(outs),
        out_shape=[jax.ShapeDtypeStruct((t, width), dt) for dt, _ in outs],
        compiler_params=_cparams("parallel"),
        name="group_norm",
    )(src, bd, g)
    return res


CONV_HALO = 8


def _conv_kernel(x_ref, halo_ref, w_ref, b_ref, o_ref, sc, *, first_tile_has_no_history):
    tm = x_ref.shape[0]
    halo = halo_ref[...]
    if first_tile_has_no_history:
        halo = jnp.where(pl.program_id(0) == 0, 0.0, halo)
    sc[0:CONV_HALO, :] = halo
    sc[CONV_HALO:, :] = x_ref[...]
    acc = b_ref[...]
    for j in range(SSM_CONV):
        acc = acc + w_ref[j:j + 1, :] * sc[pl.ds(CONV_HALO - (SSM_CONV - 1 - j), tm), :]
    o_ref[...] = jax.nn.silu(acc)


def causal_conv_silu(src, col0, width, row0, rows, tm, tc, halo_src, conv_w, conv_b):
    assert col0 % tc == 0 and width % tc == 0 and row0 % tm == 0 and rows % tm == 0 and tm % CONV_HALO == 0
    cb, rb, hb = col0 // tc, row0 // tm, tm // CONV_HALO
    if halo_src is None:
        halo_arr = src
        halo_spec = pl.BlockSpec((CONV_HALO, tc), lambda i, j: (jnp.maximum((rb + i) * hb - 1, 0), cb + j))
    else:
        halo_arr = halo_src
        halo_spec = pl.BlockSpec((CONV_HALO, tc), lambda i, j: (i, j))
    return pl.pallas_call(
        functools.partial(_conv_kernel, first_tile_has_no_history=halo_src is None),
        grid=(rows // tm, width // tc),
        in_specs=[pl.BlockSpec((tm, tc), lambda i, j: (rb + i, cb + j)),
                  halo_spec,
                  pl.BlockSpec((SSM_CONV, tc), lambda i, j: (0, j)),
                  pl.BlockSpec((1, tc), lambda i, j: (0, j))],
        out_specs=pl.BlockSpec((tm, tc), lambda i, j: (i, j)),
        out_shape=jax.ShapeDtypeStruct((rows, width), F32),
        scratch_shapes=[pltpu.VMEM((tm + CONV_HALO, tc), F32)],
        compiler_params=_cparams("parallel", "parallel"),
        name="causal_conv_silu",
    )(src, halo_arr, conv_w.astype(F32), conv_b.reshape(1, width).astype(F32))


SSM_INNER = SSM_HEADS * SSM_HEAD_DIM
SSM_GN = SSM_GROUPS * SSM_STATE
HEADS_PER_GROUP = SSM_HEADS // SSM_GROUPS
GROUP_W = HEADS_PER_GROUP * SSM_HEAD_DIM


def _ssd_kernel(xs_ref, b_ref, c_ref, z_ref, dt_ref, dtt_ref, dtb_row_ref, dtb_col_ref, a_row_ref, a_col_ref,
                e_ref, dsk_ref, g_ref, h0_ref, y_ref, hout_ref, h_sc, y_sc, *pad_scs, valid):
    q = SSM_CHUNK
    hi = lax.Precision.HIGHEST
    c = pl.program_id(1)

    @pl.when(c == 0)
    def _():
        h_sc[...] = h0_ref[...]

    def rows(ref, sc):
        if valid == q:
            return ref[...]
        sc[...] = jnp.zeros(sc.shape, F32)
        sc[0:valid, :] = ref[...]
        return sc[...]

    if valid == q:
        pad_scs = (None,) * 5
    xs = rows(xs_ref, pad_scs[0])
    bm = rows(b_ref, pad_scs[1])
    cm = rows(c_ref, pad_scs[2])
    z = rows(z_ref, pad_scs[3])
    dt_raw = rows(dt_ref, pad_scs[4])

    row_i = lax.broadcasted_iota(jnp.int32, (q, q), 0)
    col_i = lax.broadcasted_iota(jnp.int32, (q, q), 1)
    causal = row_i >= col_i
    dt = jax.nn.softplus(dt_raw + dtb_row_ref[...])
    dtt = jax.nn.softplus(dtt_ref[...] + dtb_col_ref[...])
    if valid < q:
        dt = jnp.where(row_i < valid, dt, 0.0)
        dtt = jnp.where(lax.broadcasted_iota(jnp.int32, dtt.shape, 1) < valid, dtt, 0.0)
    acs = jnp.dot(causal.astype(F32), dt * a_row_ref[...], precision=hi, preferred_element_type=F32)
    acst = jnp.dot(dtt * a_col_ref[...], (row_i <= col_i).astype(F32), precision=hi,
                   preferred_element_type=F32)
    per_head = jnp.concatenate([dt, jnp.exp(acs[q - 1:q, :] - acs), jnp.exp(acs)], axis=0)
    expand = e_ref[...]
    piece = per_head.astype(BF16)
    expanded = jnp.dot(piece, expand, preferred_element_type=F32)
    rest = per_head - piece.astype(F32)
    for _ in range(2):
        piece = rest.astype(BF16)
        expanded = expanded + jnp.dot(piece, expand, preferred_element_type=F32)
        rest = rest - piece.astype(F32)
    dt_e, dec_e, eacs_e = expanded[0:q], expanded[q:2 * q], expanded[2 * q:3 * q]
    xdt = xs * dt_e
    xdd = xdt * dec_e
    xdt_b = xdt.astype(BF16)
    lane = lax.broadcasted_iota(jnp.int32, (q, 2 * SSM_HEAD_DIM), 1)

    for g in range(SSM_GROUPS):
        bg = bm[:, g * SSM_STATE:(g + 1) * SSM_STATE].astype(BF16)
        cg = cm[:, g * SSM_STATE:(g + 1) * SSM_STATE].astype(BF16)
        cb = _dot_nt(cg, bg)
        for pair in range(HEADS_PER_GROUP // 2):
            h_a = g * HEADS_PER_GROUP + 2 * pair
            slab = slice(h_a * SSM_HEAD_DIM, (h_a + 2) * SSM_HEAD_DIM)
            xpair = xdt_b[:, slab]
            ypair = jnp.zeros((q, 2 * SSM_HEAD_DIM), F32)
            for which in range(2):
                h = h_a + which
                seg = acs[:, h:h + 1] - acst[h:h + 1, :]
                lmat = jnp.exp(jnp.where(causal, seg, -jnp.inf))
                mine = (lane >= which * SSM_HEAD_DIM) & (lane < (which + 1) * SSM_HEAD_DIM)
                ypair = ypair + jnp.dot((cb * lmat).astype(BF16), jnp.where(mine, xpair, jnp.zeros_like(xpair)),
                                        preferred_element_type=F32)
            y_sc[:, slab] = ypair
        gs = slice(g * GROUP_W, (g + 1) * GROUP_W)
        y_sc[:, gs] += _dot_nt(cg, h_sc[gs, :].astype(BF16)) * eacs_e[:, gs]
        st = jnp.dot(xdd[:, gs].T.astype(BF16), bg, preferred_element_type=F32)
        for hh in range(HEADS_PER_GROUP):
            h = g * HEADS_PER_GROUP + hh
            hs = slice(h * SSM_HEAD_DIM, (h + 1) * SSM_HEAD_DIM)
            dec = jnp.exp(acst[h:h + 1, q - 1:q])
            h_sc[hs, :] = h_sc[hs, :] * dec + st[hh * SSM_HEAD_DIM:(hh + 1) * SSM_HEAD_DIM, :]

    y = y_sc[...] + dsk_ref[...] * xs
    y = y * jax.nn.silu(z)
    gain = g_ref[...]
    for g in range(SSM_GROUPS):
        gs = slice(g * GROUP_W, (g + 1) * GROUP_W)
        yg = y[:, gs]
        yn = yg * lax.rsqrt(jnp.mean(yg * yg, axis=-1, keepdims=True) + NORM_EPS) * gain[:, gs]
        y_ref[:, gs] = yn[0:valid].astype(y_ref.dtype)

    @pl.when(c == pl.num_programs(1) - 1)
    def _():
        hout_ref[...] = h_sc[...]


def ssd_branch(xc, u, z_col0, u_row0, dt_pad, dt_t, h0, h0_blk0, n_batch, n_chunks, valid, dt_bias, a_log, d_skip,
               g_ssm, out_dtype):
    q = SSM_CHUNK
    rows = xc.shape[0]
    assert rows == n_batch * n_chunks * valid and u_row0 % valid == 0 and z_col0 % SSM_INNER == 0
    rb = u_row0 // valid
    step = lambda b, c: b * n_chunks + c
    pad128 = lambda v: jnp.pad(v.astype(F32), (0, 128 - SSM_HEADS))
    a = -jnp.exp(a_log.astype(F32))
    head_lane = jnp.arange(SSM_INNER, dtype=jnp.int32) // SSM_HEAD_DIM
    expand = (jnp.arange(128, dtype=jnp.int32)[:, None] == head_lane[None, :]).astype(BF16)
    const = lambda shape: pl.BlockSpec(shape, lambda b, c: (0,) * len(shape))
    pad_scs = [] if valid == q else [pltpu.VMEM((q, w), F32) for w in (SSM_INNER, SSM_GN, SSM_GN, SSM_INNER, 128)]
    y, h_fin = pl.pallas_call(
        functools.partial(_ssd_kernel, valid=valid),
        grid=(n_batch, n_chunks),
        in_specs=[
            pl.BlockSpec((valid, SSM_INNER), lambda b, c: (step(b, c), 0)),
            pl.BlockSpec((valid, SSM_GN), lambda b, c: (step(b, c), SSM_INNER // SSM_GN)),
            pl.BlockSpec((valid, SSM_GN), lambda b, c: (step(b, c), SSM_INNER // SSM_GN + 1)),
            pl.BlockSpec((valid, SSM_INNER), lambda b, c: (rb + step(b, c), z_col0 // SSM_INNER)),
            pl.BlockSpec((valid, 128), lambda b, c: (rb + step(b, c), 0)),
            pl.BlockSpec((None, SSM_HEADS, q), lambda b, c: (step(b, c), 0, 0)),
            const((1, 128)), const((SSM_HEADS, q)), const((1, 128)), const((SSM_HEADS, q)),
            const((128, SSM_INNER)), const((1, SSM_INNER)), const((1, SSM_INNER)),
            pl.BlockSpec((None, SSM_INNER, SSM_STATE), lambda b, c: (h0_blk0 + b, 0, 0)),
        ],
        out_specs=[pl.BlockSpec((valid, SSM_INNER), lambda b, c: (step(b, c), 0)),
                   pl.BlockSpec((None, SSM_INNER, SSM_STATE), lambda b, c: (b, 0, 0))],
        out_shape=[jax.ShapeDtypeStruct((rows, SSM_INNER), out_dtype),
                   jax.ShapeDtypeStruct((n_batch, SSM_INNER, SSM_STATE), F32)],
        scratch_shapes=[pltpu.VMEM((SSM_INNER, SSM_STATE), F32), pltpu.VMEM((q, SSM_INNER), F32)] + pad_scs,
        compiler_params=_cparams("parallel", "arbitrary"),
        name="ssd_branch",
    )(xc, xc, xc, u, dt_pad, dt_t,
      pad128(dt_bias).reshape(1, 128), jnp.broadcast_to(dt_bias.astype(F32)[:, None], (SSM_HEADS, q)),
      pad128(a).reshape(1, 128), jnp.broadcast_to(a[:, None], (SSM_HEADS, q)),
      expand, jnp.repeat(d_skip.astype(F32), SSM_HEAD_DIM).reshape(1, SSM_INNER),
      g_ssm.astype(F32).reshape(1, SSM_INNER), h0)
    return y, h_fin


def _mem_attn_kernel(q_ref, k_ref, v_ref, o_ref):
    for h in range(MEM_HEADS):
        hs = slice(h * MEM_HEAD_DIM, (h + 1) * MEM_HEAD_DIM)
        s = _dot_nt(q_ref[:, hs].astype(BF16), k_ref[:, hs].astype(BF16))
        p = jnp.exp(s - jnp.max(s, axis=-1, keepdims=True))
        o = jnp.dot(p.astype(BF16), v_ref[:, hs].astype(BF16), preferred_element_type=F32)
        o_ref[:, hs] = (o / jnp.sum(p, axis=-1, keepdims=True)).astype(o_ref.dtype)


def memory_attention(qn, q_row0, rows, tq, mk, mv, n_mem, kv_blk0, v_col_blk, tiles_per_batch, out_dtype):
    rb = q_row0 // tq
    return pl.pallas_call(
        _mem_attn_kernel,
        grid=(rows // tq,),
        in_specs=[pl.BlockSpec((tq, MEM_W), lambda i: (rb + i, 0)),
                  pl.BlockSpec((n_mem, MEM_W), lambda i: (kv_blk0 + i // tiles_per_batch, 0)),
                  pl.BlockSpec((n_mem, MEM_W), lambda i: (kv_blk0 + i // tiles_per_batch, v_col_blk))],
        out_specs=pl.BlockSpec((tq, MEM_W), lambda i: (i, 0)),
        out_shape=jax.ShapeDtypeStruct((rows, MEM_W), out_dtype),
        compiler_params=_cparams("parallel"),
        name="memory_attention",
    )(qn, mk, mv)


SAMPLE_PAGES_PER_STEP = 16
SAMPLE_ROWS = DA_HEADS * 2 * 8


def _sample_attn_kernel(pt_ref, lam_ref, q_ref, *refs, pages_per_step, post_scale):
    n = pages_per_step
    kt_refs, v_refs = refs[:n], refs[n:2 * n]
    knew_ref, vnew_ref, blast_ref, bnew_ref, rexp_ref, hmask_ref, g_ref, o_ref, m_sc, l_sc, acc_sc = refs[2 * n:]
    del pt_ref
    step = pl.program_id(1)
    n_new = o_ref.shape[0]
    rows_per_head = 2 * n_new

    @pl.when(step == 0)
    def _():
        m_sc[...] = jnp.full(m_sc.shape, NEG_BIG, F32)
        l_sc[...] = jnp.zeros(l_sc.shape, F32)
        acc_sc[...] = jnp.zeros(acc_sc.shape, F32)

    q = q_ref[...]

    def update(pages):
        logits = []
        for kt_ref, _, bias in pages:
            s = jnp.dot(q, kt_ref[...].astype(BF16), preferred_element_type=F32)
            logits.append(s if bias is None else s + bias)
        s_max = logits[0]
        for s in logits[1:]:
            s_max = jnp.maximum(s_max, s)
        m_prev = m_sc[...]
        m_new = jnp.maximum(m_prev, jnp.max(s_max, axis=-1, keepdims=True))
        alpha = jnp.exp2(m_prev - m_new)
        m_sc[...] = m_new
        l_new = alpha * l_sc[...]
        acc = alpha * acc_sc[...]
        rexp = rexp_ref[...]
        own_head = hmask_ref[...]
        for (_, v_ref, _), s in zip(pages, logits):
            p = jnp.exp2(s - m_new)
            l_new = l_new + p
            p_rep = jnp.dot(p.astype(BF16), rexp, preferred_element_type=F32)
            acc = acc + jnp.dot(p_rep.astype(BF16) * own_head, v_ref[...].astype(BF16),
                                preferred_element_type=F32)
        l_sc[...] = l_new
        acc_sc[...] = acc

    last = pl.num_programs(1) - 1
    past = [(kt_refs[i], v_refs[i], None) for i in range(n)]

    @pl.when(step < last)
    def _():
        update(past)

    @pl.when(step == last)
    def _():
        update(past[:-1] + [(kt_refs[n - 1], v_refs[n - 1], blast_ref[...]), (knew_ref, vnew_ref, bnew_ref[...])])
        attn = acc_sc[...] / jnp.sum(l_sc[...], axis=-1, keepdims=True)
        for h in range(DA_HEADS):
            r0 = h * rows_per_head
            o = attn[r0:r0 + n_new, :] - lam_ref[0] * attn[r0 + n_new:r0 + 2 * n_new, :]
            y = o * lax.rsqrt(jnp.mean(o * o, axis=-1, keepdims=True) + NORM_EPS)
            o_ref[:, h * DA_DV:(h + 1) * DA_DV] = (y * g_ref[...]) * post_scale


def sample_bias_tiles(table, n_new, page):
    def rows(off):
        b = _toeplitz_bias(table, off, page)[:, :n_new, :]
        return jnp.broadcast_to(b[:, None], (DA_HEADS, 2, n_new, page)).reshape(SAMPLE_ROWS, page)
    return rows(page), rows(0)


def sample_diff_attention(layer, qn_s, kn_s, v_s, lam, k_pool_t, v_pool, page_table, b_last, b_new, g_sub,
                          post_scale):
    bsz, n_new, qw = qn_s.shape
    page = v_pool.shape[2]
    n_pages = page_table.shape[1]
    n = SAMPLE_PAGES_PER_STEP
    assert n_pages % n == 0 and SAMPLE_ROWS == DA_HEADS * 2 * n_new
    qg = qn_s.reshape(bsz, n_new, DA_HEADS * 2, DA_DK)
    eye = jnp.eye(DA_HEADS * 2, dtype=F32)
    q_rows = (qg.transpose(0, 2, 1, 3)[:, :, :, None, :] * eye[None, :, None, :, None]).reshape(
        bsz, SAMPLE_ROWS, qw).astype(BF16)
    knew_t = jnp.pad(kn_s.transpose(0, 2, 1), ((0, 0), (0, 0), (0, page - n_new)))
    vnew = jnp.pad(v_s, ((0, 0), (0, page - n_new), (0, 0), (0, 0))).reshape(bsz, page * DA_HEADS, DA_DV)
    v_rows = v_pool.reshape(v_pool.shape[0], v_pool.shape[1], page * DA_HEADS, DA_DV)
    col = jnp.arange(page * DA_HEADS, dtype=jnp.int32)
    rexp = (col[None, :] // DA_HEADS == jnp.arange(page, dtype=jnp.int32)[:, None]).astype(BF16)
    hmask = (col[None, :] % DA_HEADS
             == jnp.arange(SAMPLE_ROWS, dtype=jnp.int32)[:, None] // (2 * n_new)).astype(BF16)
    kspec = lambda i: pl.BlockSpec((None, None, qw, page), lambda b, s, pt: (layer, pt[b, s * n + i], 0, 0))
    vspec = lambda i: pl.BlockSpec((None, None, page * DA_HEADS, DA_DV),
                                   lambda b, s, pt: (layer, pt[b, s * n + i], 0, 0))
    const2 = lambda shape: pl.BlockSpec(shape, lambda b, s, pt: (0, 0))
    grid_spec = pltpu.PrefetchScalarGridSpec(
        num_scalar_prefetch=1,
        grid=(bsz, n_pages // n),
        in_specs=[pl.BlockSpec(memory_space=pltpu.SMEM),
                  pl.BlockSpec((None, SAMPLE_ROWS, qw), lambda b, s, pt: (b, 0, 0))]
                 + [kspec(i) for i in range(n)] + [vspec(i) for i in range(n)]
                 + [pl.BlockSpec((None, qw, page), lambda b, s, pt: (b, 0, 0)),
                    pl.BlockSpec((None, page * DA_HEADS, DA_DV), lambda b, s, pt: (b, 0, 0)),
                    const2((SAMPLE_ROWS, page)), const2((SAMPLE_ROWS, page)),
                    const2((page, page * DA_HEADS)), const2((SAMPLE_ROWS, page * DA_HEADS)), const2((1, DA_DV))],
        out_specs=pl.BlockSpec((n_new, DA_VW), lambda b, s, pt: (b, 0)),
        scratch_shapes=[pltpu.VMEM((SAMPLE_ROWS, page), F32), pltpu.VMEM((SAMPLE_ROWS, page), F32),
                        pltpu.VMEM((SAMPLE_ROWS, DA_DV), F32)],
    )
    return pl.pallas_call(
        functools.partial(_sample_attn_kernel, pages_per_step=n, post_scale=post_scale),
        grid_spec=grid_spec,
        out_shape=jax.ShapeDtypeStruct((bsz * n_new, DA_VW), F32),
        compiler_params=_cparams("parallel", "arbitrary"),
        name="sample_diff_attention",
    )(page_table, lam.reshape(1).astype(F32), q_rows, *([k_pool_t] * n), *([v_rows] * n), knew_t, vnew,
      b_last, b_new, rexp, hmask, g_sub.reshape(1, DA_DV).astype(F32))


FA_TILE = 512


def kernel(x_prompt, x_sample, cache_attn_k, cache_attn_v, cache_mem_k, cache_mem_v, state_ssm, state_conv, page_table, mem_prompt, rel_bias, g_mix, w_in, g_q, g_k, lam_q1, lam_k1, lam_q2, lam_k2, g_sub, conv_w, conv_b, dt_bias, a_log, d_skip, g_ssm, g_mem, w_mem_kv, g_mq, g_mk, w_br_attn, w_br_ssm, w_br_mem, w_out, g_ffn, peer_wq, peer_k1, peer_k2, peer_u, peer_v):
    depth = w_in.shape[0]
    bp, sp, d = x_prompt.shape
    bs, ts, _ = x_sample.shape
    assert bp == 1 and sp % SSM_CHUNK == 0 and sp % FA_TILE == 0 and ts >= SSM_CONV - 1
    n_p = bp * sp
    n_s = bs * ts
    conv_dim = SSM_INNER + 2 * SSM_GN
    n_mem = mem_prompt.shape[1]
    n_pool, page = cache_attn_v.shape[1], cache_attn_v.shape[2]
    s_z = 2 * DA_QW + DA_VW
    s_dt = s_z + SSM_INNER + conv_dim
    s_mq = s_dt + SSM_HEADS
    c_z, c_xbc = 0, SSM_INNER
    c_q = c_xbc + conv_dim
    c_k = c_q + DA_QW
    c_v = c_k + DA_QW
    c_mq = c_v + DA_VW
    c_gate = c_mq + MEM_W

    x = jnp.concatenate([x_prompt.reshape(n_p, d), x_sample.reshape(n_s, d)], axis=0)
    bias_tiles = prompt_bias_tiles(rel_bias, FA_TILE)
    b_last, b_new = sample_bias_tiles(rel_bias, ts, page)
    k_pool_t = jnp.transpose(cache_attn_k, (0, 1, 3, 4, 5, 2)).reshape(depth, n_pool, DA_QW, page)
    zero_h = jnp.zeros((bp, SSM_INNER, SSM_STATE), F32)
    ssm_states = state_ssm.reshape(depth * bs, SSM_INNER, SSM_STATE)
    mem_k_rows = cache_mem_k.reshape(depth * bs * n_mem, MEM_W)
    mem_v_rows = cache_mem_v.reshape(depth * bs * n_mem, MEM_W)
    peer_u_b = peer_u.astype(BF16)
    peer_v_b = peer_v.astype(BF16)

    outs = {k: [] for k in ("pk", "pv", "pmk", "pmv", "ph", "pc", "sk", "sv", "sh", "sc")}
    for l in range(depth):
        lam_init = 0.8 - 0.6 * math.exp(-0.3 * l)
        lam = (jnp.exp(jnp.sum(lam_q1[l].astype(F32) * lam_k1[l].astype(F32)))
               - jnp.exp(jnp.sum(lam_q2[l].astype(F32) * lam_k2[l].astype(F32))) + lam_init)
        wl = w_in[l]
        w_main = jnp.concatenate([wl[:, s_z:s_dt], wl[:, :s_z], wl[:, s_mq:]], axis=1).astype(BF16)
        w_dt = jnp.pad(wl[:, s_dt:s_mq], ((0, 0), (0, 128 - SSM_HEADS))).astype(BF16)

        u, dt_pad = norm_matmul(x, g_mix[l], w_main, side_w=w_dt, name="in_proj")
        (qn,) = group_norm(u, c_q, DA_QW, DA_DK, g_q[l], [(BF16, DA_SCALE * LOG2E)])
        kn, kn_b = group_norm(u, c_k, DA_QW, DA_DK, g_k[l], [(F32, 1.0), (BF16, 1.0)])
        (mqn,) = group_norm(u, c_mq, MEM_W, MEM_HEAD_DIM, g_mq[l], [(F32, MEM_SCALE)])
        v_all = u[:, c_v:c_v + DA_VW]
        xbc_s = u[n_p:, c_xbc:c_xbc + conv_dim].reshape(bs, ts, conv_dim)

        a_p = prompt_diff_attention(qn, kn_b, u, c_v, n_p, lam, bias_tiles, g_sub[l], 1.0 - lam_init, FA_TILE)
        a_s = sample_diff_attention(l, qn[n_p:].astype(F32).reshape(bs, ts, DA_QW), kn[n_p:].reshape(bs, ts, DA_QW),
                                    v_all[n_p:].reshape(bs, ts, DA_HEADS, DA_DV), lam, k_pool_t, cache_attn_v,
                                    page_table, b_last, b_new, g_sub[l], 1.0 - lam_init)

        (mkv,) = norm_matmul(mem_prompt.reshape(bp * n_mem, d), g_mem[l], w_mem_kv[l].astype(BF16), name="mem_kv")
        (mk_p,) = group_norm(mkv, 0, MEM_W, MEM_HEAD_DIM, g_mk[l], [(F32, 1.0)])
        tq = 512
        m_p = memory_attention(mqn, 0, n_p, tq, mk_p, mkv, n_mem, 0, 1, sp // tq, BF16)
        m_s = memory_attention(mqn, n_p, n_s, ts, mem_k_rows, mem_v_rows, n_mem, l * bs, 0, 1, F32)

        xc_p = causal_conv_silu(u, c_xbc, conv_dim, 0, n_p, 512, 1024, None, conv_w[l], conv_b[l])
        halo_s = jnp.pad(state_conv[l], ((0, 0), (CONV_HALO - (SSM_CONV - 1), 0), (0, 0))).reshape(
            bs * CONV_HALO, conv_dim)
        xc_s = causal_conv_silu(xbc_s.reshape(n_s, conv_dim), 0, conv_dim, 0, n_s, ts, conv_dim, halo_s,
                                conv_w[l], conv_b[l])
        dt_raw = dt_pad[:, :SSM_HEADS]
        dtt_p = dt_raw[:n_p].reshape(n_p // SSM_CHUNK, SSM_CHUNK, SSM_HEADS).transpose(0, 2, 1)
        dtt_s = jnp.pad(dt_raw[n_p:].reshape(bs, ts, SSM_HEADS).transpose(0, 2, 1),
                        ((0, 0), (0, 0), (0, SSM_CHUNK - ts)))
        s_p, h_p = ssd_branch(xc_p, u, c_z, 0, dt_pad, dtt_p, zero_h, 0, bp, sp // SSM_CHUNK, SSM_CHUNK,
                              dt_bias[l], a_log[l], d_skip[l], g_ssm[l], BF16)
        s_s, h_s = ssd_branch(xc_s, u, c_z, n_p, dt_pad, dtt_s, ssm_states, l * bs, bs, 1, ts,
                              dt_bias[l], a_log[l], d_skip[l], g_ssm[l], F32)

        a_out = jnp.concatenate([a_p, a_s.astype(BF16)], axis=0)
        s_out = jnp.concatenate([s_p, s_s.astype(BF16)], axis=0)
        m_out = jnp.concatenate([m_p, m_s.astype(BF16)], axis=0)
        merged = merge_branches(a_out, s_out, m_out, u, c_gate, w_br_attn[l].astype(BF16),
                                w_br_ssm[l].astype(BF16), w_br_mem[l].astype(BF16))
        x = matmul(merged, w_out[l].astype(BF16), residual=x, name="out_proj")
        x = peer_layer(x, g_ffn[l], peer_wq[l].astype(BF16), peer_k1[l].astype(BF16), peer_k2[l].astype(BF16),
                       peer_u_b, peer_v_b, l)

        outs["pk"].append(kn[:n_p].reshape(bp, sp, DA_HEADS, 2, DA_DK))
        outs["pv"].append(v_all[:n_p].reshape(bp, sp, DA_HEADS, DA_DV))
        outs["pmk"].append(mk_p.reshape(bp, n_mem, MEM_HEADS, MEM_HEAD_DIM))
        outs["pmv"].append(mkv[:, MEM_W:].reshape(bp, n_mem, MEM_HEADS, MEM_HEAD_DIM))
        outs["ph"].append(h_p.reshape(bp, SSM_HEADS, SSM_HEAD_DIM, SSM_STATE))
        outs["pc"].append(u[n_p - (SSM_CONV - 1):n_p, c_xbc:c_xbc + conv_dim].reshape(bp, SSM_CONV - 1, conv_dim))
        outs["sk"].append(kn[n_p:].reshape(bs, ts, DA_HEADS, 2, DA_DK))
        outs["sv"].append(v_all[n_p:].reshape(bs, ts, DA_HEADS, DA_DV))
        outs["sh"].append(h_s.reshape(bs, SSM_HEADS, SSM_HEAD_DIM, SSM_STATE))
        outs["sc"].append(xbc_s[:, ts - (SSM_CONV - 1):])

    y_prompt = x[:n_p].reshape(bp, sp, d)
    y_sample = x[n_p:].reshape(bs, ts, d)
    st = lambda name: jnp.stack(outs[name])
    return (y_prompt, y_sample, st("pk"), st("pv"), st("pmk"), st("pmv"), st("ph"), st("pc"),
            st("sk"), st("sv"), st("sh"), st("sc"))
```

```python
import functools
import math

import jax
import jax.numpy as jnp
from jax import lax
from jax.experimental import pallas as pl
from jax.experimental.pallas import tpu as pltpu

F32 = jnp.float32
BF16 = jnp.bfloat16

NORM_EPS = 1e-6
NEG_BIG = -1e30
V7X_VMEM_LIMIT_BYTES = 56 * 1024 * 1024

DA_HEADS, DA_DK, DA_DV = 8, 64, 128
DA_QW = DA_HEADS * 2 * DA_DK
DA_VW = DA_HEADS * DA_DV
DA_SCALE = DA_DK ** -0.5
N_BUCKETS, MAX_DISTANCE = 32, 128
SSM_HEADS, SSM_HEAD_DIM, SSM_GROUPS, SSM_STATE, SSM_CONV, SSM_CHUNK = 32, 64, 4, 128, 4, 128
MEM_HEADS, MEM_HEAD_DIM = 4, 256
MEM_W = MEM_HEADS * MEM_HEAD_DIM
MEM_SCALE = MEM_HEAD_DIM ** -0.5
PEER_HEADS, PEER_KEYS, PEER_TOPK, PEER_HALF = 8, 128, 16, 128


def _cparams(*sem):
    return pltpu.CompilerParams(dimension_semantics=sem, vmem_limit_bytes=V7X_VMEM_LIMIT_BYTES)


def _dot_nt(a, b):
    return lax.dot_general(a, b, (((1,), (1,)), ((), ())), preferred_element_type=F32)


def _pick_tile(n, candidates):
    for c in candidates:
        if n % c == 0:
            return c
    return n


MXU_ROW_TILES = (1056, 1024, 512, 256, 128)
ONCE = pl.Buffered(1)


def _norm_mm_kernel(x_ref, g_ref, w_ref, *rest, has_side, emit_h):
    rest = list(rest)
    side_w_ref = rest.pop(0) if has_side else None
    o_ref = rest.pop(0)
    side_o_ref = rest.pop(0) if has_side else None
    h_o_ref = rest.pop(0) if emit_h else None
    (h_sc,) = rest

    @pl.when(pl.program_id(1) == 0)
    def _():
        x = x_ref[...]
        ms = jnp.mean(x * x, axis=-1, keepdims=True)
        h = (x * lax.rsqrt(ms + NORM_EPS) * g_ref[...]).astype(BF16)
        h_sc[...] = h
        if has_side:
            side_o_ref[...] = jnp.dot(h, side_w_ref[...], preferred_element_type=F32)
        if emit_h:
            h_o_ref[...] = h

    o_ref[...] = jnp.dot(h_sc[...], w_ref[...], preferred_element_type=F32)


def norm_matmul(x, g, w, side_w=None, emit_h=False, name="norm_matmul"):
    t, d = x.shape
    n = w.shape[1]
    tm = _pick_tile(t, MXU_ROW_TILES)
    tn = _pick_tile(n, (1536, 1024, 512, 256, 128))
    row = lambda width: pl.BlockSpec((tm, width), lambda i, j: (i, 0))
    in_specs = [pl.BlockSpec((tm, d), lambda i, j: (i, 0), pipeline_mode=ONCE),
                pl.BlockSpec((1, d), lambda i, j: (0, 0)), pl.BlockSpec((d, tn), lambda i, j: (0, j))]
    args = [x, g.reshape(1, d).astype(F32), w]
    out_specs = [pl.BlockSpec((tm, tn), lambda i, j: (i, j))]
    out_shape = [jax.ShapeDtypeStruct((t, n), F32)]
    if side_w is not None:
        ns = side_w.shape[1]
        in_specs.append(pl.BlockSpec((d, ns), lambda i, j: (0, 0)))
        args.append(side_w)
        out_specs.append(row(ns))
        out_shape.append(jax.ShapeDtypeStruct((t, ns), F32))
    if emit_h:
        out_specs.append(row(d))
        out_shape.append(jax.ShapeDtypeStruct((t, d), BF16))
    return pl.pallas_call(
        functools.partial(_norm_mm_kernel, has_side=side_w is not None, emit_h=emit_h),
        grid=(t // tm, n // tn),
        in_specs=in_specs,
        out_specs=out_specs,
        out_shape=out_shape,
        scratch_shapes=[pltpu.VMEM((tm, d), BF16)],
        compiler_params=_cparams("parallel", "arbitrary"),
        name=name,
    )(*args)


def _mm_kernel(a_ref, w_ref, o_ref):
    o_ref[...] = jnp.dot(a_ref[...], w_ref[...], preferred_element_type=F32).astype(o_ref.dtype)


def _mm_res_kernel(a_ref, w_ref, r_ref, o_ref):
    o_ref[...] = r_ref[...] + jnp.dot(a_ref[...], w_ref[...], preferred_element_type=F32)


def matmul(a, w, residual=None, out_dtype=F32, name="matmul"):
    t, k = a.shape
    n = w.shape[1]
    tm = _pick_tile(t, MXU_ROW_TILES)
    tn = _pick_tile(n, (1024, 512, 256, 128))
    in_specs = [pl.BlockSpec((tm, k), lambda i, j: (i, 0)), pl.BlockSpec((k, tn), lambda i, j: (0, j))]
    args = [a, w]
    body = _mm_kernel
    if residual is not None:
        in_specs.append(pl.BlockSpec((tm, tn), lambda i, j: (i, j)))
        args.append(residual)
        body = _mm_res_kernel
    return pl.pallas_call(
        body,
        grid=(t // tm, n // tn),
        in_specs=in_specs,
        out_specs=pl.BlockSpec((tm, tn), lambda i, j: (i, j)),
        out_shape=jax.ShapeDtypeStruct((t, n), out_dtype),
        compiler_params=_cparams("parallel", "parallel"),
        name=name,
    )(*args)


def _merge_kernel(a_ref, s_ref, m_ref, ga_ref, gs_ref, gm_ref, wa_ref, ws_ref, wm_ref, o_ref):
    acc = jax.nn.sigmoid(ga_ref[...]) * jnp.dot(a_ref[...], wa_ref[...], preferred_element_type=F32)
    acc += jax.nn.sigmoid(gs_ref[...]) * jnp.dot(s_ref[...], ws_ref[...], preferred_element_type=F32)
    acc += jax.nn.sigmoid(gm_ref[...]) * jnp.dot(m_ref[...], wm_ref[...], preferred_element_type=F32)
    o_ref[...] = acc.astype(o_ref.dtype)


def merge_branches(a, s, m, u, gate_col0, wa, ws, wm):
    t = a.shape[0]
    d = wa.shape[1]
    tm = _pick_tile(t, MXU_ROW_TILES)
    tn = 512
    gb = gate_col0 // tn
    nd = d // tn
    row = lambda width: pl.BlockSpec((tm, width), lambda i, j: (i, 0))
    gate = lambda which: pl.BlockSpec((tm, tn), lambda i, j: (i, gb + which * nd + j))
    wcol = lambda kdim: pl.BlockSpec((kdim, tn), lambda i, j: (0, j))
    return pl.pallas_call(
        _merge_kernel,
        grid=(t // tm, nd),
        in_specs=[row(a.shape[1]), row(s.shape[1]), row(m.shape[1]), gate(0), gate(1), gate(2),
                  wcol(wa.shape[0]), wcol(ws.shape[0]), wcol(wm.shape[0])],
        out_specs=pl.BlockSpec((tm, tn), lambda i, j: (i, j)),
        out_shape=jax.ShapeDtypeStruct((t, d), BF16),
        compiler_params=_cparams("parallel", "parallel"),
        name="merge_branches",
    )(a, s, m, u, u, u, wa, ws, wm)


def _t5_bucket(dist):
    n = jnp.maximum(dist, 0)
    exact = N_BUCKETS // 2
    nf = jnp.maximum(n, exact).astype(F32)
    large = exact + (jnp.log(nf / exact) / math.log(MAX_DISTANCE / exact) * (N_BUCKETS - exact)).astype(jnp.int32)
    return jnp.where(n < exact, n, jnp.minimum(large, N_BUCKETS - 1))


def _far_bias(table):
    return table.astype(F32)[N_BUCKETS - 1]


LOG2E = math.log2(math.e)


def _toeplitz_bias(table, off, t):
    x = jnp.arange(2 * t, dtype=jnp.int32)
    dist = jnp.where(x < t, off - x, off + 2 * t - x)
    far = _far_bias(table)
    v = (jnp.moveaxis(table.astype(F32)[_t5_bucket(dist)], -1, 0) - far[:, None]) * LOG2E
    v = jnp.where(dist[None, :] >= 0, v, NEG_BIG)
    h = v.shape[0]
    return jnp.tile(v, (1, t))[:, :t * (2 * t - 1)].reshape(h, t, 2 * t - 1)[:, :, :t]


def prompt_bias_tiles(table, tile):
    assert tile >= MAX_DISTANCE
    return jnp.stack([_toeplitz_bias(table, 0, tile), _toeplitz_bias(table, tile, tile)], axis=1)


FA_HEADS_PER_STEP = 4


def _fa_kernel(qt_ref, kt_ref, lam_ref, q_ref, k_ref, v_ref, b_ref, g_ref, o_ref, m_sc, l_sc, acc_sc, *,
               post_scale):
    s_id = pl.program_id(1)
    qi = qt_ref[s_id]
    ki = kt_ref[s_id]

    @pl.when(ki == 0)
    def _():
        m_sc[...] = jnp.full(m_sc.shape, NEG_BIG, F32)
        l_sc[...] = jnp.zeros(l_sc.shape, F32)
        acc_sc[...] = jnp.zeros(acc_sc.shape, F32)

    def update(bias_idx):
        n_lane_tiles = k_ref.shape[0] // 128
        lane = lax.broadcasted_iota(jnp.int32, (q_ref.shape[0], 2 * DA_DK), 1)
        for hh in range(FA_HEADS_PER_STEP):
            hs = slice(hh * 2 * DA_DK, (hh + 1) * 2 * DA_DK)
            q = q_ref[:, hs]
            k = k_ref[:, hs]
            v = v_ref[:, hh * DA_DV:(hh + 1) * DA_DV].astype(BF16)
            logits = []
            for m in range(2):
                in_map = (lane >= m * DA_DK) & (lane < (m + 1) * DA_DK)
                s = _dot_nt(jnp.where(in_map, q, jnp.zeros_like(q)), k)
                logits.append(s if bias_idx is None else s + b_ref[hh, bias_idx])
            for m in range(2):
                s = logits[m]
                slot = 2 * hh + m
                m_prev = m_sc[slot]
                m_new = jnp.maximum(m_prev, jnp.max(s, axis=-1, keepdims=True))
                alpha = jnp.exp2(m_prev - m_new)
                p = jnp.exp2(s - jnp.tile(m_new, (1, n_lane_tiles)))
                l_part = p[:, 0:128]
                for t in range(1, n_lane_tiles):
                    l_part = l_part + p[:, t * 128:(t + 1) * 128]
                l_sc[slot] = alpha * l_sc[slot] + l_part
                acc_sc[slot] = alpha * acc_sc[slot] + jnp.dot(p.astype(BF16), v, preferred_element_type=F32)
                m_sc[slot] = m_new

    @pl.when(ki < qi - 1)
    def _():
        update(None)

    @pl.when(ki == qi - 1)
    def _():
        update(1)

    @pl.when(ki == qi)
    def _():
        update(0)
        for hh in range(FA_HEADS_PER_STEP):
            l0 = jnp.sum(l_sc[2 * hh], axis=-1, keepdims=True)
            l1 = jnp.sum(l_sc[2 * hh + 1], axis=-1, keepdims=True)
            o = acc_sc[2 * hh] / l0 - lam_ref[0] * (acc_sc[2 * hh + 1] / l1)
            y = o * lax.rsqrt(jnp.mean(o * o, axis=-1, keepdims=True) + NORM_EPS)
            o_ref[:, hh * DA_DV:(hh + 1) * DA_DV] = ((y * g_ref[...]) * post_scale).astype(o_ref.dtype)


def prompt_diff_attention(qn, kn, v_src, v_col0, s_len, lam, bias_tiles, g_sub, post_scale, tile):
    hp = FA_HEADS_PER_STEP
    width = hp * DA_DV
    assert v_col0 % width == 0 and DA_HEADS % hp == 0 and 2 * DA_DK == DA_DV
    vb = v_col0 // width
    nq = s_len // tile
    pairs = [(qi, ki) for qi in range(nq) for ki in range(qi + 1)]
    qt = jnp.asarray([p[0] for p in pairs], jnp.int32)
    kt = jnp.asarray([p[1] for p in pairs], jnp.int32)
    grid_spec = pltpu.PrefetchScalarGridSpec(
        num_scalar_prefetch=2,
        grid=(DA_HEADS // hp, len(pairs)),
        in_specs=[
            pl.BlockSpec(memory_space=pltpu.SMEM),
            pl.BlockSpec((tile, width), lambda h, s, qt, kt: (qt[s], h)),
            pl.BlockSpec((tile, width), lambda h, s, qt, kt: (kt[s], h)),
            pl.BlockSpec((tile, width), lambda h, s, qt, kt: (kt[s], vb + h)),
            pl.BlockSpec((hp, 2, tile, tile), lambda h, s, qt, kt: (h, 0, 0, 0)),
            pl.BlockSpec((1, DA_DV), lambda h, s, qt, kt: (0, 0)),
        ],
        out_specs=pl.BlockSpec((tile, width), lambda h, s, qt, kt: (qt[s], h)),
        scratch_shapes=[pltpu.VMEM((2 * hp, tile, 128), F32), pltpu.VMEM((2 * hp, tile, 128), F32),
                        pltpu.VMEM((2 * hp, tile, DA_DV), F32)],
    )
    return pl.pallas_call(
        functools.partial(_fa_kernel, post_scale=post_scale),
        grid_spec=grid_spec,
        out_shape=jax.ShapeDtypeStruct((s_len, DA_VW), BF16),
        compiler_params=_cparams("parallel", "arbitrary"),
        name="prompt_diff_attention",
    )(qt, kt, lam.reshape(1).astype(F32), qn, kn, v_src, bias_tiles, g_sub.reshape(1, DA_DV).astype(F32))


def _topk_rows(s, k):
    riota = lax.broadcasted_iota(jnp.int32, s.shape, 0).astype(F32)
    big = float(s.shape[0])
    vals, idxs = [], []
    for _ in range(k):
        m = jnp.max(s, axis=0, keepdims=True)
        idx = jnp.min(jnp.where(s == m, riota, big), axis=0, keepdims=True)
        vals.append(m)
        idxs.append(idx)
        s = jnp.where(riota == idx, -jnp.inf, s)
    return vals, idxs


def _pruned_pair_candidates(v1, i1, v2, i2):
    k = PEER_TOPK
    tt = v1[0].shape[1]
    cat = lambda rows: jnp.concatenate(rows, axis=0)
    v1m, i1m, v2m, i2m = cat(v1), cat(i1), cat(v2), cat(i2)
    row16 = lax.broadcasted_iota(jnp.int32, (k, tt), 0)
    row8 = lax.broadcasted_iota(jnp.int32, (k // 2, tt), 0)
    half = k // 2
    cand = [v1[0] + v2m]
    c1 = [jnp.broadcast_to(i1[0], (k, tt))]
    c2 = [i2m]
    rank = [row16]
    for a in range(1, half):
        cand.append(jnp.where(row8 < k // (a + 1), v1[a] + v2m[:half], -jnp.inf))
        c1.append(jnp.broadcast_to(i1[a], (half, tt)))
        c2.append(i2m[:half])
        rank.append(a * k + row8)
    cand.append(v1m[half:] + v2[0])
    c1.append(i1m[half:])
    c2.append(jnp.broadcast_to(i2[0], (half, tt)))
    rank.append((half + row8) * k)
    return cat(cand), cat(c1), cat(c2), cat(rank).astype(F32)


def _peer_route_kernel(q_ref, k1_ref, k2_ref, gate_ref, e1_ref, e2_ref):
    k1 = k1_ref[...]
    k2 = k2_ref[...]
    gates, e1s, e2s = [], [], []
    for h in range(PEER_HEADS):
        qa = q_ref[:, (2 * h) * PEER_HALF:(2 * h + 1) * PEER_HALF].astype(BF16)
        qb = q_ref[:, (2 * h + 1) * PEER_HALF:(2 * h + 2) * PEER_HALF].astype(BF16)
        v1, i1 = _topk_rows(_dot_nt(k1, qa), PEER_TOPK)
        v2, i2 = _topk_rows(_dot_nt(k2, qb), PEER_TOPK)
        cand, c1, c2, rank = _pruned_pair_candidates(v1, i1, v2, i2)
        sc, ea, eb = [], [], []
        for _ in range(PEER_TOPK):
            m = jnp.max(cand, axis=0, keepdims=True)
            first = jnp.min(jnp.where(cand == m, rank, float(PEER_TOPK * PEER_TOPK)), axis=0, keepdims=True)
            sel = rank == first
            sc.append(m)
            ea.append(jnp.sum(jnp.where(sel, c1, 0.0), axis=0, keepdims=True))
            eb.append(jnp.sum(jnp.where(sel, c2, 0.0), axis=0, keepdims=True))
            cand = jnp.where(sel, -jnp.inf, cand)
        scm = jnp.concatenate(sc, axis=0)
        ex = jnp.exp(scm - sc[0])
        gates.append(ex / jnp.sum(ex, axis=0, keepdims=True))
        e1s.append(jnp.concatenate(ea, axis=0))
        e2s.append(jnp.concatenate(eb, axis=0))
    gate_ref[...] = jnp.concatenate(gates, axis=0).T
    e1_ref[...] = jnp.concatenate(e1s, axis=0).T
    e2_ref[...] = jnp.concatenate(e2s, axis=0).T


def peer_route(q, k1, k2):
    t = q.shape[0]
    tt = 128
    hk = PEER_HEADS * PEER_TOPK
    out = jax.ShapeDtypeStruct((t, hk), F32)
    ospec = pl.BlockSpec((tt, hk), lambda i: (i, 0))
    kspec = pl.BlockSpec((PEER_KEYS, PEER_HALF), lambda i: (0, 0))
    return pl.pallas_call(
        _peer_route_kernel,
        grid=(t // tt,),
        in_specs=[pl.BlockSpec((tt, q.shape[1]), lambda i: (i, 0)), kspec, kspec],
        out_specs=[ospec, ospec, ospec],
        out_shape=[out, out, out],
        compiler_params=_cparams("parallel"),
        name="peer_route",
    )(q, k1, k2)


def _peer_w_kernel(gate_ref, e1_ref, e2_ref, w_ref):
    tb = gate_ref.shape[0]
    sub = lax.broadcasted_iota(jnp.int32, (PEER_KEYS, gate_ref.shape[1]), 0).astype(F32)

    def body(t, carry):
        g = gate_ref[pl.ds(t, 1), :]
        a = e1_ref[pl.ds(t, 1), :]
        b = e2_ref[pl.ds(t, 1), :]
        lhs = jnp.where(a == sub, g, 0.0).astype(BF16)
        rhs = jnp.where(b == sub, 1.0, 0.0).astype(BF16)
        w_ref[t] = _dot_nt(lhs, rhs)
        return carry

    lax.fori_loop(0, tb, body, 0, unroll=16)


def peer_dense_weights(gate, e1, e2):
    t, hk = gate.shape
    tb = 128
    spec = pl.BlockSpec((tb, hk), lambda i: (i, 0))
    return pl.pallas_call(
        _peer_w_kernel,
        grid=(t // tb,),
        in_specs=[spec, spec, spec],
        out_specs=pl.BlockSpec((tb, PEER_KEYS, PEER_KEYS), lambda i: (i, 0, 0)),
        out_shape=jax.ShapeDtypeStruct((t, PEER_KEYS, PEER_KEYS), F32),
        compiler_params=_cparams("parallel"),
        name="peer_dense_weights",
    )(gate, e1, e2)


PEER_KEY1_PER_STEP = 8


def _peer_ffn_kernel(x_ref, w_ref, u_ref, v_ref, r_ref, o_ref):
    e = pl.program_id(1)

    @pl.when(e == 0)
    def _():
        o_ref[...] = r_ref[...]

    a = _dot_nt(x_ref[...], u_ref[...])
    act = 0.5 * a * (1.0 + lax.erf(a * (2.0 ** -0.5)))
    hmat = jnp.concatenate(
        [(act[:, i * PEER_KEYS:(i + 1) * PEER_KEYS] * w_ref[:, i, :]).astype(BF16)
         for i in range(PEER_KEY1_PER_STEP)], axis=1)
    o_ref[...] += jnp.dot(hmat, v_ref[...], preferred_element_type=F32)


def peer_ffn_dense(xn, w, u_tab, v_tab, layer, resid):
    t, d = xn.shape
    n_exp = u_tab.shape[1]
    tb = _pick_tile(t, MXU_ROW_TILES)
    eb = PEER_KEY1_PER_STEP * PEER_KEYS
    return pl.pallas_call(
        _peer_ffn_kernel,
        grid=(t // tb, n_exp // eb),
        in_specs=[pl.BlockSpec((tb, d), lambda i, e: (i, 0), pipeline_mode=ONCE),
                  pl.BlockSpec((tb, PEER_KEY1_PER_STEP, PEER_KEYS), lambda i, e: (i, e, 0)),
                  pl.BlockSpec((None, eb, d), lambda i, e: (layer, e, 0)),
                  pl.BlockSpec((None, eb, d), lambda i, e: (layer, e, 0)),
                  pl.BlockSpec((tb, d), lambda i, e: (i, 0), pipeline_mode=ONCE)],
        out_specs=pl.BlockSpec((tb, d), lambda i, e: (i, 0), pipeline_mode=ONCE),
        out_shape=jax.ShapeDtypeStruct((t, d), F32),
        compiler_params=_cparams("parallel", "arbitrary"),
        name="peer_ffn_dense",
    )(xn, w, u_tab, v_tab, resid)


def peer_layer(x, g_ffn, wq, k1, k2, u_tab, v_tab, layer):
    q, xn = norm_matmul(x, g_ffn, wq, emit_h=True, name="peer_query")
    gate, e1, e2 = peer_route(q, k1, k2)
    w = peer_dense_weights(gate, e1, e2)
    return peer_ffn_dense(xn, w, u_tab, v_tab, layer, x)


def _group_norm_kernel(x_ref, bd_ref, g_ref, *out_refs, inv_group, scales):
    x = x_ref[...]
    sq = x * x
    hi = sq.astype(BF16)
    lo = (sq - hi.astype(F32)).astype(BF16)
    bd = bd_ref[...]
    ss = jnp.dot(hi, bd, preferred_element_type=F32) + jnp.dot(lo, bd, preferred_element_type=F32)
    y = x * lax.rsqrt(ss * inv_group + NORM_EPS) * g_ref[...]
    for o_ref, sc in zip(out_refs, scales):
        o_ref[...] = (y if sc == 1.0 else y * sc).astype(o_ref.dtype)


def group_norm(src, col0, width, group, gain, outs):
    t = src.shape[0]
    assert col0 % width == 0 and width % group == 0
    tm = _pick_tile(t, (528, 512, 256, 128))
    lane = jnp.arange(width, dtype=jnp.int32) // group
    bd = (lane[:, None] == lane[None, :]).astype(BF16)
    g = jnp.tile(gain.astype(F32), width // group).reshape(1, width)
    ospec = pl.BlockSpec((tm, width), lambda i: (i, 0))
    res = pl.pallas_call(
        functools.partial(_group_norm_kernel, inv_group=1.0 / group, scales=tuple(s for _, s in outs)),
        grid=(t // tm,),
        in_specs=[pl.BlockSpec((tm, width), lambda i: (i, col0 // width)),
                  pl.BlockSpec((width, width), lambda i: (0, 0)),
                  pl.BlockSpec((1, width), lambda i: (0, 0))],
        out_specs=[ospec] * len(outs),
        out_shape=[jax.ShapeDtypeStruct((t, width), dt) for dt, _ in outs],
        compiler_params=_cparams("parallel"),
        name="group_norm",
    )(src, bd, g)
    return res


CONV_HALO = 8


def _conv_kernel(x_ref, halo_ref, w_ref, b_ref, o_ref, sc, *, first_tile_has_no_history):
    tm = x_ref.shape[0]
    halo = halo_ref[...]
    if first_tile_has_no_history:
        halo = jnp.where(pl.program_id(0) == 0, 0.0, halo)
    sc[0:CONV_HALO, :] = halo
    sc[CONV_HALO:, :] = x_ref[...]
    acc = b_ref[...]
    for j in range(SSM_CONV):
        acc = acc + w_ref[j:j + 1, :] * sc[pl.ds(CONV_HALO - (SSM_CONV - 1 - j), tm), :]
    o_ref[...] = jax.nn.silu(acc)


def causal_conv_silu(src, col0, width, row0, rows, tm, tc, halo_src, conv_w, conv_b):
    assert col0 % tc == 0 and width % tc == 0 and row0 % tm == 0 and rows % tm == 0 and tm % CONV_HALO == 0
    cb, rb, hb = col0 // tc, row0 // tm, tm // CONV_HALO
    if halo_src is None:
        halo_arr = src
        halo_spec = pl.BlockSpec((CONV_HALO, tc), lambda i, j: (jnp.maximum((rb + i) * hb - 1, 0), cb + j))
    else:
        halo_arr = halo_src
        halo_spec = pl.BlockSpec((CONV_HALO, tc), lambda i, j: (i, j))
    return pl.pallas_call(
        functools.partial(_conv_kernel, first_tile_has_no_history=halo_src is None),
        grid=(rows // tm, width // tc),
        in_specs=[pl.BlockSpec((tm, tc), lambda i, j: (rb + i, cb + j)),
                  halo_spec,
                  pl.BlockSpec((SSM_CONV, tc), lambda i, j: (0, j)),
                  pl.BlockSpec((1, tc), lambda i, j: (0, j))],
        out_specs=pl.BlockSpec((tm, tc), lambda i, j: (i, j)),
        out_shape=jax.ShapeDtypeStruct((rows, width), F32),
        scratch_shapes=[pltpu.VMEM((tm + CONV_HALO, tc), F32)],
        compiler_params=_cparams("parallel", "parallel"),
        name="causal_conv_silu",
    )(src, halo_arr, conv_w.astype(F32), conv_b.reshape(1, width).astype(F32))


SSM_INNER = SSM_HEADS * SSM_HEAD_DIM
SSM_GN = SSM_GROUPS * SSM_STATE
HEADS_PER_GROUP = SSM_HEADS // SSM_GROUPS
GROUP_W = HEADS_PER_GROUP * SSM_HEAD_DIM


def _ssd_kernel(xs_ref, b_ref, c_ref, z_ref, dt_ref, dtt_ref, dtb_row_ref, dtb_col_ref, a_row_ref, a_col_ref,
                e_ref, dsk_ref, g_ref, h0_ref, y_ref, hout_ref, h_sc, y_sc, *pad_scs, valid):
    q = SSM_CHUNK
    hi = lax.Precision.HIGHEST
    c = pl.program_id(1)

    @pl.when(c == 0)
    def _():
        h_sc[...] = h0_ref[...]

    def rows(ref, sc):
        if valid == q:
            return ref[...]
        sc[...] = jnp.zeros(sc.shape, F32)
        sc[0:valid, :] = ref[...]
        return sc[...]

    if valid == q:
        pad_scs = (None,) * 5
    xs = rows(xs_ref, pad_scs[0])
    bm = rows(b_ref, pad_scs[1])
    cm = rows(c_ref, pad_scs[2])
    z = rows(z_ref, pad_scs[3])
    dt_raw = rows(dt_ref, pad_scs[4])

    row_i = lax.broadcasted_iota(jnp.int32, (q, q), 0)
    col_i = lax.broadcasted_iota(jnp.int32, (q, q), 1)
    causal = row_i >= col_i
    dt = jax.nn.softplus(dt_raw + dtb_row_ref[...])
    dtt = jax.nn.softplus(dtt_ref[...] + dtb_col_ref[...])
    if valid < q:
        dt = jnp.where(row_i < valid, dt, 0.0)
        dtt = jnp.where(lax.broadcasted_iota(jnp.int32, dtt.shape, 1) < valid, dtt, 0.0)
    acs = jnp.dot(causal.astype(F32), dt * a_row_ref[...], precision=hi, preferred_element_type=F32)
    acst = jnp.dot(dtt * a_col_ref[...], (row_i <= col_i).astype(F32), precision=hi,
                   preferred_element_type=F32)
    per_head = jnp.concatenate([dt, jnp.exp(acs[q - 1:q, :] - acs), jnp.exp(acs)], axis=0)
    expand = e_ref[...]
    piece = per_head.astype(BF16)
    expanded = jnp.dot(piece, expand, preferred_element_type=F32)
    rest = per_head - piece.astype(F32)
    for _ in range(2):
        piece = rest.astype(BF16)
        expanded = expanded + jnp.dot(piece, expand, preferred_element_type=F32)
        rest = rest - piece.astype(F32)
    dt_e, dec_e, eacs_e = expanded[0:q], expanded[q:2 * q], expanded[2 * q:3 * q]
    xdt = xs * dt_e
    xdd = xdt * dec_e
    xdt_b = xdt.astype(BF16)
    lane = lax.broadcasted_iota(jnp.int32, (q, 2 * SSM_HEAD_DIM), 1)

    for g in range(SSM_GROUPS):
        bg = bm[:, g * SSM_STATE:(g + 1) * SSM_STATE].astype(BF16)
        cg = cm[:, g * SSM_STATE:(g + 1) * SSM_STATE].astype(BF16)
        cb = _dot_nt(cg, bg)
        for pair in range(HEADS_PER_GROUP // 2):
            h_a = g * HEADS_PER_GROUP + 2 * pair
            slab = slice(h_a * SSM_HEAD_DIM, (h_a + 2) * SSM_HEAD_DIM)
            xpair = xdt_b[:, slab]
            ypair = jnp.zeros((q, 2 * SSM_HEAD_DIM), F32)
            for which in range(2):
                h = h_a + which
                seg = acs[:, h:h + 1] - acst[h:h + 1, :]
                lmat = jnp.exp(jnp.where(causal, seg, -jnp.inf))
                mine = (lane >= which * SSM_HEAD_DIM) & (lane < (which + 1) * SSM_HEAD_DIM)
                ypair = ypair + jnp.dot((cb * lmat).astype(BF16), jnp.where(mine, xpair, jnp.zeros_like(xpair)),
                                        preferred_element_type=F32)
            y_sc[:, slab] = ypair
        gs = slice(g * GROUP_W, (g + 1) * GROUP_W)
        y_sc[:, gs] += _dot_nt(cg, h_sc[gs, :].astype(BF16)) * eacs_e[:, gs]
        st = jnp.dot(xdd[:, gs].T.astype(BF16), bg, preferred_element_type=F32)
        for hh in range(HEADS_PER_GROUP):
            h = g * HEADS_PER_GROUP + hh
            hs = slice(h * SSM_HEAD_DIM, (h + 1) * SSM_HEAD_DIM)
            dec = jnp.exp(acst[h:h + 1, q - 1:q])
            h_sc[hs, :] = h_sc[hs, :] * dec + st[hh * SSM_HEAD_DIM:(hh + 1) * SSM_HEAD_DIM, :]

    y = y_sc[...] + dsk_ref[...] * xs
    y = y * jax.nn.silu(z)
    gain = g_ref[...]
    for g in range(SSM_GROUPS):
        gs = slice(g * GROUP_W, (g + 1) * GROUP_W)
        yg = y[:, gs]
        yn = yg * lax.rsqrt(jnp.mean(yg * yg, axis=-1, keepdims=True) + NORM_EPS) * gain[:, gs]
        y_ref[:, gs] = yn[0:valid].astype(y_ref.dtype)

    @pl.when(c == pl.num_programs(1) - 1)
    def _():
        hout_ref[...] = h_sc[...]


def ssd_branch(xc, u, z_col0, u_row0, dt_pad, dt_t, h0, h0_blk0, n_batch, n_chunks, valid, dt_bias, a_log, d_skip,
               g_ssm, out_dtype):
    q = SSM_CHUNK
    rows = xc.shape[0]
    assert rows == n_batch * n_chunks * valid and u_row0 % valid == 0 and z_col0 % SSM_INNER == 0
    rb = u_row0 // valid
    step = lambda b, c: b * n_chunks + c
    pad128 = lambda v: jnp.pad(v.astype(F32), (0, 128 - SSM_HEADS))
    a = -jnp.exp(a_log.astype(F32))
    head_lane = jnp.arange(SSM_INNER, dtype=jnp.int32) // SSM_HEAD_DIM
    expand = (jnp.arange(128, dtype=jnp.int32)[:, None] == head_lane[None, :]).astype(BF16)
    const = lambda shape: pl.BlockSpec(shape, lambda b, c: (0,) * len(shape))
    pad_scs = [] if valid == q else [pltpu.VMEM((q, w), F32) for w in (SSM_INNER, SSM_GN, SSM_GN, SSM_INNER, 128)]
    y, h_fin = pl.pallas_call(
        functools.partial(_ssd_kernel, valid=valid),
        grid=(n_batch, n_chunks),
        in_specs=[
            pl.BlockSpec((valid, SSM_INNER), lambda b, c: (step(b, c), 0)),
            pl.BlockSpec((valid, SSM_GN), lambda b, c: (step(b, c), SSM_INNER // SSM_GN)),
            pl.BlockSpec((valid, SSM_GN), lambda b, c: (step(b, c), SSM_INNER // SSM_GN + 1)),
            pl.BlockSpec((valid, SSM_INNER), lambda b, c: (rb + step(b, c), z_col0 // SSM_INNER)),
            pl.BlockSpec((valid, 128), lambda b, c: (rb + step(b, c), 0)),
            pl.BlockSpec((None, SSM_HEADS, q), lambda b, c: (step(b, c), 0, 0)),
            const((1, 128)), const((SSM_HEADS, q)), const((1, 128)), const((SSM_HEADS, q)),
            const((128, SSM_INNER)), const((1, SSM_INNER)), const((1, SSM_INNER)),
            pl.BlockSpec((None, SSM_INNER, SSM_STATE), lambda b, c: (h0_blk0 + b, 0, 0)),
        ],
        out_specs=[pl.BlockSpec((valid, SSM_INNER), lambda b, c: (step(b, c), 0)),
                   pl.BlockSpec((None, SSM_INNER, SSM_STATE), lambda b, c: (b, 0, 0))],
        out_shape=[jax.ShapeDtypeStruct((rows, SSM_INNER), out_dtype),
                   jax.ShapeDtypeStruct((n_batch, SSM_INNER, SSM_STATE), F32)],
        scratch_shapes=[pltpu.VMEM((SSM_INNER, SSM_STATE), F32), pltpu.VMEM((q, SSM_INNER), F32)] + pad_scs,
        compiler_params=_cparams("parallel", "arbitrary"),
        name="ssd_branch",
    )(xc, xc, xc, u, dt_pad, dt_t,
      pad128(dt_bias).reshape(1, 128), jnp.broadcast_to(dt_bias.astype(F32)[:, None], (SSM_HEADS, q)),
      pad128(a).reshape(1, 128), jnp.broadcast_to(a[:, None], (SSM_HEADS, q)),
      expand, jnp.repeat(d_skip.astype(F32), SSM_HEAD_DIM).reshape(1, SSM_INNER),
      g_ssm.astype(F32).reshape(1, SSM_INNER), h0)
    return y, h_fin


def _mem_attn_kernel(q_ref, k_ref, v_ref, o_ref):
    for h in range(MEM_HEADS):
        hs = slice(h * MEM_HEAD_DIM, (h + 1) * MEM_HEAD_DIM)
        s = _dot_nt(q_ref[:, hs].astype(BF16), k_ref[:, hs].astype(BF16))
        p = jnp.exp(s - jnp.max(s, axis=-1, keepdims=True))
        o = jnp.dot(p.astype(BF16), v_ref[:, hs].astype(BF16), preferred_element_type=F32)
        o_ref[:, hs] = (o / jnp.sum(p, axis=-1, keepdims=True)).astype(o_ref.dtype)


def memory_attention(qn, q_row0, rows, tq, mk, mv, n_mem, kv_blk0, v_col_blk, tiles_per_batch, out_dtype):
    rb = q_row0 // tq
    return pl.pallas_call(
        _mem_attn_kernel,
        grid=(rows // tq,),
        in_specs=[pl.BlockSpec((tq, MEM_W), lambda i: (rb + i, 0)),
                  pl.BlockSpec((n_mem, MEM_W), lambda i: (kv_blk0 + i // tiles_per_batch, 0)),
                  pl.BlockSpec((n_mem, MEM_W), lambda i: (kv_blk0 + i // tiles_per_batch, v_col_blk))],
        out_specs=pl.BlockSpec((tq, MEM_W), lambda i: (i, 0)),
        out_shape=jax.ShapeDtypeStruct((rows, MEM_W), out_dtype),
        compiler_params=_cparams("parallel"),
        name="memory_attention",
    )(qn, mk, mv)


SAMPLE_PAGES_PER_STEP = 16
SAMPLE_ROWS = DA_HEADS * 2 * 8


def _sample_attn_kernel(pt_ref, lam_ref, q_ref, *refs, pages_per_step, post_scale):
    n = pages_per_step
    kt_refs, v_refs = refs[:n], refs[n:2 * n]
    knew_ref, vnew_ref, blast_ref, bnew_ref, rexp_ref, hmask_ref, g_ref, o_ref, m_sc, l_sc, acc_sc = refs[2 * n:]
    del pt_ref
    step = pl.program_id(1)
    n_new = o_ref.shape[0]
    rows_per_head = 2 * n_new

    @pl.when(step == 0)
    def _():
        m_sc[...] = jnp.full(m_sc.shape, NEG_BIG, F32)
        l_sc[...] = jnp.zeros(l_sc.shape, F32)
        acc_sc[...] = jnp.zeros(acc_sc.shape, F32)

    q = q_ref[...]

    def update(pages):
        logits = []
        for kt_ref, _, bias in pages:
            s = jnp.dot(q, kt_ref[...].astype(BF16), preferred_element_type=F32)
            logits.append(s if bias is None else s + bias)
        s_max = logits[0]
        for s in logits[1:]:
            s_max = jnp.maximum(s_max, s)
        m_prev = m_sc[...]
        m_new = jnp.maximum(m_prev, jnp.max(s_max, axis=-1, keepdims=True))
        alpha = jnp.exp2(m_prev - m_new)
        m_sc[...] = m_new
        l_new = alpha * l_sc[...]
        acc = alpha * acc_sc[...]
        rexp = rexp_ref[...]
        own_head = hmask_ref[...]
        for (_, v_ref, _), s in zip(pages, logits):
            p = jnp.exp2(s - m_new)
            l_new = l_new + p
            p_rep = jnp.dot(p.astype(BF16), rexp, preferred_element_type=F32)
            acc = acc + jnp.dot(p_rep.astype(BF16) * own_head, v_ref[...].astype(BF16),
                                preferred_element_type=F32)
        l_sc[...] = l_new
        acc_sc[...] = acc

    last = pl.num_programs(1) - 1
    past = [(kt_refs[i], v_refs[i], None) for i in range(n)]

    @pl.when(step < last)
    def _():
        update(past)

    @pl.when(step == last)
    def _():
        update(past[:-1] + [(kt_refs[n - 1], v_refs[n - 1], blast_ref[...]), (knew_ref, vnew_ref, bnew_ref[...])])
        attn = acc_sc[...] / jnp.sum(l_sc[...], axis=-1, keepdims=True)
        for h in range(DA_HEADS):
            r0 = h * rows_per_head
            o = attn[r0:r0 + n_new, :] - lam_ref[0] * attn[r0 + n_new:r0 + 2 * n_new, :]
            y = o * lax.rsqrt(jnp.mean(o * o, axis=-1, keepdims=True) + NORM_EPS)
            o_ref[:, h * DA_DV:(h + 1) * DA_DV] = (y * g_ref[...]) * post_scale


def sample_bias_tiles(table, n_new, page):
    def rows(off):
        b = _toeplitz_bias(table, off, page)[:, :n_new, :]
        return jnp.broadcast_to(b[:, None], (DA_HEADS, 2, n_new, page)).reshape(SAMPLE_ROWS, page)
    return rows(page), rows(0)


def sample_diff_attention(layer, qn_s, kn_s, v_s, lam, k_pool_t, v_pool, page_table, b_last, b_new, g_sub,
                          post_scale):
    bsz, n_new, qw = qn_s.shape
    page = v_pool.shape[2]
    n_pages = page_table.shape[1]
    n = SAMPLE_PAGES_PER_STEP
    assert n_pages % n == 0 and SAMPLE_ROWS == DA_HEADS * 2 * n_new
    qg = qn_s.reshape(bsz, n_new, DA_HEADS * 2, DA_DK)
    eye = jnp.eye(DA_HEADS * 2, dtype=F32)
    q_rows = (qg.transpose(0, 2, 1, 3)[:, :, :, None, :] * eye[None, :, None, :, None]).reshape(
        bsz, SAMPLE_ROWS, qw).astype(BF16)
    knew_t = jnp.pad(kn_s.transpose(0, 2, 1), ((0, 0), (0, 0), (0, page - n_new)))
    vnew = jnp.pad(v_s, ((0, 0), (0, page - n_new), (0, 0), (0, 0))).reshape(bsz, page * DA_HEADS, DA_DV)
    v_rows = v_pool.reshape(v_pool.shape[0], v_pool.shape[1], page * DA_HEADS, DA_DV)
    col = jnp.arange(page * DA_HEADS, dtype=jnp.int32)
    rexp = (col[None, :] // DA_HEADS == jnp.arange(page, dtype=jnp.int32)[:, None]).astype(BF16)
    hmask = (col[None, :] % DA_HEADS
             == jnp.arange(SAMPLE_ROWS, dtype=jnp.int32)[:, None] // (2 * n_new)).astype(BF16)
    kspec = lambda i: pl.BlockSpec((None, None, qw, page), lambda b, s, pt: (layer, pt[b, s * n + i], 0, 0))
    vspec = lambda i: pl.BlockSpec((None, None, page * DA_HEADS, DA_DV),
                                   lambda b, s, pt: (layer, pt[b, s * n + i], 0, 0))
    const2 = lambda shape: pl.BlockSpec(shape, lambda b, s, pt: (0, 0))
    grid_spec = pltpu.PrefetchScalarGridSpec(
        num_scalar_prefetch=1,
        grid=(bsz, n_pages // n),
        in_specs=[pl.BlockSpec(memory_space=pltpu.SMEM),
                  pl.BlockSpec((None, SAMPLE_ROWS, qw), lambda b, s, pt: (b, 0, 0))]
                 + [kspec(i) for i in range(n)] + [vspec(i) for i in range(n)]
                 + [pl.BlockSpec((None, qw, page), lambda b, s, pt: (b, 0, 0)),
                    pl.BlockSpec((None, page * DA_HEADS, DA_DV), lambda b, s, pt: (b, 0, 0)),
                    const2((SAMPLE_ROWS, page)), const2((SAMPLE_ROWS, page)),
                    const2((page, page * DA_HEADS)), const2((SAMPLE_ROWS, page * DA_HEADS)), const2((1, DA_DV))],
        out_specs=pl.BlockSpec((n_new, DA_VW), lambda b, s, pt: (b, 0)),
        scratch_shapes=[pltpu.VMEM((SAMPLE_ROWS, page), F32), pltpu.VMEM((SAMPLE_ROWS, page), F32),
                        pltpu.VMEM((SAMPLE_ROWS, DA_DV), F32)],
    )
    return pl.pallas_call(
        functools.partial(_sample_attn_kernel, pages_per_step=n, post_scale=post_scale),
        grid_spec=grid_spec,
        out_shape=jax.ShapeDtypeStruct((bsz * n_new, DA_VW), F32),
        compiler_params=_cparams("parallel", "arbitrary"),
        name="sample_diff_attention",
    )(page_table, lam.reshape(1).astype(F32), q_rows, *([k_pool_t] * n), *([v_rows] * n), knew_t, vnew,
      b_last, b_new, rexp, hmask, g_sub.reshape(1, DA_DV).astype(F32))


FA_TILE = 512


def kernel(x_prompt, x_sample, cache_attn_k, cache_attn_v, cache_mem_k, cache_mem_v, state_ssm, state_conv, page_table, mem_prompt, rel_bias, g_mix, w_in, g_q, g_k, lam_q1, lam_k1, lam_q2, lam_k2, g_sub, conv_w, conv_b, dt_bias, a_log, d_skip, g_ssm, g_mem, w_mem_kv, g_mq, g_mk, w_br_attn, w_br_ssm, w_br_mem, w_out, g_ffn, peer_wq, peer_k1, peer_k2, peer_u, peer_v):
    depth = w_in.shape[0]
    bp, sp, d = x_prompt.shape
    bs, ts, _ = x_sample.shape
    assert bp == 1 and sp % SSM_CHUNK == 0 and sp % FA_TILE == 0 and ts >= SSM_CONV - 1
    n_p = bp * sp
    n_s = bs * ts
    conv_dim = SSM_INNER + 2 * SSM_GN
    n_mem = mem_prompt.shape[1]
    n_pool, page = cache_attn_v.shape[1], cache_attn_v.shape[2]
    s_z = 2 * DA_QW + DA_VW
    s_dt = s_z + SSM_INNER + conv_dim
    s_mq = s_dt + SSM_HEADS
    c_z, c_xbc = 0, SSM_INNER
    c_q = c_xbc + conv_dim
    c_k = c_q + DA_QW
    c_v = c_k + DA_QW
    c_mq = c_v + DA_VW
    c_gate = c_mq + MEM_W

    x = jnp.concatenate([x_prompt.reshape(n_p, d), x_sample.reshape(n_s, d)], axis=0)
    bias_tiles = prompt_bias_tiles(rel_bias, FA_TILE)
    b_last, b_new = sample_bias_tiles(rel_bias, ts, page)
    k_pool_t = jnp.transpose(cache_attn_k, (0, 1, 3, 4, 5, 2)).reshape(depth, n_pool, DA_QW, page)
    zero_h = jnp.zeros((bp, SSM_INNER, SSM_STATE), F32)
    ssm_states = state_ssm.reshape(depth * bs, SSM_INNER, SSM_STATE)
    mem_k_rows = cache_mem_k.reshape(depth * bs * n_mem, MEM_W)
    mem_v_rows = cache_mem_v.reshape(depth * bs * n_mem, MEM_W)
    peer_u_b = peer_u.astype(BF16)
    peer_v_b = peer_v.astype(BF16)

    outs = {k: [] for k in ("pk", "pv", "pmk", "pmv", "ph", "pc", "sk", "sv", "sh", "sc")}
    for l in range(depth):
        lam_init = 0.8 - 0.6 * math.exp(-0.3 * l)
        lam = (jnp.exp(jnp.sum(lam_q1[l].astype(F32) * lam_k1[l].astype(F32)))
               - jnp.exp(jnp.sum(lam_q2[l].astype(F32) * lam_k2[l].astype(F32))) + lam_init)
        wl = w_in[l]
        w_main = jnp.concatenate([wl[:, s_z:s_dt], wl[:, :s_z], wl[:, s_mq:]], axis=1).astype(BF16)
        w_dt = jnp.pad(wl[:, s_dt:s_mq], ((0, 0), (0, 128 - SSM_HEADS))).astype(BF16)

        u, dt_pad = norm_matmul(x, g_mix[l], w_main, side_w=w_dt, name="in_proj")
        (qn,) = group_norm(u, c_q, DA_QW, DA_DK, g_q[l], [(BF16, DA_SCALE * LOG2E)])
        kn, kn_b = group_norm(u, c_k, DA_QW, DA_DK, g_k[l], [(F32, 1.0), (BF16, 1.0)])
        (mqn,) = group_norm(u, c_mq, MEM_W, MEM_HEAD_DIM, g_mq[l], [(F32, MEM_SCALE)])
        v_all = u[:, c_v:c_v + DA_VW]
        xbc_s = u[n_p:, c_xbc:c_xbc + conv_dim].reshape(bs, ts, conv_dim)

        a_p = prompt_diff_attention(qn, kn_b, u, c_v, n_p, lam, bias_tiles, g_sub[l], 1.0 - lam_init, FA_TILE)
        a_s = sample_diff_attention(l, qn[n_p:].astype(F32).reshape(bs, ts, DA_QW), kn[n_p:].reshape(bs, ts, DA_QW),
                                    v_all[n_p:].reshape(bs, ts, DA_HEADS, DA_DV), lam, k_pool_t, cache_attn_v,
                                    page_table, b_last, b_new, g_sub[l], 1.0 - lam_init)

        (mkv,) = norm_matmul(mem_prompt.reshape(bp * n_mem, d), g_mem[l], w_mem_kv[l].astype(BF16), name="mem_kv")
        (mk_p,) = group_norm(mkv, 0, MEM_W, MEM_HEAD_DIM, g_mk[l], [(F32, 1.0)])
        tq = 512
        m_p = memory_attention(mqn, 0, n_p, tq, mk_p, mkv, n_mem, 0, 1, sp // tq, BF16)
        m_s = memory_attention(mqn, n_p, n_s, ts, mem_k_rows, mem_v_rows, n_mem, l * bs, 0, 1, F32)

        xc_p = causal_conv_silu(u, c_xbc, conv_dim, 0, n_p, 512, 1024, None, conv_w[l], conv_b[l])
        halo_s = jnp.pad(state_conv[l], ((0, 0), (CONV_HALO - (SSM_CONV - 1), 0), (0, 0))).reshape(
            bs * CONV_HALO, conv_dim)
        xc_s = causal_conv_silu(xbc_s.reshape(n_s, conv_dim), 0, conv_dim, 0, n_s, ts, conv_dim, halo_s,
                                conv_w[l], conv_b[l])
        dt_raw = dt_pad[:, :SSM_HEADS]
        dtt_p = dt_raw[:n_p].reshape(n_p // SSM_CHUNK, SSM_CHUNK, SSM_HEADS).transpose(0, 2, 1)
        dtt_s = jnp.pad(dt_raw[n_p:].reshape(bs, ts, SSM_HEADS).transpose(0, 2, 1),
                        ((0, 0), (0, 0), (0, SSM_CHUNK - ts)))
        s_p, h_p = ssd_branch(xc_p, u, c_z, 0, dt_pad, dtt_p, zero_h, 0, bp, sp // SSM_CHUNK, SSM_CHUNK,
                              dt_bias[l], a_log[l], d_skip[l], g_ssm[l], BF16)
        s_s, h_s = ssd_branch(xc_s, u, c_z, n_p, dt_pad, dtt_s, ssm_states, l * bs, bs, 1, ts,
                              dt_bias[l], a_log[l], d_skip[l], g_ssm[l], F32)

        a_out = jnp.concatenate([a_p, a_s.astype(BF16)], axis=0)
        s_out = jnp.concatenate([s_p, s_s.astype(BF16)], axis=0)
        m_out = jnp.concatenate([m_p, m_s.astype(BF16)], axis=0)
        merged = merge_branches(a_out, s_out, m_out, u, c_gate, w_br_attn[l].astype(BF16),
                                w_br_ssm[l].astype(BF16), w_br_mem[l].astype(BF16))
        x = matmul(merged, w_out[l].astype(BF16), residual=x, name="out_proj")
        x = peer_layer(x, g_ffn[l], peer_wq[l].astype(BF16), peer_k1[l].astype(BF16), peer_k2[l].astype(BF16),
                       peer_u_b, peer_v_b, l)

        outs["pk"].append(kn[:n_p].reshape(bp, sp, DA_HEADS, 2, DA_DK))
        outs["pv"].append(v_all[:n_p].reshape(bp, sp, DA_HEADS, DA_DV))
        outs["pmk"].append(mk_p.reshape(bp, n_mem, MEM_HEADS, MEM_HEAD_DIM))
        outs["pmv"].append(mkv[:, MEM_W:].reshape(bp, n_mem, MEM_HEADS, MEM_HEAD_DIM))
        outs["ph"].append(h_p.reshape(bp, SSM_HEADS, SSM_HEAD_DIM, SSM_STATE))
        outs["pc"].append(u[n_p - (SSM_CONV - 1):n_p, c_xbc:c_xbc + conv_dim].reshape(bp, SSM_CONV - 1, conv_dim))
        outs["sk"].append(kn[n_p:].reshape(bs, ts, DA_HEADS, 2, DA_DK))
        outs["sv"].append(v_all[n_p:].reshape(bs, ts, DA_HEADS, DA_DV))
        outs["sh"].append(h_s.reshape(bs, SSM_HEADS, SSM_HEAD_DIM, SSM_STATE))
        outs["sc"].append(xbc_s[:, ts - (SSM_CONV - 1):])

    y_prompt = x[:n_p].reshape(bp, sp, d)
    y_sample = x[n_p:].reshape(bs, ts, d)
    st = lambda name: jnp.stack(outs[name])
    return (y_prompt, y_sample, st("pk"), st("pv"), st("pmk"), st("pmv"), st("ph"), st("pc"),
            st("sk"), st("sv"), st("sh"), st("sc"))
```
